```python
import math
import jax
import jax.numpy as jnp
from jax import lax
import numpy as np

D_MODEL = 4096
BATCH = 4
SEQ = 4096
DEPTH = 2
DEC_BATCH = 8
DEC_SEQ = 2048
PAST_LEN = 128

EPS = 1e-6
NEG_INF = -1e30
N_BRANCH = 4
BRANCH_W = D_MODEL // N_BRANCH
HEAD_DIM = 128
Q_BLOCK = 128

MLA_NOPE = 128
MLA_ROPE = 64
MLA_V = 128
MLA_HEADS = BRANCH_W // MLA_V
MLA_Q_LORA = D_MODEL // 4
MLA_KV_LORA = D_MODEL // 8
ROPE_THETA = 10000.0

DIL_HEADS = BRANCH_W // HEAD_DIM
DIL_PATTERNS = ((128, 1), (512, 4), (2048, 16))

WIN_Q_HEADS = BRANCH_W // HEAD_DIM
WIN_KV_HEADS = WIN_Q_HEADS // 4
WIN_RADIUS = 128

DIFF_HD = 128
DIFF_HEADS = BRANCH_W // (2 * DIFF_HD)

MEM_TOKENS = 256
MEM_HEADS = 4
MEM_W = MEM_HEADS * HEAD_DIM

FFN_HIDDEN = -(-8 * D_MODEL // (3 * 256)) * 256

MIX_SPLITS = (MLA_Q_LORA, MLA_KV_LORA, MLA_ROPE,
              DIL_HEADS * HEAD_DIM, DIL_HEADS * HEAD_DIM, DIL_HEADS * HEAD_DIM,
              WIN_Q_HEADS * HEAD_DIM, WIN_KV_HEADS * HEAD_DIM, WIN_KV_HEADS * HEAD_DIM,
              DIFF_HEADS * 2 * DIFF_HD, DIFF_HEADS * 2 * DIFF_HD, DIFF_HEADS * 2 * DIFF_HD)
MIX_COLS = sum(MIX_SPLITS)
MIX_OFFSETS = tuple(int(o) for o in np.cumsum(MIX_SPLITS)[:-1])
IN_COLS = MIX_COLS + N_BRANCH * D_MODEL

kernel_name = 'hybrid_gated_mla_dilated_window_diff_encoder'


def rms_norm(x, g):
    xf = x.astype(jnp.float32)
    y = xf * lax.rsqrt(jnp.mean(xf * xf, axis=-1, keepdims=True) + EPS)
    return (y * g.astype(jnp.float32)).astype(x.dtype)


def alibi_slopes(n):
    return 2.0 ** (-8.0 * jnp.arange(1, n + 1, dtype=jnp.float32) / n)


def rotary(x, pos):
    half = x.shape[-1] // 2
    inv_freq = ROPE_THETA ** (-jnp.arange(half, dtype=jnp.float32) / half)
    ang = pos.astype(jnp.float32)[:, None] * inv_freq[None, :]
    cos = jnp.cos(ang)[:, None, :]
    sin = jnp.sin(ang)[:, None, :]
    xf = x.astype(jnp.float32)
    x1, x2 = xf[..., :half], xf[..., half:]
    return jnp.concatenate([x1 * cos - x2 * sin, x2 * cos + x1 * sin], axis=-1).astype(x.dtype)


def dense_attention(q, k, v):
    B, S, H, dk = q.shape
    nb = S // Q_BLOCK
    scale = dk ** -0.5
    qb = q.reshape(B, nb, Q_BLOCK, H, dk).swapaxes(0, 1)

    def block(qi):
        s = jnp.einsum('bqhd,bkhd->bhqk', qi, k, preferred_element_type=jnp.float32) * scale
        p = jax.nn.softmax(s, axis=-1)
        return jnp.einsum('bhqk,bkhd->bqhd', p.astype(v.dtype), v)

    o = lax.map(block, qb)
    return o.swapaxes(0, 1).reshape(B, S, H, v.shape[-1])


def differential_attention(q, k, v, slopes, lam):
    B, S, H, _, dh = q.shape
    nb = S // Q_BLOCK
    scale = dh ** -0.5
    qb = q.reshape(B, nb, Q_BLOCK, H, 2, dh).swapaxes(0, 1)
    kpos = jnp.arange(S)

    def block(args):
        qi, i = args
        qpos = i * Q_BLOCK + jnp.arange(Q_BLOCK)
        dist = jnp.abs(qpos[:, None] - kpos[None, :]).astype(jnp.float32)
        s = jnp.einsum('bqhmd,bkhmd->bhmqk', qi, k, preferred_element_type=jnp.float32) * scale
        s = s - slopes[None, :, None, None, None] * dist
        p = jax.nn.softmax(s, axis=-1)
        a = p[:, :, 0] - lam * p[:, :, 1]
        return jnp.einsum('bhqk,bkhd->bqhd', a.astype(v.dtype), v)

    o = lax.map(block, (qb, jnp.arange(nb)))
    return o.swapaxes(0, 1).reshape(B, S, H, v.shape[-1])


def banded_attention(q, k, v, radius, slopes, dist_scale, sink=None):
    N, L, H, dh = q.shape
    G = k.shape[2]
    rep = H // G
    blk = radius
    nb = -(-L // blk)
    pad = nb * blk - L
    scale = dh ** -0.5
    qb = jnp.pad(q, ((0, 0), (0, pad), (0, 0), (0, 0))).reshape(N, nb, blk, G, rep, dh)

    def key_windows(t):
        tb = jnp.pad(t, ((0, 0), (blk, blk + pad), (0, 0), (0, 0))).reshape(N, nb + 2, blk, G, dh)
        return jnp.concatenate([tb[:, :-2], tb[:, 1:-1], tb[:, 2:]], axis=2)

    kw, vw = key_windows(k), key_windows(v)
    a = jnp.arange(blk)
    c = jnp.arange(3 * blk)
    delta = c[None, :] - blk - a[:, None]
    kpos = jnp.arange(nb)[:, None] * blk + c[None, :] - blk
    valid = (jnp.abs(delta)[None] <= radius) & (kpos[:, None, :] >= 0) & (kpos[:, None, :] < L)
    bias = -(slopes * dist_scale).reshape(G, rep, 1, 1) * jnp.abs(delta).astype(jnp.float32)
    s = jnp.einsum('nbqgrd,nbkgd->nbgrqk', qb, kw, preferred_element_type=jnp.float32) * scale + bias
    s = jnp.where(valid[None, :, None, None], s, NEG_INF)
    m = jnp.max(s, axis=-1)
    if sink is not None:
        sk = sink.astype(jnp.float32).reshape(1, 1, G, rep, 1)
        m = jnp.maximum(m, sk)
    p = jnp.exp(s - m[..., None])
    denom = jnp.sum(p, axis=-1)
    if sink is not None:
        denom = denom + jnp.exp(sk - m)
    o = jnp.einsum('nbgrqk,nbkgd->nbqgrd', (p / denom[..., None]).astype(v.dtype), vw)
    lse = (m + jnp.log(denom)).transpose(0, 1, 4, 2, 3)
    o = o.reshape(N, nb * blk, H, dh)[:, :L]
    lse = lse.reshape(N, nb * blk, H)[:, :L]
    return o, lse


def mla_branch(q_lat, kv_lat, k_rope, pos, qa_g, kva_g, wq_up, wkv_up, qk_g):
    B, S, _ = q_lat.shape
    q = (rms_norm(q_lat, qa_g) @ wq_up).reshape(B, S, MLA_HEADS, MLA_NOPE + MLA_ROPE)
    kv = (rms_norm(kv_lat, kva_g) @ wkv_up).reshape(B, S, MLA_HEADS, MLA_NOPE + MLA_V)
    k_pe = rotary(k_rope.reshape(B, S, 1, MLA_ROPE), pos)
    q = jnp.concatenate([q[..., :MLA_NOPE], rotary(q[..., MLA_NOPE:], pos)], axis=-1)
    k = jnp.concatenate([kv[..., :MLA_NOPE],
                         jnp.broadcast_to(k_pe, (B, S, MLA_HEADS, MLA_ROPE))], axis=-1)
    v = kv[..., MLA_NOPE:]
    o = dense_attention(rms_norm(q, qk_g[0]), rms_norm(k, qk_g[1]), v)
    return o.reshape(B, S, BRANCH_W)


def to_residue(t, dil):
    B, S, H, d = t.shape
    return t.reshape(B, S // dil, dil, H, d).transpose(0, 2, 1, 3, 4).reshape(B * dil, S // dil, H, d)


def from_residue(t, B, dil):
    N, L = t.shape[:2]
    rest = t.shape[2:]
    t = t.reshape((B, dil, L) + rest)
    t = jnp.moveaxis(t, 1, 2)
    return t.reshape((B, L * dil) + rest)


def dilated_branch(q, k, v, qk_g, slopes):
    B, S, _ = q.shape
    shp = (B, S, DIL_HEADS, HEAD_DIM)
    q = rms_norm(q.reshape(shp), qk_g[0])
    k = rms_norm(k.reshape(shp), qk_g[1])
    v = v.reshape(shp)
    outs, lses = [], []
    for window, dil in DIL_PATTERNS:
        o, lse = banded_attention(to_residue(q, dil), to_residue(k, dil), to_residue(v, dil),
                                  window // (2 * dil), slopes, float(dil))
        outs.append(from_residue(o, B, dil))
        lses.append(from_residue(lse, B, dil))
    w = jax.nn.softmax(jnp.stack(lses), axis=0)
    o = jnp.einsum('pbsh,pbshd->bshd', w, jnp.stack(outs).astype(jnp.float32))
    return o.astype(q.dtype).reshape(B, S, BRANCH_W)


def window_branch(q, k, v, qk_g, sink, slopes):
    B, S, _ = q.shape
    q = rms_norm(q.reshape(B, S, WIN_Q_HEADS, HEAD_DIM), qk_g[0])
    k = rms_norm(k.reshape(B, S, WIN_KV_HEADS, HEAD_DIM), qk_g[1])
    v = v.reshape(B, S, WIN_KV_HEADS, HEAD_DIM)
    o, _ = banded_attention(q, k, v, WIN_RADIUS, slopes, 1.0, sink)
    return o.reshape(B, S, BRANCH_W)


def diff_branch(q, k, v, qk_g, lam_p, subln_g, slopes, layer):
    B, S, _ = q.shape
    lam_init = 0.8 - 0.6 * math.exp(-0.3 * layer)
    lp = lam_p.astype(jnp.float32)
    lam = jnp.exp(jnp.sum(lp[0] * lp[1])) - jnp.exp(jnp.sum(lp[2] * lp[3])) + lam_init
    q = rms_norm(q.reshape(B, S, DIFF_HEADS, 2, DIFF_HD), qk_g[0])
    k = rms_norm(k.reshape(B, S, DIFF_HEADS, 2, DIFF_HD), qk_g[1])
    v = v.reshape(B, S, DIFF_HEADS, 2 * DIFF_HD)
    o = differential_attention(q, k, v, slopes, lam)
    o = rms_norm(o, subln_g) * (1.0 - lam_init)
    return o.reshape(B, S, BRANCH_W)


def memory_attention(x, mem, ln_g, mem_ln_g, wq, wkv, wo, qk_g):
    B, S, _ = x.shape
    M = mem.shape[1]
    q = rms_norm((rms_norm(x, ln_g) @ wq).reshape(B, S, MEM_HEADS, HEAD_DIM), qk_g[0])
    kv = (rms_norm(mem, mem_ln_g) @ wkv).reshape(B, M, 2, MEM_HEADS, HEAD_DIM)
    k = rms_norm(kv[:, :, 0], qk_g[1])
    v = kv[:, :, 1]
    s = jnp.einsum('bshd,bmhd->bhsm', q, k, preferred_element_type=jnp.float32) * HEAD_DIM ** -0.5
    p = jax.nn.softmax(s, axis=-1)
    o = jnp.einsum('bhsm,bmhd->bshd', p.astype(v.dtype), v).reshape(B, S, MEM_W)
    return o @ wo


def swiglu(x, ln_g, w_in, w_out):
    g, u = jnp.split(rms_norm(x, ln_g) @ w_in, 2, axis=-1)
    return (jax.nn.silu(g) * u) @ w_out


def trunk(x, mem, ln_mix_g, w_in, mla_qa_g, mla_kva_g, mla_wq_up, mla_wkv_up, mla_qk_g,
          dil_qk_g, win_qk_g, win_sink, diff_qk_g, diff_lambda, diff_subln_g, w_branch, w_out,
          ln_mem_g, mem_ln_g, mem_wq, mem_wkv, mem_qk_g, mem_wo, ln_ffn_g, ffn_w_in, ffn_w_out):
    S = x.shape[1]
    pos = jnp.arange(S)
    slopes_dil = alibi_slopes(DIL_HEADS)
    slopes_win = alibi_slopes(WIN_Q_HEADS)
    slopes_diff = alibi_slopes(DIFF_HEADS)
    for l in range(DEPTH):
        h = rms_norm(x, ln_mix_g[l])
        (qa, kva, kpe, dq, dk, dv, wq, wk, wv, fq, fk, fv) = jnp.split(
            h @ w_in[l, :, :MIX_COLS], MIX_OFFSETS, axis=-1)
        branches = (
            mla_branch(qa, kva, kpe, pos, mla_qa_g[l], mla_kva_g[l], mla_wq_up[l], mla_wkv_up[l], mla_qk_g[l]),
            dilated_branch(dq, dk, dv, dil_qk_g[l], slopes_dil),
            window_branch(wq, wk, wv, win_qk_g[l], win_sink[l], slopes_win),
            diff_branch(fq, fk, fv, diff_qk_g[l], diff_lambda[l], diff_subln_g[l], slopes_diff, l),
        )
        merged = jnp.zeros_like(x)
        for i, o in enumerate(branches):
            g0 = MIX_COLS + i * D_MODEL
            gate = jax.nn.sigmoid(h @ w_in[l, :, g0:g0 + D_MODEL])
            merged = merged + gate * (o @ w_branch[l, i])
        x = x + merged @ w_out[l]
        x = x + memory_attention(x, mem, ln_mem_g[l], mem_ln_g[l], mem_wq[l], mem_wkv[l], mem_wo[l], mem_qk_g[l])
        x = x + swiglu(x, ln_ffn_g[l], ffn_w_in[l], ffn_w_out[l])
    return x


def setup_inputs(seed: int = 0) -> dict:
    key = jax.random.key(seed)
    ks = jax.random.split(key, 32)

    def nrm(k, shape, scale):
        return jax.random.normal(k, shape, jnp.float32) * scale

    def gain(k, shape):
        return 1.0 + 0.02 * jax.random.normal(k, shape, jnp.float32)

    return {
        'x_prompt': nrm(ks[0], (BATCH, SEQ, D_MODEL), 1.0),
        'x_sample': nrm(ks[1], (DEC_BATCH, DEC_SEQ, D_MODEL), 1.0),
        'mem_prompt': nrm(ks[2], (BATCH, MEM_TOKENS, D_MODEL), 1.0),
        'mem_sample': nrm(ks[3], (DEC_BATCH, MEM_TOKENS, D_MODEL), 1.0),
        'ln_mix_g': gain(ks[4], (DEPTH, D_MODEL)),
        'w_in': nrm(ks[5], (DEPTH, D_MODEL, IN_COLS), D_MODEL ** -0.5),
        'mla_qa_g': gain(ks[6], (DEPTH, MLA_Q_LORA)),
        'mla_kva_g': gain(ks[7], (DEPTH, MLA_KV_LORA)),
        'mla_wq_up': nrm(ks[8], (DEPTH, MLA_Q_LORA, MLA_HEADS * (MLA_NOPE + MLA_ROPE)), MLA_Q_LORA ** -0.5),
        'mla_wkv_up': nrm(ks[9], (DEPTH, MLA_KV_LORA, MLA_HEADS * (MLA_NOPE + MLA_V)), MLA_KV_LORA ** -0.5),
        'mla_qk_g': gain(ks[10], (DEPTH, 2, MLA_NOPE + MLA_ROPE)),
        'dil_qk_g': gain(ks[11], (DEPTH, 2, HEAD_DIM)),
        'win_qk_g': gain(ks[12], (DEPTH, 2, HEAD_DIM)),
        'win_sink': nrm(ks[13], (DEPTH, WIN_Q_HEADS), 0.5),
        'diff_qk_g': gain(ks[14], (DEPTH, 2, DIFF_HD)),
        'diff_lambda': nrm(ks[15], (DEPTH, 4, DIFF_HD), 0.1),
        'diff_subln_g': gain(ks[16], (DEPTH, 2 * DIFF_HD)),
        'w_branch': nrm(ks[17], (DEPTH, N_BRANCH, BRANCH_W, D_MODEL), BRANCH_W ** -0.5),
        'w_out': nrm(ks[18], (DEPTH, D_MODEL, D_MODEL), D_MODEL ** -0.5),
        'ln_mem_g': gain(ks[19], (DEPTH, D_MODEL)),
        'mem_ln_g': gain(ks[20], (DEPTH, D_MODEL)),
        'mem_wq': nrm(ks[21], (DEPTH, D_MODEL, MEM_W), D_MODEL ** -0.5),
        'mem_wkv': nrm(ks[22], (DEPTH, D_MODEL, 2 * MEM_W), D_MODEL ** -0.5),
        'mem_qk_g': gain(ks[23], (DEPTH, 2, HEAD_DIM)),
        'mem_wo': nrm(ks[24], (DEPTH, MEM_W, D_MODEL), MEM_W ** -0.5),
        'ln_ffn_g': gain(ks[25], (DEPTH, D_MODEL)),
        'ffn_w_in': nrm(ks[26], (DEPTH, D_MODEL, 2 * FFN_HIDDEN), D_MODEL ** -0.5),
        'ffn_w_out': nrm(ks[27], (DEPTH, FFN_HIDDEN, D_MODEL), FFN_HIDDEN ** -0.5),
    }


def reference(x_prompt, x_sample, mem_prompt, mem_sample, ln_mix_g, w_in, mla_qa_g, mla_kva_g,
              mla_wq_up, mla_wkv_up, mla_qk_g, dil_qk_g, win_qk_g, win_sink, diff_qk_g, diff_lambda,
              diff_subln_g, w_branch, w_out, ln_mem_g, mem_ln_g, mem_wq, mem_wkv, mem_qk_g, mem_wo,
              ln_ffn_g, ffn_w_in, ffn_w_out):
    params = (ln_mix_g, w_in, mla_qa_g, mla_kva_g, mla_wq_up, mla_wkv_up, mla_qk_g,
              dil_qk_g, win_qk_g, win_sink, diff_qk_g, diff_lambda, diff_subln_g, w_branch, w_out,
              ln_mem_g, mem_ln_g, mem_wq, mem_wkv, mem_qk_g, mem_wo, ln_ffn_g, ffn_w_in, ffn_w_out)
    y_prompt = trunk(x_prompt, mem_prompt, *params)
    y_sample = trunk(x_sample, mem_sample, *params)
    return (y_prompt, y_sample)
```

```python
import functools
import math

import jax
import jax.numpy as jnp
import numpy as np
from jax import lax
from jax.experimental import pallas as pl
from jax.experimental.pallas import tpu as pltpu

F32 = jnp.float32
BF16 = jnp.bfloat16

D_MODEL = 4096
DEPTH = 2
EPS = 1e-6
NEG_INF = -1e30
N_BRANCH = 4
BRANCH_W = D_MODEL // N_BRANCH
HEAD_DIM = 128

MLA_NOPE = 128
MLA_ROPE = 64
MLA_V = 128
MLA_HEADS = BRANCH_W // MLA_V
MLA_Q_LORA = D_MODEL // 4
MLA_KV_LORA = D_MODEL // 8
MLA_QK = MLA_NOPE + MLA_ROPE
MLA_QK_PAD = 256
ROPE_THETA = 10000.0

DIL_HEADS = BRANCH_W // HEAD_DIM
DIL_PATTERNS = ((128, 1), (512, 4), (2048, 16))
DIL_RADIUS = max(w // 2 for w, _ in DIL_PATTERNS)

WIN_Q_HEADS = BRANCH_W // HEAD_DIM
WIN_KV_HEADS = WIN_Q_HEADS // 4
WIN_RADIUS = 128

DIFF_HD = 128
DIFF_HEADS = BRANCH_W // (2 * DIFF_HD)

MEM_HEADS = 4
MEM_W = MEM_HEADS * HEAD_DIM

FFN_HIDDEN = -(-8 * D_MODEL // (3 * 256)) * 256

MIX_SPLITS = (MLA_Q_LORA, MLA_KV_LORA, MLA_ROPE,
              BRANCH_W, BRANCH_W, BRANCH_W,
              BRANCH_W, WIN_KV_HEADS * HEAD_DIM, WIN_KV_HEADS * HEAD_DIM,
              BRANCH_W, BRANCH_W, BRANCH_W)
MIX_COLS = sum(MIX_SPLITS)
MIX_OFFSETS = tuple(int(o) for o in np.cumsum((0,) + MIX_SPLITS))

LANES = 128
VMEM_LIMIT_BYTES = 56 * 1024 * 1024


def _pick(n, prefs):
    for p in prefs:
        if n % p == 0:
            return p
    raise ValueError(f"no tile in {prefs} divides {n}")


def _params(*sem):
    return pltpu.CompilerParams(dimension_semantics=sem, vmem_limit_bytes=VMEM_LIMIT_BYTES)


def _rmsnorm_kernel(x_ref, g_ref, o_ref):
    x = x_ref[...].astype(F32)
    ms = jnp.mean(x * x, axis=-1, keepdims=True)
    o_ref[...] = (x * lax.rsqrt(ms + EPS) * g_ref[...]).astype(o_ref.dtype)


def rmsnorm(x, g):
    M, D = x.shape
    tm = _pick(M, (512, 256, 128, 64, 8))
    return pl.pallas_call(
        _rmsnorm_kernel,
        grid=(M // tm,),
        in_specs=[pl.BlockSpec((tm, D), lambda i: (i, 0)),
                  pl.BlockSpec((1, D), lambda i: (0, 0))],
        out_specs=pl.BlockSpec((tm, D), lambda i: (i, 0)),
        out_shape=jax.ShapeDtypeStruct((M, D), BF16),
        compiler_params=_params("parallel"),
        name="rmsnorm",
    )(x, g.reshape(1, D).astype(F32))


def _dot(a, b):
    return jnp.dot(a, b, preferred_element_type=F32)


def _mm_sigmoid_kernel(a_ref, b_ref, o_ref):
    o_ref[...] = jax.nn.sigmoid(_dot(a_ref[...], b_ref[...])).astype(o_ref.dtype)


def _mm_segnorm_kernel(a_ref, b_ref, g_ref, f_ref, o_ref):
    acc = _dot(a_ref[...], b_ref[...])
    for c in range(acc.shape[1] // LANES):
        seg = slice(c * LANES, (c + 1) * LANES)
        y = acc[:, seg]
        r = lax.rsqrt(jnp.mean(y * y, axis=-1, keepdims=True) + EPS)
        mult = jnp.where(f_ref[:, seg] > 0.0, r, 1.0) * g_ref[:, seg]
        o_ref[:, seg] = (y * mult).astype(o_ref.dtype)


def _mm_residual_kernel(a_ref, b_ref, r_ref, o_ref):
    o_ref[...] = r_ref[...] + _dot(a_ref[...], b_ref[...])


def _mm_swiglu_kernel(a_ref, bg_ref, bu_ref, o_ref):
    a = a_ref[...]
    g = _dot(a, bg_ref[...])
    u = _dot(a, bu_ref[...])
    o_ref[...] = (g * jax.nn.sigmoid(g) * u).astype(o_ref.dtype)


def mm_sigmoid(a, w, tm_prefs=(1024, 512, 256, 128), tn_prefs=(512, 256, 128)):
    M, K = a.shape
    N = w.shape[1]
    tm, tn = _pick(M, tm_prefs), _pick(N, tn_prefs)
    return pl.pallas_call(
        _mm_sigmoid_kernel,
        grid=(M // tm, N // tn),
        in_specs=[pl.BlockSpec((tm, K), lambda i, j: (i, 0)),
                  pl.BlockSpec((K, tn), lambda i, j: (0, j))],
        out_specs=pl.BlockSpec((tm, tn), lambda i, j: (i, j)),
        out_shape=jax.ShapeDtypeStruct((M, N), BF16),
        compiler_params=_params("parallel", "arbitrary"),
        name="mm_sigmoid",
    )(a, w)


def mm_segnorm(a, w, gain, flag, tm_prefs=(1024, 512, 256, 128), tn_prefs=(512, 256, 128)):
    M, K = a.shape
    N = w.shape[1]
    tm, tn = _pick(M, tm_prefs), _pick(N, tn_prefs)
    return pl.pallas_call(
        _mm_segnorm_kernel,
        grid=(M // tm, N // tn),
        in_specs=[pl.BlockSpec((tm, K), lambda i, j: (i, 0)),
                  pl.BlockSpec((K, tn), lambda i, j: (0, j)),
                  pl.BlockSpec((1, tn), lambda i, j: (0, j)),
                  pl.BlockSpec((1, tn), lambda i, j: (0, j))],
        out_specs=pl.BlockSpec((tm, tn), lambda i, j: (i, j)),
        out_shape=jax.ShapeDtypeStruct((M, N), BF16),
        compiler_params=_params("parallel", "arbitrary"),
        name="mm_segnorm",
    )(a, w, gain.reshape(1, N).astype(F32), flag.reshape(1, N).astype(F32))


def mm_residual(a, w, res, tm_prefs=(1024, 512, 256, 128), tn_prefs=(512, 256, 128)):
    M, K = a.shape
    N = w.shape[1]
    tm, tn = _pick(M, tm_prefs), _pick(N, tn_prefs)
    return pl.pallas_call(
        _mm_residual_kernel,
        grid=(M // tm, N // tn),
        in_specs=[pl.BlockSpec((tm, K), lambda i, j: (i, 0)),
                  pl.BlockSpec((K, tn), lambda i, j: (0, j)),
                  pl.BlockSpec((tm, tn), lambda i, j: (i, j))],
        out_specs=pl.BlockSpec((tm, tn), lambda i, j: (i, j)),
        out_shape=jax.ShapeDtypeStruct((M, N), F32),
        compiler_params=_params("parallel", "arbitrary"),
        name="mm_residual",
    )(a, w, res)


def mm_swiglu(a, w, hidden):
    M, K = a.shape
    tm = _pick(M, (1024, 512, 256, 128))
    tn = _pick(hidden, (256, 128))
    nj = hidden // tn
    return pl.pallas_call(
        _mm_swiglu_kernel,
        grid=(M // tm, nj),
        in_specs=[pl.BlockSpec((tm, K), lambda i, j: (i, 0)),
                  pl.BlockSpec((K, tn), lambda i, j: (0, j)),
                  pl.BlockSpec((K, tn), lambda i, j: (0, j + nj))],
        out_specs=pl.BlockSpec((tm, tn), lambda i, j: (i, j)),
        out_shape=jax.ShapeDtypeStruct((M, hidden), BF16),
        compiler_params=_params("parallel", "arbitrary"),
        name="mm_swiglu",
    )(a, w, w)


def _branch_kernel(oa_ref, ob_ref, oc_ref, od_ref, w_ref, ga_ref, gb_ref, gc_ref, gd_ref, o_ref):
    acc = ga_ref[...].astype(F32) * _dot(oa_ref[...], w_ref[0])
    acc += gb_ref[...].astype(F32) * _dot(ob_ref[...], w_ref[1])
    acc += gc_ref[...].astype(F32) * _dot(oc_ref[...], w_ref[2])
    acc += gd_ref[...].astype(F32) * _dot(od_ref[...], w_ref[3])
    o_ref[...] = acc.astype(o_ref.dtype)


def branch_merge(outs, w_branch, gates):
    M, W = outs[0].shape
    D = w_branch.shape[2]
    tm = _pick(M, (1024, 512, 256, 128))
    tn = _pick(D, (512, 256, 128))
    nj = D // tn
    o_spec = pl.BlockSpec((tm, W), lambda i, j: (i, 0))
    g_specs = [pl.BlockSpec((tm, tn), functools.partial(lambda i, j, b: (i, b * nj + j), b=b))
               for b in range(N_BRANCH)]
    return pl.pallas_call(
        _branch_kernel,
        grid=(M // tm, nj),
        in_specs=[o_spec] * N_BRANCH + [pl.BlockSpec((N_BRANCH, W, tn), lambda i, j: (0, 0, j))] + g_specs,
        out_specs=pl.BlockSpec((tm, tn), lambda i, j: (i, j)),
        out_shape=jax.ShapeDtypeStruct((M, D), BF16),
        compiler_params=_params("parallel", "arbitrary"),
        name="branch_merge",
    )(*outs, w_branch, gates, gates, gates, gates)


def _mla_prep_kernel(qa_ref, kva_ref, kpe_ref, ct_ref, st_ref, qag_ref, kvag_ref,
                     wq_ref, wqs_ref, wk_ref, wv_ref, gq_ref, gk_ref,
                     q_ref, k_ref, v_ref):
    def norm(x, g):
        return (x * lax.rsqrt(jnp.mean(x * x, axis=-1, keepdims=True) + EPS) * g).astype(BF16)

    ct = ct_ref[...]
    st = st_ref[...]
    qn = norm(qa_ref[...].astype(F32), qag_ref[...])
    kvn = norm(kva_ref[...].astype(F32), kvag_ref[...])
    qfull = _dot(qn, wq_ref[...])
    qsw = _dot(qn, wqs_ref[...])
    knope = _dot(kvn, wk_ref[...])
    v_ref[...] = _dot(kvn, wv_ref[...]).astype(v_ref.dtype)

    kpe = kpe_ref[...].astype(F32)
    kpe_rot = kpe * ct + pltpu.roll(kpe, 2 * (MLA_ROPE // 2), 1) * st
    kpe_ss = jnp.sum(kpe_rot * kpe_rot, axis=-1, keepdims=True)
    gq_n, gq_r = gq_ref[:, :LANES], gq_ref[:, LANES:]
    gk_n, gk_r = gk_ref[:, :LANES], gk_ref[:, LANES:]
    for h in range(MLA_HEADS):
        lo = h * MLA_QK_PAD
        q_n = qfull[:, lo:lo + LANES]
        q_r = qfull[:, lo + LANES:lo + 2 * LANES] * ct + qsw[:, h * LANES:(h + 1) * LANES] * st
        ss = jnp.sum(q_n * q_n, axis=-1, keepdims=True) + jnp.sum(q_r * q_r, axis=-1, keepdims=True)
        r = lax.rsqrt(ss * (1.0 / MLA_QK) + EPS)
        q_ref[:, lo:lo + LANES] = (q_n * r * gq_n).astype(q_ref.dtype)
        q_ref[:, lo + LANES:lo + 2 * LANES] = (q_r * r * gq_r).astype(q_ref.dtype)
        k_n = knope[:, h * LANES:(h + 1) * LANES]
        ss = jnp.sum(k_n * k_n, axis=-1, keepdims=True) + kpe_ss
        r = lax.rsqrt(ss * (1.0 / MLA_QK) + EPS)
        k_ref[:, lo:lo + LANES] = (k_n * r * gk_n).astype(k_ref.dtype)
        k_ref[:, lo + LANES:lo + 2 * LANES] = (kpe_rot * r * gk_r).astype(k_ref.dtype)


def mla_prep(mix, cols, S, ct, st, lw):
    M = mix.shape[0]
    tm = _pick(S, (256, 128))
    ns = S // tm
    HQ = MLA_HEADS * MLA_QK_PAD
    HV = MLA_HEADS * MLA_V

    def col(width, off):
        assert off % width == 0
        return pl.BlockSpec((tm, width), lambda i: (i, off // width))

    def whole(a):
        return pl.BlockSpec(a.shape, lambda i: (0,) * a.ndim)

    consts = (lw["mla_qa_g"], lw["mla_kva_g"], lw["mla_wq"], lw["mla_wq_sw"], lw["mla_wk"], lw["mla_wv"],
              lw["mla_gq"], lw["mla_gk"])
    return pl.pallas_call(
        _mla_prep_kernel,
        grid=(M // tm,),
        in_specs=[col(MLA_Q_LORA, cols["qa"]), col(MLA_KV_LORA, cols["kva"]), col(LANES, cols["kpe"]),
                  pl.BlockSpec((tm, LANES), lambda i: (i % ns, 0)),
                  pl.BlockSpec((tm, LANES), lambda i: (i % ns, 0))] + [whole(c) for c in consts],
        out_specs=[pl.BlockSpec((tm, HQ), lambda i: (i, 0)),
                   pl.BlockSpec((tm, HQ), lambda i: (i, 0)),
                   pl.BlockSpec((tm, HV), lambda i: (i, 0))],
        out_shape=[jax.ShapeDtypeStruct((M, HQ), BF16),
                   jax.ShapeDtypeStruct((M, HQ), BF16),
                   jax.ShapeDtypeStruct((M, HV), BF16)],
        compiler_params=_params("parallel"),
        name="mla_prep",
    )(mix, mix, mix, ct, st, *consts)


def _qk(q, k):
    return lax.dot_general(q, k, (((1,), (1,)), ((), ())), preferred_element_type=F32)


def _dense_attn_kernel(q_ref, k_ref, v_ref, o_ref):
    s = _qk(q_ref[...], k_ref[...])
    p = jnp.exp(s - jnp.max(s, axis=-1, keepdims=True))
    l = jnp.sum(p, axis=-1, keepdims=True)
    o_ref[...] = (_dot(p.astype(BF16), v_ref[...]) / l).astype(o_ref.dtype)


def dense_attention(q, k, v, B, S, H, dk, dv):
    tq = _pick(S, (512, 256, 128))
    q3, k3, v3 = (t.reshape(B, S, t.shape[1]) for t in (q, k, v))
    out = pl.pallas_call(
        _dense_attn_kernel,
        grid=(B, H, S // tq),
        in_specs=[pl.BlockSpec((None, tq, dk), lambda b, h, i: (b, i, h)),
                  pl.BlockSpec((None, S, dk), lambda b, h, i: (b, 0, h)),
                  pl.BlockSpec((None, S, dv), lambda b, h, i: (b, 0, h))],
        out_specs=pl.BlockSpec((None, tq, dv), lambda b, h, i: (b, i, h)),
        out_shape=jax.ShapeDtypeStruct((B, S, H * dv), BF16),
        compiler_params=_params("parallel", "parallel", "arbitrary"),
        name="dense_attention",
    )(q3, k3, v3)
    return out.reshape(B * S, H * dv)


def _band_kernel(sink_ref, q_ref, k_ref, v_ref, bias_ref, o_ref, *, tq, tk, radius, seq, n_chunks):
    h = pl.program_id(1)
    q0 = pl.program_id(2) * tq
    q = q_ref[...]
    scores, starts = [], []
    for c in range(n_chunks):
        start = q0 - radius + c * tk
        inside = jnp.logical_and(start >= 0, start + tk <= seq)
        sc = pl.multiple_of(jnp.clip(start, 0, seq - tk), tk)
        s = _qk(q, k_ref[pl.ds(sc, tk), :]) + bias_ref[c] + jnp.where(inside, 0.0, NEG_INF)
        scores.append(s)
        starts.append(sc)
    sink = sink_ref[h]
    m = jnp.max(scores[0], axis=-1, keepdims=True)
    for s in scores[1:]:
        m = jnp.maximum(m, jnp.max(s, axis=-1, keepdims=True))
    m = jnp.maximum(m, sink)
    l = jnp.exp(sink - m)
    acc = jnp.zeros((tq, v_ref.shape[1]), F32)
    for s, sc in zip(scores, starts):
        p = jnp.exp(s - m)
        l = l + jnp.sum(p, axis=-1, keepdims=True)
        acc = acc + _dot(p.astype(BF16), v_ref[pl.ds(sc, tk), :])
    o_ref[...] = (acc / l).astype(o_ref.dtype)


def band_bias(slopes, mult_fn, tq, tk, radius):
    n_chunks = (tq + 2 * radius) // tk
    a = jnp.arange(tq)[None, :, None]
    c = jnp.arange(tk)[None, None, :]
    delta = (jnp.arange(n_chunks)[:, None, None] * tk - radius) + c - a
    mult = mult_fn(delta)
    dist = jnp.abs(delta).astype(F32)
    logm = jnp.log(jnp.maximum(mult, 1).astype(F32))
    bias = logm[None] - slopes[:, None, None, None] * dist[None]
    return jnp.where(mult[None] > 0, bias, NEG_INF)


def dil_multiplicity(delta):
    m = jnp.zeros(delta.shape, jnp.int32)
    for window, dil in DIL_PATTERNS:
        m = m + ((delta % dil == 0) & (jnp.abs(delta) <= window // 2)).astype(jnp.int32)
    return m


def win_multiplicity(delta):
    return (jnp.abs(delta) <= WIN_RADIUS).astype(jnp.int32)


def band_attention(qarr, q_off, karr, k_off, varr, v_off, bias, sinks, B, S, H, rep, tq, tk, radius):
    n_chunks = bias.shape[1]
    assert radius % tk == 0 and tq % tk == 0 and S % tq == 0 and S >= tk
    qb, kb, vb = q_off // HEAD_DIM, k_off // HEAD_DIM, v_off // HEAD_DIM
    q3, k3, v3 = (t.reshape(B, S, t.shape[1]) for t in (qarr, karr, varr))
    kern = functools.partial(_band_kernel, tq=tq, tk=tk, radius=radius, seq=S, n_chunks=n_chunks)
    out = pl.pallas_call(
        kern,
        grid=(B, H, S // tq),
        in_specs=[pl.BlockSpec(memory_space=pltpu.SMEM),
                  pl.BlockSpec((None, tq, HEAD_DIM), lambda b, h, i: (b, i, qb + h)),
                  pl.BlockSpec((None, S, HEAD_DIM), lambda b, h, i: (b, 0, kb + h // rep)),
                  pl.BlockSpec((None, S, HEAD_DIM), lambda b, h, i: (b, 0, vb + h // rep)),
                  pl.BlockSpec((None, n_chunks, tq, tk), lambda b, h, i: (h, 0, 0, 0))],
        out_specs=pl.BlockSpec((None, tq, HEAD_DIM), lambda b, h, i: (b, i, h)),
        out_shape=jax.ShapeDtypeStruct((B, S, H * HEAD_DIM), BF16),
        compiler_params=_params("parallel", "parallel", "arbitrary"),
        name="band_attention",
    )(sinks.astype(F32), q3, k3, v3, bias)
    return out.reshape(B * S, H * HEAD_DIM)


def _diff_kernel(slope_ref, lam_ref, q_ref, k_ref, v_ref, g_ref, o_ref, *, tq, lam_init):
    h = pl.program_id(1)
    q0 = pl.program_id(2) * tq
    seq = k_ref.shape[0]
    lp = lam_ref[...]
    lam = (jnp.exp(jnp.sum(lp[0:1] * lp[1:2], axis=-1, keepdims=True))
           - jnp.exp(jnp.sum(lp[2:3] * lp[3:4], axis=-1, keepdims=True)) + lam_init)
    qpos = q0 + lax.broadcasted_iota(jnp.int32, (tq, seq), 0)
    kpos = lax.broadcasted_iota(jnp.int32, (tq, seq), 1)
    bias = slope_ref[h] * jnp.abs(qpos - kpos).astype(F32)

    def softmax_map(m):
        seg = slice(m * DIFF_HD, (m + 1) * DIFF_HD)
        s = _qk(q_ref[:, seg], k_ref[:, seg]) - bias
        p = jnp.exp(s - jnp.max(s, axis=-1, keepdims=True))
        return p, jnp.sum(p, axis=-1, keepdims=True)

    p1, l1 = softmax_map(0)
    p2, l2 = softmax_map(1)
    a = p1 * (1.0 / l1) - p2 * (lam / l2)
    o = _dot(a.astype(BF16), v_ref[...])
    r = lax.rsqrt(jnp.mean(o * o, axis=-1, keepdims=True) + EPS)
    o_ref[...] = (o * r * g_ref[...] * (1.0 - lam_init)).astype(o_ref.dtype)


def diff_attention(mix, cols, slopes, lam_p, subln_g, B, S, layer):
    tq = _pick(S, (256, 128))
    lam_init = 0.8 - 0.6 * math.exp(-0.3 * layer)
    W = 2 * DIFF_HD
    qb, kb, vb = (cols[n] // W for n in ("fq", "fk", "fv"))
    m3 = mix.reshape(B, S, mix.shape[1])
    kern = functools.partial(_diff_kernel, tq=tq, lam_init=lam_init)
    out = pl.pallas_call(
        kern,
        grid=(B, DIFF_HEADS, S // tq),
        in_specs=[pl.BlockSpec(memory_space=pltpu.SMEM),
                  pl.BlockSpec((4, DIFF_HD), lambda b, h, i: (0, 0)),
                  pl.BlockSpec((None, tq, W), lambda b, h, i: (b, i, qb + h)),
                  pl.BlockSpec((None, S, W), lambda b, h, i: (b, 0, kb + h)),
                  pl.BlockSpec((None, S, W), lambda b, h, i: (b, 0, vb + h)),
                  pl.BlockSpec((1, W), lambda b, h, i: (0, 0))],
        out_specs=pl.BlockSpec((None, tq, W), lambda b, h, i: (b, i, h)),
        out_shape=jax.ShapeDtypeStruct((B, S, DIFF_HEADS * W), BF16),
        compiler_params=_params("parallel", "parallel", "arbitrary"),
        name="diff_attention",
    )(slopes.astype(F32), lam_p.astype(F32), m3, m3, m3, subln_g.reshape(1, W).astype(F32))
    return out.reshape(B * S, DIFF_HEADS * W)


def _mem_attn_kernel(q_ref, kv_ref, wo_ref, x_ref, o_ref):
    heads = []
    for h in range(MEM_HEADS):
        seg = slice(h * HEAD_DIM, (h + 1) * HEAD_DIM)
        s = _qk(q_ref[:, seg], kv_ref[:, seg])
        p = jnp.exp(s - jnp.max(s, axis=-1, keepdims=True))
        l = jnp.sum(p, axis=-1, keepdims=True)
        vseg = slice(MEM_W + h * HEAD_DIM, MEM_W + (h + 1) * HEAD_DIM)
        heads.append((_dot(p.astype(BF16), kv_ref[:, vseg]) / l).astype(BF16))
    o = jnp.concatenate(heads, axis=-1)
    o_ref[...] = x_ref[...] + _dot(o, wo_ref[...])


def mem_attention(q, kv, wo, x, B, S):
    D = x.shape[1]
    Mt = kv.shape[0] // B
    tq = _pick(S, (512, 256, 128))
    q3 = q.reshape(B, S, MEM_W)
    kv3 = kv.reshape(B, Mt, 2 * MEM_W)
    x3 = x.reshape(B, S, D)
    out = pl.pallas_call(
        _mem_attn_kernel,
        grid=(B, S // tq),
        in_specs=[pl.BlockSpec((None, tq, MEM_W), lambda b, i: (b, i, 0)),
                  pl.BlockSpec((None, Mt, 2 * MEM_W), lambda b, i: (b, 0, 0)),
                  pl.BlockSpec((MEM_W, D), lambda b, i: (0, 0)),
                  pl.BlockSpec((None, tq, D), lambda b, i: (b, i, 0))],
        out_specs=pl.BlockSpec((None, tq, D), lambda b, i: (b, i, 0)),
        out_shape=jax.ShapeDtypeStruct((B, S, D), F32),
        compiler_params=_params("parallel", "arbitrary"),
        name="mem_attention",
    )(q3, kv3, wo, x3)
    return out.reshape(B * S, D)


def alibi_slopes(n):
    return 2.0 ** (-8.0 * jnp.arange(1, n + 1, dtype=F32) / n)


def _mix_layout():
    names = ("qa", "kva", "kpe", "dq", "dk", "dv", "wq", "wk", "wv", "fq", "fk", "fv")
    src = {n: (MIX_OFFSETS[i], MIX_OFFSETS[i + 1]) for i, n in enumerate(names)}
    order = ("dq", "dk", "dv", "fq", "fk", "fv", "wq", "qa", "wk", "wv", "kva", "kpe")
    cols, off = {}, 0
    for n in order:
        width = LANES if n == "kpe" else src[n][1] - src[n][0]
        assert off % min(width, BRANCH_W) == 0, (n, off, width)
        cols[n] = off
        off += width
    total = -(-off // 512) * 512
    return src, order, cols, off, total


def pack_layer(p, l):
    src, order, cols, used, total = _mix_layout()
    w_in = p["w_in"][l]
    half = MLA_ROPE // 2
    pieces, gains, flags = [], [], []
    qscale = HEAD_DIM ** -0.5

    def tile(g, n):
        return jnp.tile(g.astype(F32), n)

    norm_gain = {
        "dq": tile(p["dil_qk_g"][l, 0], DIL_HEADS) * qscale, "dk": tile(p["dil_qk_g"][l, 1], DIL_HEADS),
        "fq": tile(p["diff_qk_g"][l, 0], 2 * DIFF_HEADS) * (DIFF_HD ** -0.5),
        "fk": tile(p["diff_qk_g"][l, 1], 2 * DIFF_HEADS),
        "wq": tile(p["win_qk_g"][l, 0], WIN_Q_HEADS) * qscale, "wk": tile(p["win_qk_g"][l, 1], WIN_KV_HEADS),
    }
    for n in order:
        a, b = src[n]
        w = w_in[:, a:b]
        if n == "kpe":
            x1, x2 = w[:, :half], w[:, half:]
            w = jnp.concatenate([x1, x2, x2, x1], axis=1)
        pieces.append(w)
        width = w.shape[1]
        if n in norm_gain:
            gains.append(norm_gain[n])
            flags.append(jnp.ones((width,), F32))
        else:
            gains.append(jnp.ones((width,), F32))
            flags.append(jnp.zeros((width,), F32))
    pad = total - used
    pieces.append(jnp.zeros((D_MODEL, pad), F32))
    gains.append(jnp.ones((pad,), F32))
    flags.append(jnp.zeros((pad,), F32))

    wq3 = p["mla_wq_up"][l].reshape(MLA_Q_LORA, MLA_HEADS, MLA_QK)
    nope, x1, x2 = wq3[:, :, :MLA_NOPE], wq3[:, :, MLA_NOPE:MLA_NOPE + half], wq3[:, :, MLA_NOPE + half:]
    z = jnp.zeros((MLA_Q_LORA, MLA_HEADS, LANES - MLA_ROPE), F32)
    wq_full = jnp.concatenate([nope, x1, x2, z], axis=-1).reshape(MLA_Q_LORA, MLA_HEADS * MLA_QK_PAD)
    wq_sw = jnp.concatenate([x2, x1, z], axis=-1).reshape(MLA_Q_LORA, MLA_HEADS * LANES)
    wkv3 = p["mla_wkv_up"][l].reshape(MLA_KV_LORA, MLA_HEADS, MLA_NOPE + MLA_V)
    zg = jnp.zeros((LANES - MLA_ROPE,), F32)
    qk_g = p["mla_qk_g"][l].astype(F32)

    return {
        "ln_mix_g": p["ln_mix_g"][l],
        "w_mix": jnp.concatenate(pieces, axis=1).astype(BF16),
        "mix_gain": jnp.concatenate(gains), "mix_flag": jnp.concatenate(flags),
        "w_gate": w_in[:, MIX_COLS:].astype(BF16),
        "mla_qa_g": p["mla_qa_g"][l].reshape(1, -1).astype(F32),
        "mla_kva_g": p["mla_kva_g"][l].reshape(1, -1).astype(F32),
        "mla_wq": wq_full.astype(BF16), "mla_wq_sw": wq_sw.astype(BF16),
        "mla_wk": wkv3[:, :, :MLA_NOPE].reshape(MLA_KV_LORA, -1).astype(BF16),
        "mla_wv": wkv3[:, :, MLA_NOPE:].reshape(MLA_KV_LORA, -1).astype(BF16),
        "mla_gq": (jnp.concatenate([qk_g[0], zg]) * (MLA_QK ** -0.5)).reshape(1, -1),
        "mla_gk": jnp.concatenate([qk_g[1], zg]).reshape(1, -1),
        "win_sink": p["win_sink"][l], "diff_lambda": p["diff_lambda"][l], "diff_subln_g": p["diff_subln_g"][l],
        "w_branch": p["w_branch"][l].astype(BF16), "w_out": p["w_out"][l].astype(BF16),
        "ln_mem_g": p["ln_mem_g"][l], "mem_ln_g": p["mem_ln_g"][l],
        "mem_wq": p["mem_wq"][l].astype(BF16), "mem_wkv": p["mem_wkv"][l].astype(BF16),
        "mem_q_gain": tile(p["mem_qk_g"][l, 0], MEM_HEADS) * (HEAD_DIM ** -0.5),
        "mem_kv_gain": jnp.concatenate([tile(p["mem_qk_g"][l, 1], MEM_HEADS), jnp.ones((MEM_W,), F32)]),
        "mem_kv_flag": jnp.concatenate([jnp.ones((MEM_W,), F32), jnp.zeros((MEM_W,), F32)]),
        "mem_wo": p["mem_wo"][l].astype(BF16),
        "ln_ffn_g": p["ln_ffn_g"][l],
        "ffn_w_in": p["ffn_w_in"][l].astype(BF16), "ffn_w_out": p["ffn_w_out"][l].astype(BF16),
    }


def rotary_tables(S):
    half = MLA_ROPE // 2
    inv_freq = ROPE_THETA ** (-jnp.arange(half, dtype=F32) / half)
    ang = jnp.arange(S, dtype=F32)[:, None] * inv_freq[None, :]
    cos, sin = jnp.cos(ang), jnp.sin(ang)
    z = jnp.zeros((S, LANES - MLA_ROPE), F32)
    return jnp.concatenate([cos, cos, z], axis=1), jnp.concatenate([-sin, sin, z], axis=1)


def _trunk(x, mem, layers, tables):
    B, S, D = x.shape
    M = B * S
    _, _, cols, _, _ = _mix_layout()
    ct, st = rotary_tables(S)
    xf = x.reshape(M, D)
    memf = mem.reshape(-1, D)
    dil_tq = _pick(S, (256,))
    for l, lw in enumerate(layers):
        h = rmsnorm(xf, lw["ln_mix_g"])
        mix = mm_segnorm(h, lw["w_mix"], lw["mix_gain"], lw["mix_flag"])
        gates = mm_sigmoid(h, lw["w_gate"])

        q, k, v = mla_prep(mix, cols, S, ct, st, lw)
        o_mla = dense_attention(q, k, v, B, S, MLA_HEADS, MLA_QK_PAD, MLA_V)
        o_dil = band_attention(mix, cols["dq"], mix, cols["dk"], mix, cols["dv"], tables["dil_bias"],
                               jnp.full((DIL_HEADS,), NEG_INF, F32), B, S, DIL_HEADS, 1,
                               dil_tq, dil_tq, DIL_RADIUS)
        o_win = band_attention(mix, cols["wq"], mix, cols["wk"], mix, cols["wv"], tables["win_bias"],
                               lw["win_sink"], B, S, WIN_Q_HEADS, WIN_Q_HEADS // WIN_KV_HEADS,
                               2 * WIN_RADIUS, WIN_RADIUS, WIN_RADIUS)
        o_diff = diff_attention(mix, cols, tables["slopes_diff"], lw["diff_lambda"], lw["diff_subln_g"], B, S, l)

        merged = branch_merge((o_mla, o_dil, o_win, o_diff), lw["w_branch"], gates)
        xf = mm_residual(merged, lw["w_out"], xf)

        hm = rmsnorm(xf, lw["ln_mem_g"])
        qm = mm_segnorm(hm, lw["mem_wq"], lw["mem_q_gain"], jnp.ones((MEM_W,), F32))
        kvm = mm_segnorm(rmsnorm(memf, lw["mem_ln_g"]), lw["mem_wkv"], lw["mem_kv_gain"], lw["mem_kv_flag"])
        xf = mem_attention(qm, kvm, lw["mem_wo"], xf, B, S)

        hf = rmsnorm(xf, lw["ln_ffn_g"])
        hid = mm_swiglu(hf, lw["ffn_w_in"], FFN_HIDDEN)
        xf = mm_residual(hid, lw["ffn_w_out"], xf, tm_prefs=(512, 256, 128), tn_prefs=(256, 128))
    return xf.reshape(B, S, D)


def kernel(x_prompt, x_sample, mem_prompt, mem_sample, ln_mix_g, w_in, mla_qa_g, mla_kva_g, mla_wq_up,
           mla_wkv_up, mla_qk_g, dil_qk_g, win_qk_g, win_sink, diff_qk_g, diff_lambda, diff_subln_g,
           w_branch, w_out, ln_mem_g, mem_ln_g, mem_wq, mem_wkv, mem_qk_g, mem_wo, ln_ffn_g, ffn_w_in,
           ffn_w_out):
    p = dict(ln_mix_g=ln_mix_g, w_in=w_in, mla_qa_g=mla_qa_g, mla_kva_g=mla_kva_g, mla_wq_up=mla_wq_up,
             mla_wkv_up=mla_wkv_up, mla_qk_g=mla_qk_g, dil_qk_g=dil_qk_g, win_qk_g=win_qk_g,
             win_sink=win_sink, diff_qk_g=diff_qk_g, diff_lambda=diff_lambda, diff_subln_g=diff_subln_g,
             w_branch=w_branch, w_out=w_out, ln_mem_g=ln_mem_g, mem_ln_g=mem_ln_g, mem_wq=mem_wq,
             mem_wkv=mem_wkv, mem_qk_g=mem_qk_g, mem_wo=mem_wo, ln_ffn_g=ln_ffn_g, ffn_w_in=ffn_w_in,
             ffn_w_out=ffn_w_out)
    layers = [pack_layer(p, l) for l in range(DEPTH)]
    tables = {
        "dil_bias": band_bias(alibi_slopes(DIL_HEADS), dil_multiplicity, 256, 256, DIL_RADIUS),
        "win_bias": band_bias(alibi_slopes(WIN_Q_HEADS), win_multiplicity, 2 * WIN_RADIUS, WIN_RADIUS,
                              WIN_RADIUS),
        "slopes_diff": alibi_slopes(DIFF_HEADS),
    }
    y_prompt = _trunk(x_prompt, mem_prompt, layers, tables)
    y_sample = _trunk(x_sample, mem_sample, layers, tables)
    return (y_prompt, y_sample)
```

```python
import functools
import math

import jax
import jax.numpy as jnp
import numpy as np
from jax import lax
from jax.experimental import pallas as pl
from jax.experimental.pallas import tpu as pltpu

F32 = jnp.float32
BF16 = jnp.bfloat16

D_MODEL = 4096
DEPTH = 2
EPS = 1e-6
NEG_INF = -1e30
N_BRANCH = 4
BRANCH_W = D_MODEL // N_BRANCH
HEAD_DIM = 128

MLA_NOPE = 128
MLA_ROPE = 64
MLA_V = 128
MLA_HEADS = BRANCH_W // MLA_V
MLA_Q_LORA = D_MODEL // 4
MLA_KV_LORA = D_MODEL // 8
MLA_QK = MLA_NOPE + MLA_ROPE
MLA_QK_PAD = 256
ROPE_THETA = 10000.0

DIL_HEADS = BRANCH_W // HEAD_DIM
DIL_PATTERNS = ((128, 1), (512, 4), (2048, 16))
DIL_RADIUS = max(w // 2 for w, _ in DIL_PATTERNS)

WIN_Q_HEADS = BRANCH_W // HEAD_DIM
WIN_KV_HEADS = WIN_Q_HEADS // 4
WIN_RADIUS = 128

DIFF_HD = 128
DIFF_HEADS = BRANCH_W // (2 * DIFF_HD)

MEM_HEADS = 4
MEM_W = MEM_HEADS * HEAD_DIM

FFN_HIDDEN = -(-8 * D_MODEL // (3 * 256)) * 256

MIX_SPLITS = (MLA_Q_LORA, MLA_KV_LORA, MLA_ROPE,
              BRANCH_W, BRANCH_W, BRANCH_W,
              BRANCH_W, WIN_KV_HEADS * HEAD_DIM, WIN_KV_HEADS * HEAD_DIM,
              BRANCH_W, BRANCH_W, BRANCH_W)
MIX_COLS = sum(MIX_SPLITS)
MIX_OFFSETS = tuple(int(o) for o in np.cumsum((0,) + MIX_SPLITS))

LOG2E = math.log2(math.e)
LANES = 128
MIX_TILE = 512
VMEM_LIMIT_BYTES = 56 * 1024 * 1024


def _pick(n, prefs):
    for p in prefs:
        if n % p == 0:
            return p
    raise ValueError(f"no tile in {prefs} divides {n}")


def _params(*sem):
    return pltpu.CompilerParams(dimension_semantics=sem, vmem_limit_bytes=VMEM_LIMIT_BYTES)


def _rmsnorm_kernel(x_ref, g_ref, o_ref):
    x = x_ref[...].astype(F32)
    ms = jnp.mean(x * x, axis=-1, keepdims=True)
    o_ref[...] = (x * lax.rsqrt(ms + EPS) * g_ref[...]).astype(o_ref.dtype)


def rmsnorm(x, g):
    M, D = x.shape
    tm = _pick(M, (512, 256, 128, 64, 8))
    return pl.pallas_call(
        _rmsnorm_kernel,
        grid=(M // tm,),
        in_specs=[pl.BlockSpec((tm, D), lambda i: (i, 0)),
                  pl.BlockSpec((1, D), lambda i: (0, 0))],
        out_specs=pl.BlockSpec((tm, D), lambda i: (i, 0)),
        out_shape=jax.ShapeDtypeStruct((M, D), BF16),
        compiler_params=_params("parallel"),
        name="rmsnorm",
    )(x, g.reshape(1, D).astype(F32))


def _dot(a, b):
    return jnp.dot(a, b, preferred_element_type=F32)


def _mm_sigmoid_kernel(a_ref, b_ref, o_ref):
    o_ref[...] = jax.nn.sigmoid(_dot(a_ref[...], b_ref[...])).astype(o_ref.dtype)


def _mm_segnorm_kernel(a_ref, b_ref, g_ref, f_ref, o_ref):
    acc = _dot(a_ref[...], b_ref[...])
    for c in range(acc.shape[1] // LANES):
        seg = slice(c * LANES, (c + 1) * LANES)
        y = acc[:, seg]
        r = lax.rsqrt(jnp.mean(y * y, axis=-1, keepdims=True) + EPS)
        mult = jnp.where(f_ref[:, seg] > 0.0, r, 1.0) * g_ref[:, seg]
        o_ref[:, seg] = (y * mult).astype(o_ref.dtype)


def _mm_residual_kernel(a_ref, b_ref, r_ref, o_ref):
    o_ref[...] = r_ref[...] + _dot(a_ref[...], b_ref[...])


def _mm_swiglu_kernel(a_ref, bg_ref, bu_ref, o_ref):
    a = a_ref[...]
    g = _dot(a, bg_ref[...])
    u = _dot(a, bu_ref[...])
    o_ref[...] = (g * jax.nn.sigmoid(g) * u).astype(o_ref.dtype)


def mm_sigmoid(a, w, n_cols, col_off, tm_prefs=(1024, 512, 256, 128), tn_prefs=(512, 256, 128)):
    M, K = a.shape
    tm, tn = _pick(M, tm_prefs), _pick(math.gcd(n_cols, col_off) if col_off else n_cols, tn_prefs)
    jb = col_off // tn
    return pl.pallas_call(
        _mm_sigmoid_kernel,
        grid=(M // tm, n_cols // tn),
        in_specs=[pl.BlockSpec((tm, K), lambda i, j: (i, 0)),
                  pl.BlockSpec((K, tn), lambda i, j: (0, j + jb))],
        out_specs=pl.BlockSpec((tm, tn), lambda i, j: (i, j)),
        out_shape=jax.ShapeDtypeStruct((M, n_cols), BF16),
        compiler_params=_params("parallel", "arbitrary"),
        name="mm_sigmoid",
    )(a, w)


def mm_segnorm(a, w, gain, flag, tm_prefs=(1024, 512, 256, 128), tn_prefs=(512, 256, 128)):
    M, K = a.shape
    N = gain.shape[0]
    tm, tn = _pick(M, tm_prefs), _pick(N, tn_prefs)
    return pl.pallas_call(
        _mm_segnorm_kernel,
        grid=(M // tm, N // tn),
        in_specs=[pl.BlockSpec((tm, K), lambda i, j: (i, 0)),
                  pl.BlockSpec((K, tn), lambda i, j: (0, j)),
                  pl.BlockSpec((1, tn), lambda i, j: (0, j)),
                  pl.BlockSpec((1, tn), lambda i, j: (0, j))],
        out_specs=pl.BlockSpec((tm, tn), lambda i, j: (i, j)),
        out_shape=jax.ShapeDtypeStruct((M, N), BF16),
        compiler_params=_params("parallel", "arbitrary"),
        name="mm_segnorm",
    )(a, w, gain.reshape(1, N).astype(F32), flag.reshape(1, N).astype(F32))


def mm_residual(a, w, res, tm_prefs=(1024, 512, 256, 128), tn_prefs=(512, 256, 128)):
    M, K = a.shape
    N = w.shape[1]
    tm, tn = _pick(M, tm_prefs), _pick(N, tn_prefs)
    return pl.pallas_call(
        _mm_residual_kernel,
        grid=(M // tm, N // tn),
        in_specs=[pl.BlockSpec((tm, K), lambda i, j: (i, 0)),
                  pl.BlockSpec((K, tn), lambda i, j: (0, j)),
                  pl.BlockSpec((tm, tn), lambda i, j: (i, j))],
        out_specs=pl.BlockSpec((tm, tn), lambda i, j: (i, j)),
        out_shape=jax.ShapeDtypeStruct((M, N), F32),
        compiler_params=_params("parallel", "arbitrary"),
        name="mm_residual",
    )(a, w, res)


def mm_swiglu(a, w, hidden):
    M, K = a.shape
    tm = _pick(M, (1024, 512, 256, 128))
    tn = _pick(hidden, (256, 128))
    nj = hidden // tn
    return pl.pallas_call(
        _mm_swiglu_kernel,
        grid=(M // tm, nj),
        in_specs=[pl.BlockSpec((tm, K), lambda i, j: (i, 0)),
                  pl.BlockSpec((K, tn), lambda i, j: (0, j)),
                  pl.BlockSpec((K, tn), lambda i, j: (0, j + nj))],
        out_specs=pl.BlockSpec((tm, tn), lambda i, j: (i, j)),
        out_shape=jax.ShapeDtypeStruct((M, hidden), BF16),
        compiler_params=_params("parallel", "arbitrary"),
        name="mm_swiglu",
    )(a, w, w)


def _branch_kernel(oa_ref, ob_ref, oc_ref, od_ref, w_ref, ga_ref, gb_ref, gc_ref, gd_ref, o_ref):
    acc = ga_ref[...].astype(F32) * _dot(oa_ref[...], w_ref[0])
    acc += gb_ref[...].astype(F32) * _dot(ob_ref[...], w_ref[1])
    acc += gc_ref[...].astype(F32) * _dot(oc_ref[...], w_ref[2])
    acc += gd_ref[...].astype(F32) * _dot(od_ref[...], w_ref[3])
    o_ref[...] = acc.astype(o_ref.dtype)


def branch_merge(outs, w_branch, gates):
    M, W = outs[0].shape
    D = w_branch.shape[2]
    tm = _pick(M, (1024, 512, 256, 128))
    tn = _pick(D, (512, 256, 128))
    nj = D // tn
    o_spec = pl.BlockSpec((tm, W), lambda i, j: (i, 0))
    g_specs = [pl.BlockSpec((tm, tn), functools.partial(lambda i, j, b: (i, b * nj + j), b=b))
               for b in range(N_BRANCH)]
    return pl.pallas_call(
        _branch_kernel,
        grid=(M // tm, nj),
        in_specs=[o_spec] * N_BRANCH + [pl.BlockSpec((N_BRANCH, W, tn), lambda i, j: (0, 0, j))] + g_specs,
        out_specs=pl.BlockSpec((tm, tn), lambda i, j: (i, j)),
        out_shape=jax.ShapeDtypeStruct((M, D), BF16),
        compiler_params=_params("parallel", "arbitrary"),
        name="branch_merge",
    )(*outs, w_branch, gates, gates, gates, gates)


def _mla_prep_kernel(qa_ref, kva_ref, kpe_ref, ct_ref, st_ref, qag_ref, kvag_ref,
                     wq_ref, wqs_ref, wk_ref, wv_ref, gq_ref, gk_ref,
                     q_ref, k_ref, v_ref):
    def norm(x, g):
        return (x * lax.rsqrt(jnp.mean(x * x, axis=-1, keepdims=True) + EPS) * g).astype(BF16)

    ct = ct_ref[...]
    st = st_ref[...]
    qn = norm(qa_ref[...].astype(F32), qag_ref[...])
    kvn = norm(kva_ref[...].astype(F32), kvag_ref[...])
    qfull = _dot(qn, wq_ref[...])
    qsw = _dot(qn, wqs_ref[...])
    knope = _dot(kvn, wk_ref[...])
    vals = _dot(kvn, wv_ref[...])
    ones = jnp.ones((vals.shape[0], MLA_V), v_ref.dtype)
    for h in range(MLA_HEADS):
        v_ref[:, 2 * h * MLA_V:(2 * h + 1) * MLA_V] = vals[:, h * MLA_V:(h + 1) * MLA_V].astype(v_ref.dtype)
        v_ref[:, (2 * h + 1) * MLA_V:(2 * h + 2) * MLA_V] = ones

    kpe = kpe_ref[...].astype(F32)
    kpe_rot = kpe * ct + pltpu.roll(kpe, 2 * (MLA_ROPE // 2), 1) * st
    kpe_ss = jnp.sum(kpe_rot * kpe_rot, axis=-1, keepdims=True)
    gq_n, gq_r = gq_ref[:, :LANES], gq_ref[:, LANES:]
    gk_n, gk_r = gk_ref[:, :LANES], gk_ref[:, LANES:]
    for h in range(MLA_HEADS):
        lo = h * MLA_QK_PAD
        q_n = qfull[:, lo:lo + LANES]
        q_r = qfull[:, lo + LANES:lo + 2 * LANES] * ct + qsw[:, h * LANES:(h + 1) * LANES] * st
        ss = jnp.sum(q_n * q_n, axis=-1, keepdims=True) + jnp.sum(q_r * q_r, axis=-1, keepdims=True)
        r = lax.rsqrt(ss * (1.0 / MLA_QK) + EPS)
        q_ref[:, lo:lo + LANES] = (q_n * r * gq_n).astype(q_ref.dtype)
        q_ref[:, lo + LANES:lo + 2 * LANES] = (q_r * r * gq_r).astype(q_ref.dtype)
        k_n = knope[:, h * LANES:(h + 1) * LANES]
        ss = jnp.sum(k_n * k_n, axis=-1, keepdims=True) + kpe_ss
        r = lax.rsqrt(ss * (1.0 / MLA_QK) + EPS)
        k_ref[:, lo:lo + LANES] = (k_n * r * gk_n).astype(k_ref.dtype)
        k_ref[:, lo + LANES:lo + 2 * LANES] = (kpe_rot * r * gk_r).astype(k_ref.dtype)


def mla_prep(mix, cols, S, ct, st, lw):
    M = mix.shape[0]
    tm = _pick(S, (256, 128))
    ns = S // tm
    HQ = MLA_HEADS * MLA_QK_PAD
    HV = MLA_HEADS * 2 * MLA_V

    def col(width, off):
        assert off % width == 0
        return pl.BlockSpec((tm, width), lambda i: (i, off // width))

    def whole(a):
        return pl.BlockSpec(a.shape, lambda i: (0,) * a.ndim)

    consts = (lw["mla_qa_g"], lw["mla_kva_g"], lw["mla_wq"], lw["mla_wq_sw"], lw["mla_wk"], lw["mla_wv"],
              lw["mla_gq"], lw["mla_gk"])
    return pl.pallas_call(
        _mla_prep_kernel,
        grid=(M // tm,),
        in_specs=[col(MLA_Q_LORA, cols["qa"]), col(MLA_KV_LORA, cols["kva"]), col(LANES, cols["kpe"]),
                  pl.BlockSpec((tm, LANES), lambda i: (i % ns, 0)),
                  pl.BlockSpec((tm, LANES), lambda i: (i % ns, 0))] + [whole(c) for c in consts],
        out_specs=[pl.BlockSpec((tm, HQ), lambda i: (i, 0)),
                   pl.BlockSpec((tm, HQ), lambda i: (i, 0)),
                   pl.BlockSpec((tm, HV), lambda i: (i, 0))],
        out_shape=[jax.ShapeDtypeStruct((M, HQ), BF16),
                   jax.ShapeDtypeStruct((M, HQ), BF16),
                   jax.ShapeDtypeStruct((M, HV), BF16)],
        compiler_params=_params("parallel"),
        name="mla_prep",
    )(mix, mix, mix, ct, st, *consts)


def _qk(q, k):
    return lax.dot_general(q, k, (((1,), (1,)), ((), ())), preferred_element_type=F32)


def _dense_attn_kernel(q_ref, k_ref, v_ref, o_ref, *, n_sub, chunk):
    dv = o_ref.shape[1]
    ts = q_ref.shape[0] // n_sub
    seq = k_ref.shape[0]
    for t in range(n_sub):
        rows = slice(t * ts, (t + 1) * ts)
        s = _qk(q_ref[rows, :], k_ref[...])
        pieces = [s[:, c:c + chunk] for c in range(0, seq, chunk)]
        mt = _lane_fold(pieces[0], jnp.maximum)
        for piece in pieces[1:]:
            mt = jnp.maximum(mt, _lane_fold(piece, jnp.maximum))
        m = jnp.max(mt, axis=-1, keepdims=True)
        o = jnp.zeros((ts, 2 * dv), F32)
        for c, piece in zip(range(0, seq, chunk), pieces):
            o = o + _dot(jnp.exp2(piece - m).astype(BF16), v_ref[c:c + chunk, :])
        o_ref[rows, :] = (o[:, :dv] / o[:, dv:]).astype(o_ref.dtype)


def dense_attention(q, k, v, B, S, H, dk, dv, n_sub=4):
    tq = _pick(S, (1024, 512, 256))
    chunk = _pick(S, (512, 256))
    q3, k3, v3 = (t.reshape(B, S, t.shape[1]) for t in (q, k, v))
    out = pl.pallas_call(
        functools.partial(_dense_attn_kernel, n_sub=n_sub, chunk=chunk),
        grid=(B, H, S // tq),
        in_specs=[pl.BlockSpec((None, tq, dk), lambda b, h, i: (b, i, h)),
                  pl.BlockSpec((None, S, dk), lambda b, h, i: (b, 0, h)),
                  pl.BlockSpec((None, S, 2 * dv), lambda b, h, i: (b, 0, h))],
        out_specs=pl.BlockSpec((None, tq, dv), lambda b, h, i: (b, i, h)),
        out_shape=jax.ShapeDtypeStruct((B, S, H * dv), BF16),
        compiler_params=_params("parallel", "parallel", "arbitrary"),
        name="dense_attention",
    )(q3, k3, v3)
    return out.reshape(B * S, H * dv)


def _lane_fold(x, op):
    out = x[:, :LANES]
    for j in range(1, x.shape[1] // LANES):
        out = op(out, x[:, j * LANES:(j + 1) * LANES])
    return out


def _band_kernel(sink_ref, q_ref, k_ref, v_ref, bias_ref, o_ref, *, tq, tk, radius, seq, n_chunks, hpb, rep):
    hb = pl.program_id(1)
    q0 = pl.program_id(2) * tq
    starts, tiles = [], []
    for c in range(n_chunks):
        start = q0 - radius + c * tk
        inside = jnp.logical_and(start >= 0, start + tk <= seq)
        starts.append(pl.multiple_of(jnp.clip(start, 0, seq - tk), tk))
        tiles.append(jnp.where(inside, c, n_chunks))
    for hh in range(hpb):
        qs = slice(hh * HEAD_DIM, (hh + 1) * HEAD_DIM)
        ks = slice((hh // rep) * HEAD_DIM, (hh // rep + 1) * HEAD_DIM)
        q = q_ref[:, qs]
        scores = [_qk(q, k_ref[pl.ds(sc, tk), ks]) + bias_ref[hh, t] for sc, t in zip(starts, tiles)]
        mt = _lane_fold(scores[0], jnp.maximum)
        for s in scores[1:]:
            mt = jnp.maximum(mt, _lane_fold(s, jnp.maximum))
        sink = sink_ref[hb * hpb + hh] * LOG2E
        m = jnp.maximum(jnp.max(mt, axis=-1, keepdims=True), sink)
        lt = jnp.zeros((tq, LANES), F32)
        acc = jnp.zeros((tq, HEAD_DIM), F32)
        for s, sc in zip(scores, starts):
            p = jnp.exp2(s - m)
            lt = lt + _lane_fold(p, jnp.add)
            acc = acc + _dot(p.astype(BF16), v_ref[pl.ds(sc, tk), ks])
        l = jnp.sum(lt, axis=-1, keepdims=True) + jnp.exp2(sink - m)
        o_ref[:, qs] = (acc / l).astype(o_ref.dtype)


def band_bias(slopes, mult_fn, tq, tk, radius):
    n_chunks = (tq + 2 * radius) // tk
    a = jnp.arange(tq)[None, :, None]
    c = jnp.arange(tk)[None, None, :]
    delta = (jnp.arange(n_chunks)[:, None, None] * tk - radius) + c - a
    mult = mult_fn(delta)
    dist = jnp.abs(delta).astype(F32)
    logm = jnp.log2(jnp.maximum(mult, 1).astype(F32))
    bias = logm[None] - (slopes * LOG2E)[:, None, None, None] * dist[None]
    bias = jnp.where(mult[None] > 0, bias, NEG_INF)
    dead = jnp.full((slopes.shape[0], 1, tq, tk), NEG_INF, F32)
    return jnp.concatenate([bias, dead], axis=1)


def dil_multiplicity(delta):
    m = jnp.zeros(delta.shape, jnp.int32)
    for window, dil in DIL_PATTERNS:
        m = m + ((delta % dil == 0) & (jnp.abs(delta) <= window // 2)).astype(jnp.int32)
    return m


def win_multiplicity(delta):
    return (jnp.abs(delta) <= WIN_RADIUS).astype(jnp.int32)


def band_attention(qarr, q_off, karr, k_off, varr, v_off, bias, sinks, B, S, H, rep, hpb, tq, tk, radius):
    n_chunks = bias.shape[1] - 1
    assert radius % tk == 0 and tq % tk == 0 and S % tq == 0 and S >= tk and hpb % rep == 0 and H % hpb == 0
    wq, wk = hpb * HEAD_DIM, (hpb // rep) * HEAD_DIM
    assert q_off % wq == 0 and k_off % wk == 0 and v_off % wk == 0
    qb, kb, vb = q_off // wq, k_off // wk, v_off // wk
    q3, k3, v3 = (t.reshape(B, S, t.shape[1]) for t in (qarr, karr, varr))
    kern = functools.partial(_band_kernel, tq=tq, tk=tk, radius=radius, seq=S, n_chunks=n_chunks, hpb=hpb, rep=rep)
    out = pl.pallas_call(
        kern,
        grid=(B, H // hpb, S // tq),
        in_specs=[pl.BlockSpec(memory_space=pltpu.SMEM),
                  pl.BlockSpec((None, tq, wq), lambda b, h, i: (b, i, qb + h)),
                  pl.BlockSpec((None, S, wk), lambda b, h, i: (b, 0, kb + h)),
                  pl.BlockSpec((None, S, wk), lambda b, h, i: (b, 0, vb + h)),
                  pl.BlockSpec((hpb, n_chunks + 1, tq, tk), lambda b, h, i: (h, 0, 0, 0))],
        out_specs=pl.BlockSpec((None, tq, wq), lambda b, h, i: (b, i, h)),
        out_shape=jax.ShapeDtypeStruct((B, S, H * HEAD_DIM), BF16),
        compiler_params=_params("parallel", "parallel", "arbitrary"),
        name="band_attention",
    )(sinks.astype(F32), q3, k3, v3, bias)
    return out.reshape(B * S, H * HEAD_DIM)


def _diff_kernel(slope_ref, lam_ref, q_ref, k_ref, v_ref, rel_ref, g_ref, o_ref, *, tq, lam_init, chunk):
    h = pl.program_id(1)
    q0 = pl.program_id(2) * tq
    lp = lam_ref[...]
    lam = (jnp.exp(jnp.sum(lp[0:1] * lp[1:2], axis=-1, keepdims=True))
           - jnp.exp(jnp.sum(lp[2:3] * lp[3:4], axis=-1, keepdims=True)) + lam_init)
    shift = slope_ref[h] * q0.astype(F32)
    seq = k_ref.shape[0]
    s1 = _qk(q_ref[:, :DIFF_HD], k_ref[:, :DIFF_HD])
    s2 = _qk(q_ref[:, DIFF_HD:], k_ref[:, DIFF_HD:])
    t1, t2, mt1, mt2 = [], [], None, None
    for c in range(0, seq, chunk):
        b = jnp.abs(rel_ref[:, c:c + chunk] - shift)
        a1, a2 = s1[:, c:c + chunk] - b, s2[:, c:c + chunk] - b
        t1.append(a1)
        t2.append(a2)
        f1, f2 = _lane_fold(a1, jnp.maximum), _lane_fold(a2, jnp.maximum)
        mt1 = f1 if mt1 is None else jnp.maximum(mt1, f1)
        mt2 = f2 if mt2 is None else jnp.maximum(mt2, f2)
    m1 = jnp.max(mt1, axis=-1, keepdims=True)
    m2 = jnp.max(mt2, axis=-1, keepdims=True)
    lt1 = jnp.zeros((tq, LANES), F32)
    lt2 = jnp.zeros((tq, LANES), F32)
    o1 = jnp.zeros((tq, 2 * DIFF_HD), F32)
    o2 = jnp.zeros((tq, 2 * DIFF_HD), F32)
    for i, c in enumerate(range(0, seq, chunk)):
        p1, p2 = jnp.exp2(t1[i] - m1), jnp.exp2(t2[i] - m2)
        lt1 = lt1 + _lane_fold(p1, jnp.add)
        lt2 = lt2 + _lane_fold(p2, jnp.add)
        o1 = o1 + _dot(p1.astype(BF16), v_ref[c:c + chunk, :])
        o2 = o2 + _dot(p2.astype(BF16), v_ref[c:c + chunk, :])
    l1 = jnp.sum(lt1, axis=-1, keepdims=True)
    l2 = jnp.sum(lt2, axis=-1, keepdims=True)
    o = o1 * (1.0 / l1) - o2 * (lam / l2)
    r = lax.rsqrt(jnp.mean(o * o, axis=-1, keepdims=True) + EPS)
    o_ref[...] = (o * r * g_ref[...] * (1.0 - lam_init)).astype(o_ref.dtype)


def diff_rel_table(slopes, tq, S):
    rel = (jnp.arange(S)[None, :] - jnp.arange(tq)[:, None]).astype(F32)
    return (slopes * LOG2E)[:, None, None] * rel[None]


def diff_attention(mix, cols, slopes, lam_p, subln_g, B, S, layer):
    tq = _pick(S, (256, 128)) if S > 2048 else 512
    lam_init = 0.8 - 0.6 * math.exp(-0.3 * layer)
    W = 2 * DIFF_HD
    qb, kb, vb = (cols[n] // W for n in ("fq", "fk", "fv"))
    m3 = mix.reshape(B, S, mix.shape[1])
    rel = diff_rel_table(slopes, tq, S)
    kern = functools.partial(_diff_kernel, tq=tq, lam_init=lam_init, chunk=_pick(S, (512, 256)))
    out = pl.pallas_call(
        kern,
        grid=(B, DIFF_HEADS, S // tq),
        in_specs=[pl.BlockSpec(memory_space=pltpu.SMEM),
                  pl.BlockSpec((4, DIFF_HD), lambda b, h, i: (0, 0)),
                  pl.BlockSpec((None, tq, W), lambda b, h, i: (b, i, qb + h)),
                  pl.BlockSpec((None, S, W), lambda b, h, i: (b, 0, kb + h)),
                  pl.BlockSpec((None, S, W), lambda b, h, i: (b, 0, vb + h)),
                  pl.BlockSpec((None, tq, S), lambda b, h, i: (h, 0, 0)),
                  pl.BlockSpec((1, W), lambda b, h, i: (0, 0))],
        out_specs=pl.BlockSpec((None, tq, W), lambda b, h, i: (b, i, h)),
        out_shape=jax.ShapeDtypeStruct((B, S, DIFF_HEADS * W), BF16),
        compiler_params=_params("parallel", "parallel", "arbitrary"),
        name="diff_attention",
    )((slopes * LOG2E).astype(F32), lam_p.astype(F32), m3, m3, m3, rel, subln_g.reshape(1, W).astype(F32))
    return out.reshape(B * S, DIFF_HEADS * W)


def _mem_attn_kernel(x_ref, gm_ref, wq_ref, gq_ref, kv_ref, wo_ref, gf_ref, o_ref, hf_ref):
    def rms(t):
        return lax.rsqrt(jnp.mean(t * t, axis=-1, keepdims=True) + EPS)

    x = x_ref[...]
    hm = (x * rms(x) * gm_ref[...]).astype(BF16)
    qacc = _dot(hm, wq_ref[...])
    heads = []
    for h in range(MEM_HEADS):
        seg = slice(h * HEAD_DIM, (h + 1) * HEAD_DIM)
        y = qacc[:, seg]
        q = (y * rms(y) * gq_ref[:, seg]).astype(BF16)
        s = _qk(q, kv_ref[:, seg])
        p = jnp.exp2(s - jnp.max(s, axis=-1, keepdims=True))
        l = jnp.sum(p, axis=-1, keepdims=True)
        vseg = slice(MEM_W + h * HEAD_DIM, MEM_W + (h + 1) * HEAD_DIM)
        heads.append((_dot(p.astype(BF16), kv_ref[:, vseg]) / l).astype(BF16))
    x2 = x + _dot(jnp.concatenate(heads, axis=-1), wo_ref[...])
    o_ref[...] = x2
    hf_ref[...] = (x2 * rms(x2) * gf_ref[...]).astype(hf_ref.dtype)


def mem_attention(x, kv, lw, B, S):
    D = x.shape[1]
    Mt = kv.shape[0] // B
    tq = _pick(S, (256, 128))
    kv3 = kv.reshape(B, Mt, 2 * MEM_W)
    x3 = x.reshape(B, S, D)

    def row(n):
        return pl.BlockSpec((1, n), lambda b, i: (0, 0))

    out, hf = pl.pallas_call(
        _mem_attn_kernel,
        grid=(B, S // tq),
        in_specs=[pl.BlockSpec((None, tq, D), lambda b, i: (b, i, 0)),
                  row(D),
                  pl.BlockSpec((D, MEM_W), lambda b, i: (0, 0)),
                  row(MEM_W),
                  pl.BlockSpec((None, Mt, 2 * MEM_W), lambda b, i: (b, 0, 0)),
                  pl.BlockSpec((MEM_W, D), lambda b, i: (0, 0)),
                  row(D)],
        out_specs=[pl.BlockSpec((None, tq, D), lambda b, i: (b, i, 0)),
                   pl.BlockSpec((None, tq, D), lambda b, i: (b, i, 0))],
        out_shape=[jax.ShapeDtypeStruct((B, S, D), F32), jax.ShapeDtypeStruct((B, S, D), BF16)],
        compiler_params=_params("parallel", "arbitrary"),
        name="mem_attention",
    )(x3, lw["ln_mem_g"].reshape(1, D).astype(F32), lw["mem_wq"], lw["mem_q_gain"].reshape(1, MEM_W),
      kv3, lw["mem_wo"], lw["ln_ffn_g"].reshape(1, D).astype(F32))
    return out.reshape(B * S, D), hf.reshape(B * S, D)


def alibi_slopes(n):
    return 2.0 ** (-8.0 * jnp.arange(1, n + 1, dtype=F32) / n)


def _mix_layout():
    names = ("qa", "kva", "kpe", "dq", "dk", "dv", "wq", "wk", "wv", "fq", "fk", "fv")
    src = {n: (MIX_OFFSETS[i], MIX_OFFSETS[i + 1]) for i, n in enumerate(names)}
    order = ("dq", "dk", "dv", "fq", "fk", "fv", "wq", "qa", "wk", "wv", "kva", "kpe")
    cols, off = {}, 0
    for n in order:
        width = LANES if n == "kpe" else src[n][1] - src[n][0]
        assert off % min(width, BRANCH_W) == 0, (n, off, width)
        cols[n] = off
        off += width
    total = -(-off // MIX_TILE) * MIX_TILE
    return src, order, cols, off, total


def pack_layer(p, l):
    src, order, cols, used, total = _mix_layout()
    w_in = p["w_in"][l].astype(BF16)
    half = MLA_ROPE // 2
    pieces, gains, flags = [], [], []
    qscale = HEAD_DIM ** -0.5 * LOG2E

    def tile(g, n):
        return jnp.tile(g.astype(F32), n)

    norm_gain = {
        "dq": tile(p["dil_qk_g"][l, 0], DIL_HEADS) * qscale, "dk": tile(p["dil_qk_g"][l, 1], DIL_HEADS),
        "fq": tile(p["diff_qk_g"][l, 0], 2 * DIFF_HEADS) * (DIFF_HD ** -0.5 * LOG2E),
        "fk": tile(p["diff_qk_g"][l, 1], 2 * DIFF_HEADS),
        "wq": tile(p["win_qk_g"][l, 0], WIN_Q_HEADS) * qscale, "wk": tile(p["win_qk_g"][l, 1], WIN_KV_HEADS),
    }
    for n in order:
        a, b = src[n]
        w = w_in[:, a:b]
        if n == "kpe":
            x1, x2 = w[:, :half], w[:, half:]
            w = jnp.concatenate([x1, x2, x2, x1], axis=1)
        pieces.append(w)
        width = w.shape[1]
        if n in norm_gain:
            gains.append(norm_gain[n])
            flags.append(jnp.ones((width,), F32))
        else:
            gains.append(jnp.ones((width,), F32))
            flags.append(jnp.zeros((width,), F32))
    pad = total - used
    pieces.append(jnp.zeros((D_MODEL, pad), BF16))
    gains.append(jnp.ones((pad,), F32))
    flags.append(jnp.zeros((pad,), F32))
    pieces.append(w_in[:, MIX_COLS:])

    wq3 = p["mla_wq_up"][l].reshape(MLA_Q_LORA, MLA_HEADS, MLA_QK)
    nope, x1, x2 = wq3[:, :, :MLA_NOPE], wq3[:, :, MLA_NOPE:MLA_NOPE + half], wq3[:, :, MLA_NOPE + half:]
    z = jnp.zeros((MLA_Q_LORA, MLA_HEADS, LANES - MLA_ROPE), F32)
    wq_full = jnp.concatenate([nope, x1, x2, z], axis=-1).reshape(MLA_Q_LORA, MLA_HEADS * MLA_QK_PAD)
    wq_sw = jnp.concatenate([x2, x1, z], axis=-1).reshape(MLA_Q_LORA, MLA_HEADS * LANES)
    wkv3 = p["mla_wkv_up"][l].reshape(MLA_KV_LORA, MLA_HEADS, MLA_NOPE + MLA_V)
    zg = jnp.zeros((LANES - MLA_ROPE,), F32)
    qk_g = p["mla_qk_g"][l].astype(F32)

    return {
        "ln_mix_g": p["ln_mix_g"][l],
        "w_in": jnp.concatenate(pieces, axis=1),
        "mix_gain": jnp.concatenate(gains), "mix_flag": jnp.concatenate(flags),
        "mla_qa_g": p["mla_qa_g"][l].reshape(1, -1).astype(F32),
        "mla_kva_g": p["mla_kva_g"][l].reshape(1, -1).astype(F32),
        "mla_wq": wq_full.astype(BF16), "mla_wq_sw": wq_sw.astype(BF16),
        "mla_wk": wkv3[:, :, :MLA_NOPE].reshape(MLA_KV_LORA, -1).astype(BF16),
        "mla_wv": wkv3[:, :, MLA_NOPE:].reshape(MLA_KV_LORA, -1).astype(BF16),
        "mla_gq": (jnp.concatenate([qk_g[0], zg]) * (MLA_QK ** -0.5 * LOG2E)).reshape(1, -1),
        "mla_gk": jnp.concatenate([qk_g[1], zg]).reshape(1, -1),
        "win_sink": p["win_sink"][l], "diff_lambda": p["diff_lambda"][l], "diff_subln_g": p["diff_subln_g"][l],
        "w_branch": p["w_branch"][l].astype(BF16), "w_out": p["w_out"][l].astype(BF16),
        "ln_mem_g": p["ln_mem_g"][l], "mem_ln_g": p["mem_ln_g"][l],
        "mem_wq": p["mem_wq"][l].astype(BF16), "mem_wkv": p["mem_wkv"][l].astype(BF16),
        "mem_q_gain": tile(p["mem_qk_g"][l, 0], MEM_HEADS) * (HEAD_DIM ** -0.5 * LOG2E),
        "mem_kv_gain": jnp.concatenate([tile(p["mem_qk_g"][l, 1], MEM_HEADS), jnp.ones((MEM_W,), F32)]),
        "mem_kv_flag": jnp.concatenate([jnp.ones((MEM_W,), F32), jnp.zeros((MEM_W,), F32)]),
        "mem_wo": p["mem_wo"][l].astype(BF16),
        "ln_ffn_g": p["ln_ffn_g"][l],
        "ffn_w_in": p["ffn_w_in"][l].astype(BF16), "ffn_w_out": p["ffn_w_out"][l].astype(BF16),
    }


def rotary_tables(S):
    half = MLA_ROPE // 2
    inv_freq = ROPE_THETA ** (-jnp.arange(half, dtype=F32) / half)
    ang = jnp.arange(S, dtype=F32)[:, None] * inv_freq[None, :]
    cos, sin = jnp.cos(ang), jnp.sin(ang)
    z = jnp.zeros((S, LANES - MLA_ROPE), F32)
    return jnp.concatenate([cos, cos, z], axis=1), jnp.concatenate([-sin, sin, z], axis=1)


def _trunk(x, mem, layers, tables):
    B, S, D = x.shape
    M = B * S
    _, _, cols, _, mix_cols = _mix_layout()
    ct, st = rotary_tables(S)
    xf = x.reshape(M, D)
    memf = mem.reshape(-1, D)
    dil_tq = _pick(S, (256,))
    for l, lw in enumerate(layers):
        h = rmsnorm(xf, lw["ln_mix_g"])
        mix = mm_segnorm(h, lw["w_in"], lw["mix_gain"], lw["mix_flag"], tn_prefs=(MIX_TILE,))
        gates = mm_sigmoid(h, lw["w_in"], N_BRANCH * D, mix_cols)

        q, k, v = mla_prep(mix, cols, S, ct, st, lw)
        o_mla = dense_attention(q, k, v, B, S, MLA_HEADS, MLA_QK_PAD, MLA_V)
        o_dil = band_attention(mix, cols["dq"], mix, cols["dk"], mix, cols["dv"], tables["dil_bias"],
                               jnp.full((DIL_HEADS,), NEG_INF, F32), B, S, DIL_HEADS, 1, DIL_HEADS // 2,
                               dil_tq, dil_tq, DIL_RADIUS)
        o_win = band_attention(mix, cols["wq"], mix, cols["wk"], mix, cols["wv"], tables["win_bias"],
                               lw["win_sink"], B, S, WIN_Q_HEADS, WIN_Q_HEADS // WIN_KV_HEADS, WIN_Q_HEADS,
                               2 * WIN_RADIUS, WIN_RADIUS, WIN_RADIUS)
        o_diff = diff_attention(mix, cols, tables["slopes_diff"], lw["diff_lambda"], lw["diff_subln_g"], B, S, l)

        merged = branch_merge((o_mla, o_dil, o_win, o_diff), lw["w_branch"], gates)
        xf = mm_residual(merged, lw["w_out"], xf)

        kvm = mm_segnorm(rmsnorm(memf, lw["mem_ln_g"]), lw["mem_wkv"], lw["mem_kv_gain"], lw["mem_kv_flag"])
        xf, hf = mem_attention(xf, kvm, lw, B, S)

        hid = mm_swiglu(hf, lw["ffn_w_in"], FFN_HIDDEN)
        xf = mm_residual(hid, lw["ffn_w_out"], xf, tm_prefs=(512, 256, 128), tn_prefs=(256, 128))
    return xf.reshape(B, S, D)


def kernel(x_prompt, x_sample, mem_prompt, mem_sample, ln_mix_g, w_in, mla_qa_g, mla_kva_g, mla_wq_up,
           mla_wkv_up, mla_qk_g, dil_qk_g, win_qk_g, win_sink, diff_qk_g, diff_lambda, diff_subln_g,
           w_branch, w_out, ln_mem_g, mem_ln_g, mem_wq, mem_wkv, mem_qk_g, mem_wo, ln_ffn_g, ffn_w_in,
           ffn_w_out):
    p = dict(ln_mix_g=ln_mix_g, w_in=w_in, mla_qa_g=mla_qa_g, mla_kva_g=mla_kva_g, mla_wq_up=mla_wq_up,
             mla_wkv_up=mla_wkv_up, mla_qk_g=mla_qk_g, dil_qk_g=dil_qk_g, win_qk_g=win_qk_g,
             win_sink=win_sink, diff_qk_g=diff_qk_g, diff_lambda=diff_lambda, diff_subln_g=diff_subln_g,
             w_branch=w_branch, w_out=w_out, ln_mem_g=ln_mem_g, mem_ln_g=mem_ln_g, mem_wq=mem_wq,
             mem_wkv=mem_wkv, mem_qk_g=mem_qk_g, mem_wo=mem_wo, ln_ffn_g=ln_ffn_g, ffn_w_in=ffn_w_in,
             ffn_w_out=ffn_w_out)
    layers = [pack_layer(p, l) for l in range(DEPTH)]
    tables = {
        "dil_bias": band_bias(alibi_slopes(DIL_HEADS), dil_multiplicity, 256, 256, DIL_RADIUS),
        "win_bias": band_bias(alibi_slopes(WIN_Q_HEADS), win_multiplicity, 2 * WIN_RADIUS, WIN_RADIUS,
                              WIN_RADIUS),
        "slopes_diff": alibi_slopes(DIFF_HEADS),
    }
    y_prompt = _trunk(x_prompt, mem_prompt, layers, tables)
    y_sample = _trunk(x_sample, mem_sample, layers, tables)
    return (y_prompt, y_sample)
```

```python
import functools
import math

import jax
import jax.numpy as jnp
import numpy as np
from jax import lax
from jax.experimental import pallas as pl
from jax.experimental.pallas import tpu as pltpu

F32 = jnp.float32
BF16 = jnp.bfloat16

D_MODEL = 4096
DEPTH = 2
EPS = 1e-6
NEG_INF = -1e30
N_BRANCH = 4
BRANCH_W = D_MODEL // N_BRANCH
HEAD_DIM = 128

MLA_NOPE = 128
MLA_ROPE = 64
MLA_V = 128
MLA_HEADS = BRANCH_W // MLA_V
MLA_Q_LORA = D_MODEL // 4
MLA_KV_LORA = D_MODEL // 8
MLA_QK = MLA_NOPE + MLA_ROPE
MLA_QK_PAD = 256
ROPE_THETA = 10000.0

DIL_HEADS = BRANCH_W // HEAD_DIM
DIL_PATTERNS = ((128, 1), (512, 4), (2048, 16))
DIL_RADIUS = max(w // 2 for w, _ in DIL_PATTERNS)

WIN_Q_HEADS = BRANCH_W // HEAD_DIM
WIN_KV_HEADS = WIN_Q_HEADS // 4
WIN_RADIUS = 128

DIFF_HD = 128
DIFF_HEADS = BRANCH_W // (2 * DIFF_HD)

MEM_HEADS = 4
MEM_W = MEM_HEADS * HEAD_DIM

FFN_HIDDEN = -(-8 * D_MODEL // (3 * 256)) * 256

MIX_SPLITS = (MLA_Q_LORA, MLA_KV_LORA, MLA_ROPE,
              BRANCH_W, BRANCH_W, BRANCH_W,
              BRANCH_W, WIN_KV_HEADS * HEAD_DIM, WIN_KV_HEADS * HEAD_DIM,
              BRANCH_W, BRANCH_W, BRANCH_W)
MIX_COLS = sum(MIX_SPLITS)
MIX_OFFSETS = tuple(int(o) for o in np.cumsum((0,) + MIX_SPLITS))

LOG2E = math.log2(math.e)
LANES = 128
MIX_TILE = 512
VMEM_LIMIT_BYTES = 56 * 1024 * 1024


def _pick(n, prefs):
    for p in prefs:
        if n % p == 0:
            return p
    raise ValueError(f"no tile in {prefs} divides {n}")


def _params(*sem):
    return pltpu.CompilerParams(dimension_semantics=sem, vmem_limit_bytes=VMEM_LIMIT_BYTES)


def _rmsnorm_kernel(x_ref, g_ref, o_ref):
    x = x_ref[...].astype(F32)
    ms = jnp.mean(x * x, axis=-1, keepdims=True)
    o_ref[...] = (x * lax.rsqrt(ms + EPS) * g_ref[...]).astype(o_ref.dtype)


def rmsnorm(x, g):
    M, D = x.shape
    tm = _pick(M, (512, 256, 128, 64, 8))
    return pl.pallas_call(
        _rmsnorm_kernel,
        grid=(M // tm,),
        in_specs=[pl.BlockSpec((tm, D), lambda i: (i, 0)),
                  pl.BlockSpec((1, D), lambda i: (0, 0))],
        out_specs=pl.BlockSpec((tm, D), lambda i: (i, 0)),
        out_shape=jax.ShapeDtypeStruct((M, D), BF16),
        compiler_params=_params("parallel"),
        name="rmsnorm",
    )(x, g.reshape(1, D).astype(F32))


def _dot(a, b):
    return jnp.dot(a, b, preferred_element_type=F32)


MXU_COLS = 256


def _sigmoid(x):
    return 0.5 * jnp.tanh(0.5 * x) + 0.5


def _col_slabs(n):
    w = MXU_COLS if n % MXU_COLS == 0 else n
    return [slice(c, c + w) for c in range(0, n, w)]


def _mm_sigmoid_kernel(a_ref, b_ref, o_ref):
    for cs in _col_slabs(o_ref.shape[1]):
        o_ref[:, cs] = _sigmoid(_dot(a_ref[...], b_ref[:, cs])).astype(o_ref.dtype)


def _mm_segnorm_kernel(a_ref, b_ref, g_ref, f_ref, o_ref):
    for cs in _col_slabs(o_ref.shape[1]):
        acc = _dot(a_ref[...], b_ref[:, cs])
        for c in range(0, acc.shape[1], LANES):
            seg = slice(cs.start + c, cs.start + c + LANES)
            y = acc[:, c:c + LANES]
            r = lax.rsqrt(jnp.mean(y * y, axis=-1, keepdims=True) + EPS)
            mult = jnp.where(f_ref[:, seg] > 0.0, r, 1.0) * g_ref[:, seg]
            o_ref[:, seg] = (y * mult).astype(o_ref.dtype)


def _mm_residual_kernel(a_ref, b_ref, r_ref, o_ref):
    for cs in _col_slabs(o_ref.shape[1]):
        o_ref[:, cs] = r_ref[:, cs] + _dot(a_ref[...], b_ref[:, cs])


def _mm_swiglu_kernel(a_ref, bg_ref, bu_ref, o_ref):
    half = a_ref.shape[0] // 2
    for rows in (slice(0, half), slice(half, 2 * half)):
        a = a_ref[rows, :]
        g = _dot(a, bg_ref[...])
        u = _dot(a, bu_ref[...])
        o_ref[rows, :] = (g * _sigmoid(g) * u).astype(o_ref.dtype)


def mm_sigmoid(a, w, tm_prefs=(1024, 512, 256, 128), tn_prefs=(512, 256, 128)):
    M, K = a.shape
    N = w.shape[1]
    tm, tn = _pick(M, tm_prefs), _pick(N, tn_prefs)
    return pl.pallas_call(
        _mm_sigmoid_kernel,
        grid=(M // tm, N // tn),
        in_specs=[pl.BlockSpec((tm, K), lambda i, j: (i, 0)),
                  pl.BlockSpec((K, tn), lambda i, j: (0, j))],
        out_specs=pl.BlockSpec((tm, tn), lambda i, j: (i, j)),
        out_shape=jax.ShapeDtypeStruct((M, N), BF16),
        compiler_params=_params("parallel", "arbitrary"),
        name="mm_sigmoid",
    )(a, w)


def mm_segnorm(a, w, gain, flag, tm_prefs=(1024, 512, 256, 128), tn_prefs=(512, 256, 128)):
    M, K = a.shape
    N = gain.shape[0]
    tm, tn = _pick(M, tm_prefs), _pick(N, tn_prefs)
    return pl.pallas_call(
        _mm_segnorm_kernel,
        grid=(M // tm, N // tn),
        in_specs=[pl.BlockSpec((tm, K), lambda i, j: (i, 0)),
                  pl.BlockSpec((K, tn), lambda i, j: (0, j)),
                  pl.BlockSpec((1, tn), lambda i, j: (0, j)),
                  pl.BlockSpec((1, tn), lambda i, j: (0, j))],
        out_specs=pl.BlockSpec((tm, tn), lambda i, j: (i, j)),
        out_shape=jax.ShapeDtypeStruct((M, N), BF16),
        compiler_params=_params("parallel", "arbitrary"),
        name="mm_segnorm",
    )(a, w, gain.reshape(1, N).astype(F32), flag.reshape(1, N).astype(F32))


def mm_residual(a, w, res, tm_prefs=(1024, 512, 256, 128), tn_prefs=(512, 256, 128)):
    M, K = a.shape
    N = w.shape[1]
    tm, tn = _pick(M, tm_prefs), _pick(N, tn_prefs)
    return pl.pallas_call(
        _mm_residual_kernel,
        grid=(M // tm, N // tn),
        in_specs=[pl.BlockSpec((tm, K), lambda i, j: (i, 0)),
                  pl.BlockSpec((K, tn), lambda i, j: (0, j)),
                  pl.BlockSpec((tm, tn), lambda i, j: (i, j))],
        out_specs=pl.BlockSpec((tm, tn), lambda i, j: (i, j)),
        out_shape=jax.ShapeDtypeStruct((M, N), F32),
        compiler_params=_params("parallel", "arbitrary"),
        name="mm_residual",
    )(a, w, res)


def mm_swiglu(a, w, hidden):
    M, K = a.shape
    tm = _pick(M, (1024, 512, 256, 128))
    tn = _pick(hidden, (256, 128))
    nj = hidden // tn
    return pl.pallas_call(
        _mm_swiglu_kernel,
        grid=(M // tm, nj),
        in_specs=[pl.BlockSpec((tm, K), lambda i, j: (i, 0)),
                  pl.BlockSpec((K, tn), lambda i, j: (0, j)),
                  pl.BlockSpec((K, tn), lambda i, j: (0, j + nj))],
        out_specs=pl.BlockSpec((tm, tn), lambda i, j: (i, j)),
        out_shape=jax.ShapeDtypeStruct((M, hidden), BF16),
        compiler_params=_params("parallel", "arbitrary"),
        name="mm_swiglu",
    )(a, w, w)


def _branch_kernel(oa_ref, ob_ref, oc_ref, od_ref, w_ref, ga_ref, gb_ref, gc_ref, gd_ref, o_ref):
    acc = ga_ref[...].astype(F32) * _dot(oa_ref[...], w_ref[0])
    acc += gb_ref[...].astype(F32) * _dot(ob_ref[...], w_ref[1])
    acc += gc_ref[...].astype(F32) * _dot(oc_ref[...], w_ref[2])
    acc += gd_ref[...].astype(F32) * _dot(od_ref[...], w_ref[3])
    o_ref[...] = acc.astype(o_ref.dtype)


def branch_merge(outs, w_branch, gates):
    M, W = outs[0].shape
    D = w_branch.shape[2]
    tm = _pick(M, (1024, 512, 256, 128))
    tn = _pick(D, (512, 256, 128))
    nj = D // tn
    o_spec = pl.BlockSpec((tm, W), lambda i, j: (i, 0))
    g_specs = [pl.BlockSpec((tm, tn), functools.partial(lambda i, j, b: (i, b * nj + j), b=b))
               for b in range(N_BRANCH)]
    return pl.pallas_call(
        _branch_kernel,
        grid=(M // tm, nj),
        in_specs=[o_spec] * N_BRANCH + [pl.BlockSpec((N_BRANCH, W, tn), lambda i, j: (0, 0, j))] + g_specs,
        out_specs=pl.BlockSpec((tm, tn), lambda i, j: (i, j)),
        out_shape=jax.ShapeDtypeStruct((M, D), BF16),
        compiler_params=_params("parallel", "arbitrary"),
        name="branch_merge",
    )(*outs, w_branch, gates, gates, gates, gates)


def _mla_prep_kernel(qa_ref, kva_ref, kpe_ref, ct_ref, st_ref, qag_ref, kvag_ref,
                     wq_ref, wqs_ref, wk_ref, wv_ref, gq_ref, gk_ref,
                     q_ref, k_ref, v_ref):
    def norm(x, g):
        return (x * lax.rsqrt(jnp.mean(x * x, axis=-1, keepdims=True) + EPS) * g).astype(BF16)

    ct = ct_ref[...]
    st = st_ref[...]
    qn = norm(qa_ref[...].astype(F32), qag_ref[...])
    kvn = norm(kva_ref[...].astype(F32), kvag_ref[...])
    qfull = _dot(qn, wq_ref[...])
    qsw = _dot(qn, wqs_ref[...])
    knope = _dot(kvn, wk_ref[...])
    vals = _dot(kvn, wv_ref[...])
    ones = jnp.ones((vals.shape[0], MLA_V), v_ref.dtype)
    for h in range(MLA_HEADS):
        v_ref[:, 2 * h * MLA_V:(2 * h + 1) * MLA_V] = vals[:, h * MLA_V:(h + 1) * MLA_V].astype(v_ref.dtype)
        v_ref[:, (2 * h + 1) * MLA_V:(2 * h + 2) * MLA_V] = ones

    kpe = kpe_ref[...].astype(F32)
    kpe_rot = kpe * ct + pltpu.roll(kpe, 2 * (MLA_ROPE // 2), 1) * st
    kpe_ss = jnp.sum(kpe_rot * kpe_rot, axis=-1, keepdims=True)
    gq_n, gq_r = gq_ref[:, :LANES], gq_ref[:, LANES:]
    gk_n, gk_r = gk_ref[:, :LANES], gk_ref[:, LANES:]
    for h in range(MLA_HEADS):
        lo = h * MLA_QK_PAD
        q_n = qfull[:, lo:lo + LANES]
        q_r = qfull[:, lo + LANES:lo + 2 * LANES] * ct + qsw[:, h * LANES:(h + 1) * LANES] * st
        ss = jnp.sum(q_n * q_n, axis=-1, keepdims=True) + jnp.sum(q_r * q_r, axis=-1, keepdims=True)
        r = lax.rsqrt(ss * (1.0 / MLA_QK) + EPS)
        q_ref[:, lo:lo + LANES] = (q_n * r * gq_n).astype(q_ref.dtype)
        q_ref[:, lo + LANES:lo + 2 * LANES] = (q_r * r * gq_r).astype(q_ref.dtype)
        k_n = knope[:, h * LANES:(h + 1) * LANES]
        ss = jnp.sum(k_n * k_n, axis=-1, keepdims=True) + kpe_ss
        r = lax.rsqrt(ss * (1.0 / MLA_QK) + EPS)
        k_ref[:, lo:lo + LANES] = (k_n * r * gk_n).astype(k_ref.dtype)
        k_ref[:, lo + LANES:lo + 2 * LANES] = (kpe_rot * r * gk_r).astype(k_ref.dtype)


def mla_prep(mix, cols, S, ct, st, lw):
    M = mix.shape[0]
    tm = _pick(S, (256, 128))
    ns = S // tm
    HQ = MLA_HEADS * MLA_QK_PAD
    HV = MLA_HEADS * 2 * MLA_V

    def col(width, off):
        assert off % width == 0
        return pl.BlockSpec((tm, width), lambda i: (i, off // width))

    def whole(a):
        return pl.BlockSpec(a.shape, lambda i: (0,) * a.ndim)

    consts = (lw["mla_qa_g"], lw["mla_kva_g"], lw["mla_wq"], lw["mla_wq_sw"], lw["mla_wk"], lw["mla_wv"],
              lw["mla_gq"], lw["mla_gk"])
    return pl.pallas_call(
        _mla_prep_kernel,
        grid=(M // tm,),
        in_specs=[col(MLA_Q_LORA, cols["qa"]), col(MLA_KV_LORA, cols["kva"]), col(LANES, cols["kpe"]),
                  pl.BlockSpec((tm, LANES), lambda i: (i % ns, 0)),
                  pl.BlockSpec((tm, LANES), lambda i: (i % ns, 0))] + [whole(c) for c in consts],
        out_specs=[pl.BlockSpec((tm, HQ), lambda i: (i, 0)),
                   pl.BlockSpec((tm, HQ), lambda i: (i, 0)),
                   pl.BlockSpec((tm, HV), lambda i: (i, 0))],
        out_shape=[jax.ShapeDtypeStruct((M, HQ), BF16),
                   jax.ShapeDtypeStruct((M, HQ), BF16),
                   jax.ShapeDtypeStruct((M, HV), BF16)],
        compiler_params=_params("parallel"),
        name="mla_prep",
    )(mix, mix, mix, ct, st, *consts)


def _qk(q, k):
    return lax.dot_general(q, k, (((1,), (1,)), ((), ())), preferred_element_type=F32)


def _dense_attn_kernel(q_ref, k_ref, v_ref, o_ref, *, n_sub, chunk):
    dv = o_ref.shape[1]
    ts = q_ref.shape[0] // n_sub
    seq = k_ref.shape[0]
    for t in range(n_sub):
        rows = slice(t * ts, (t + 1) * ts)
        s = _qk(q_ref[rows, :], k_ref[...])
        pieces = [s[:, c:c + chunk] for c in range(0, seq, chunk)]
        mt = _lane_fold(pieces[0], jnp.maximum)
        for piece in pieces[1:]:
            mt = jnp.maximum(mt, _lane_fold(piece, jnp.maximum))
        m = jnp.max(mt, axis=-1, keepdims=True)
        o = jnp.zeros((ts, 2 * dv), F32)
        for c, piece in zip(range(0, seq, chunk), pieces):
            o = o + _dot(jnp.exp2(piece - m).astype(BF16), v_ref[c:c + chunk, :])
        o_ref[rows, :] = (o[:, :dv] / o[:, dv:]).astype(o_ref.dtype)


def dense_attention(q, k, v, B, S, H, dk, dv, n_sub=4):
    tq = _pick(S, (1024, 512, 256))
    chunk = _pick(S, (512, 256))
    q3, k3, v3 = (t.reshape(B, S, t.shape[1]) for t in (q, k, v))
    out = pl.pallas_call(
        functools.partial(_dense_attn_kernel, n_sub=n_sub, chunk=chunk),
        grid=(B, H, S // tq),
        in_specs=[pl.BlockSpec((None, tq, dk), lambda b, h, i: (b, i, h)),
                  pl.BlockSpec((None, S, dk), lambda b, h, i: (b, 0, h)),
                  pl.BlockSpec((None, S, 2 * dv), lambda b, h, i: (b, 0, h))],
        out_specs=pl.BlockSpec((None, tq, dv), lambda b, h, i: (b, i, h)),
        out_shape=jax.ShapeDtypeStruct((B, S, H * dv), BF16),
        compiler_params=_params("parallel", "parallel", "arbitrary"),
        name="dense_attention",
    )(q3, k3, v3)
    return out.reshape(B * S, H * dv)


def _lane_fold(x, op):
    out = x[:, :LANES]
    for j in range(1, x.shape[1] // LANES):
        out = op(out, x[:, j * LANES:(j + 1) * LANES])
    return out


def _band_kernel(sink_ref, q_ref, k_ref, v_ref, bias_ref, o_ref, *, tq, tk, radius, seq, n_chunks, hpb, rep):
    hb = pl.program_id(1)
    q0 = pl.program_id(2) * tq
    starts, tiles = [], []
    for c in range(n_chunks):
        start = q0 - radius + c * tk
        inside = jnp.logical_and(start >= 0, start + tk <= seq)
        starts.append(pl.multiple_of(jnp.clip(start, 0, seq - tk), tk))
        tiles.append(jnp.where(inside, c, n_chunks))
    for hh in range(hpb):
        qs = slice(hh * HEAD_DIM, (hh + 1) * HEAD_DIM)
        ks = slice((hh // rep) * HEAD_DIM, (hh // rep + 1) * HEAD_DIM)
        q = q_ref[:, qs]
        scores = [_qk(q, k_ref[pl.ds(sc, tk), ks]) + bias_ref[hh, t] for sc, t in zip(starts, tiles)]
        mt = _lane_fold(scores[0], jnp.maximum)
        for s in scores[1:]:
            mt = jnp.maximum(mt, _lane_fold(s, jnp.maximum))
        sink = sink_ref[hb * hpb + hh] * LOG2E
        m = jnp.maximum(jnp.max(mt, axis=-1, keepdims=True), sink)
        lt = jnp.zeros((tq, LANES), F32)
        acc = jnp.zeros((tq, HEAD_DIM), F32)
        for s, sc in zip(scores, starts):
            p = jnp.exp2(s - m)
            lt = lt + _lane_fold(p, jnp.add)
            acc = acc + _dot(p.astype(BF16), v_ref[pl.ds(sc, tk), ks])
        l = jnp.sum(lt, axis=-1, keepdims=True) + jnp.exp2(sink - m)
        o_ref[:, qs] = (acc / l).astype(o_ref.dtype)


def band_bias(slopes, mult_fn, tq, tk, radius):
    n_chunks = (tq + 2 * radius) // tk
    a = jnp.arange(tq)[None, :, None]
    c = jnp.arange(tk)[None, None, :]
    delta = (jnp.arange(n_chunks)[:, None, None] * tk - radius) + c - a
    mult = mult_fn(delta)
    dist = jnp.abs(delta).astype(F32)
    logm = jnp.log2(jnp.maximum(mult, 1).astype(F32))
    bias = logm[None] - (slopes * LOG2E)[:, None, None, None] * dist[None]
    bias = jnp.where(mult[None] > 0, bias, NEG_INF)
    dead = jnp.full((slopes.shape[0], 1, tq, tk), NEG_INF, F32)
    return jnp.concatenate([bias, dead], axis=1)


def dil_multiplicity(delta):
    m = jnp.zeros(delta.shape, jnp.int32)
    for window, dil in DIL_PATTERNS:
        m = m + ((delta % dil == 0) & (jnp.abs(delta) <= window // 2)).astype(jnp.int32)
    return m


def win_multiplicity(delta):
    return (jnp.abs(delta) <= WIN_RADIUS).astype(jnp.int32)


def band_attention(qarr, q_off, karr, k_off, varr, v_off, bias, sinks, B, S, H, rep, hpb, tq, tk, radius):
    n_chunks = bias.shape[1] - 1
    assert radius % tk == 0 and tq % tk == 0 and S % tq == 0 and S >= tk and hpb % rep == 0 and H % hpb == 0
    wq, wk = hpb * HEAD_DIM, (hpb // rep) * HEAD_DIM
    assert q_off % wq == 0 and k_off % wk == 0 and v_off % wk == 0
    qb, kb, vb = q_off // wq, k_off // wk, v_off // wk
    q3, k3, v3 = (t.reshape(B, S, t.shape[1]) for t in (qarr, karr, varr))
    kern = functools.partial(_band_kernel, tq=tq, tk=tk, radius=radius, seq=S, n_chunks=n_chunks, hpb=hpb, rep=rep)
    out = pl.pallas_call(
        kern,
        grid=(B, H // hpb, S // tq),
        in_specs=[pl.BlockSpec(memory_space=pltpu.SMEM),
                  pl.BlockSpec((None, tq, wq), lambda b, h, i: (b, i, qb + h)),
                  pl.BlockSpec((None, S, wk), lambda b, h, i: (b, 0, kb + h)),
                  pl.BlockSpec((None, S, wk), lambda b, h, i: (b, 0, vb + h)),
                  pl.BlockSpec((hpb, n_chunks + 1, tq, tk), lambda b, h, i: (h, 0, 0, 0))],
        out_specs=pl.BlockSpec((None, tq, wq), lambda b, h, i: (b, i, h)),
        out_shape=jax.ShapeDtypeStruct((B, S, H * HEAD_DIM), BF16),
        compiler_params=_params("parallel", "parallel", "arbitrary"),
        name="band_attention",
    )(sinks.astype(F32), q3, k3, v3, bias)
    return out.reshape(B * S, H * HEAD_DIM)


def _diff_kernel(slope_ref, lam_ref, q_ref, k_ref, v_ref, pos_ref, g_ref, o_ref, *, tq, lam_init, chunk):
    h = pl.program_id(1)
    q0 = pl.program_id(2) * tq
    lp = lam_ref[...]
    lam = (jnp.exp(jnp.sum(lp[0:1] * lp[1:2], axis=-1, keepdims=True))
           - jnp.exp(jnp.sum(lp[2:3] * lp[3:4], axis=-1, keepdims=True)) + lam_init)
    slope = slope_ref[h]
    seq = k_ref.shape[0]
    kpos = pos_ref[...] * slope
    qrow = (q0 + lax.broadcasted_iota(jnp.int32, (tq, LANES), 0)).astype(F32) * slope
    qpos = jnp.concatenate([qrow] * (chunk // LANES), axis=1)
    s1 = _qk(q_ref[:, :DIFF_HD], k_ref[:, :DIFF_HD])
    s2 = _qk(q_ref[:, DIFF_HD:], k_ref[:, DIFF_HD:])
    t1, t2, mt1, mt2 = [], [], None, None
    for c in range(0, seq, chunk):
        b = jnp.abs(kpos[:, c:c + chunk] - qpos)
        a1, a2 = s1[:, c:c + chunk] - b, s2[:, c:c + chunk] - b
        t1.append(a1)
        t2.append(a2)
        f1, f2 = _lane_fold(a1, jnp.maximum), _lane_fold(a2, jnp.maximum)
        mt1 = f1 if mt1 is None else jnp.maximum(mt1, f1)
        mt2 = f2 if mt2 is None else jnp.maximum(mt2, f2)
    m1 = jnp.max(mt1, axis=-1, keepdims=True)
    m2 = jnp.max(mt2, axis=-1, keepdims=True)
    lt1 = jnp.zeros((tq, LANES), F32)
    lt2 = jnp.zeros((tq, LANES), F32)
    o1 = jnp.zeros((tq, 2 * DIFF_HD), F32)
    o2 = jnp.zeros((tq, 2 * DIFF_HD), F32)
    for i, c in enumerate(range(0, seq, chunk)):
        p1, p2 = jnp.exp2(t1[i] - m1), jnp.exp2(t2[i] - m2)
        lt1 = lt1 + _lane_fold(p1, jnp.add)
        lt2 = lt2 + _lane_fold(p2, jnp.add)
        o1 = o1 + _dot(p1.astype(BF16), v_ref[c:c + chunk, :])
        o2 = o2 + _dot(p2.astype(BF16), v_ref[c:c + chunk, :])
    l1 = jnp.sum(lt1, axis=-1, keepdims=True)
    l2 = jnp.sum(lt2, axis=-1, keepdims=True)
    o = o1 * (1.0 / l1) - o2 * (lam / l2)
    r = lax.rsqrt(jnp.mean(o * o, axis=-1, keepdims=True) + EPS)
    o_ref[...] = (o * r * g_ref[...] * (1.0 - lam_init)).astype(o_ref.dtype)


def diff_attention(mix, cols, slopes, lam_p, subln_g, B, S, layer):
    tq = _pick(S, (512, 256, 128))
    lam_init = 0.8 - 0.6 * math.exp(-0.3 * layer)
    W = 2 * DIFF_HD
    qb, kb, vb = (cols[n] // W for n in ("fq", "fk", "fv"))
    m3 = mix.reshape(B, S, mix.shape[1])
    pos = jnp.arange(S, dtype=F32).reshape(1, S)
    kern = functools.partial(_diff_kernel, tq=tq, lam_init=lam_init, chunk=_pick(S, (512, 256)))
    out = pl.pallas_call(
        kern,
        grid=(B, DIFF_HEADS, S // tq),
        in_specs=[pl.BlockSpec(memory_space=pltpu.SMEM),
                  pl.BlockSpec((4, DIFF_HD), lambda b, h, i: (0, 0)),
                  pl.BlockSpec((None, tq, W), lambda b, h, i: (b, i, qb + h)),
                  pl.BlockSpec((None, S, W), lambda b, h, i: (b, 0, kb + h)),
                  pl.BlockSpec((None, S, W), lambda b, h, i: (b, 0, vb + h)),
                  pl.BlockSpec((1, S), lambda b, h, i: (0, 0)),
                  pl.BlockSpec((1, W), lambda b, h, i: (0, 0))],
        out_specs=pl.BlockSpec((None, tq, W), lambda b, h, i: (b, i, h)),
        out_shape=jax.ShapeDtypeStruct((B, S, DIFF_HEADS * W), BF16),
        compiler_params=_params("parallel", "parallel", "arbitrary"),
        name="diff_attention",
    )((slopes * LOG2E).astype(F32), lam_p.astype(F32), m3, m3, m3, pos, subln_g.reshape(1, W).astype(F32))
    return out.reshape(B * S, DIFF_HEADS * W)


def _mem_attn_kernel(x_ref, gm_ref, wq_ref, gq_ref, kv_ref, wo_ref, gf_ref, o_ref, hf_ref):
    def rms(t):
        return lax.rsqrt(jnp.mean(t * t, axis=-1, keepdims=True) + EPS)

    x = x_ref[...]
    hm = (x * rms(x) * gm_ref[...]).astype(BF16)
    qacc = _dot(hm, wq_ref[...])
    heads = []
    for h in range(MEM_HEADS):
        seg = slice(h * HEAD_DIM, (h + 1) * HEAD_DIM)
        y = qacc[:, seg]
        q = (y * rms(y) * gq_ref[:, seg]).astype(BF16)
        s = _qk(q, kv_ref[:, seg])
        p = jnp.exp2(s - jnp.max(s, axis=-1, keepdims=True))
        l = jnp.sum(p, axis=-1, keepdims=True)
        vseg = slice(MEM_W + h * HEAD_DIM, MEM_W + (h + 1) * HEAD_DIM)
        heads.append((_dot(p.astype(BF16), kv_ref[:, vseg]) / l).astype(BF16))
    x2 = x + _dot(jnp.concatenate(heads, axis=-1), wo_ref[...])
    o_ref[...] = x2
    hf_ref[...] = (x2 * rms(x2) * gf_ref[...]).astype(hf_ref.dtype)


def mem_attention(x, kv, lw, B, S):
    D = x.shape[1]
    Mt = kv.shape[0] // B
    tq = _pick(S, (256, 128))
    kv3 = kv.reshape(B, Mt, 2 * MEM_W)
    x3 = x.reshape(B, S, D)

    def row(n):
        return pl.BlockSpec((1, n), lambda b, i: (0, 0))

    out, hf = pl.pallas_call(
        _mem_attn_kernel,
        grid=(B, S // tq),
        in_specs=[pl.BlockSpec((None, tq, D), lambda b, i: (b, i, 0)),
                  row(D),
                  pl.BlockSpec((D, MEM_W), lambda b, i: (0, 0)),
                  row(MEM_W),
                  pl.BlockSpec((None, Mt, 2 * MEM_W), lambda b, i: (b, 0, 0)),
                  pl.BlockSpec((MEM_W, D), lambda b, i: (0, 0)),
                  row(D)],
        out_specs=[pl.BlockSpec((None, tq, D), lambda b, i: (b, i, 0)),
                   pl.BlockSpec((None, tq, D), lambda b, i: (b, i, 0))],
        out_shape=[jax.ShapeDtypeStruct((B, S, D), F32), jax.ShapeDtypeStruct((B, S, D), BF16)],
        compiler_params=_params("parallel", "arbitrary"),
        name="mem_attention",
    )(x3, lw["ln_mem_g"].reshape(1, D).astype(F32), lw["mem_wq"], lw["mem_q_gain"].reshape(1, MEM_W),
      kv3, lw["mem_wo"], lw["ln_ffn_g"].reshape(1, D).astype(F32))
    return out.reshape(B * S, D), hf.reshape(B * S, D)


def alibi_slopes(n):
    return 2.0 ** (-8.0 * jnp.arange(1, n + 1, dtype=F32) / n)


MIX_NAMES = ("qa", "kva", "kpe", "dq", "dk", "dv", "wq", "wk", "wv", "fq", "fk", "fv")


def _mix_layout():
    src = {n: (MIX_OFFSETS[i], MIX_OFFSETS[i + 1]) for i, n in enumerate(MIX_NAMES)}
    head = -(-(src["kpe"][0] + LANES) // MIX_TILE) * MIX_TILE
    shift = head - src["dq"][0]
    cols = {n: src[n][0] + (0 if n in ("qa", "kva", "kpe") else shift) for n in MIX_NAMES}
    total = -(-(MIX_COLS + shift) // MIX_TILE) * MIX_TILE
    return src, cols, head, shift, total


def _pack_kernel(*refs, n_head, r, tn):
    if n_head:
        head_ref, main_ref, extra_ref, o_ref = refs
    else:
        main_ref, extra_ref, o_ref = refs
    j = pl.program_id(1)

    @pl.when(j >= n_head)
    def _():
        x = jnp.concatenate([main_ref[...], extra_ref[...]], axis=1)
        o_ref[...] = x[:, tn - r:2 * tn - r].astype(o_ref.dtype)

    if n_head:
        @pl.when(j < n_head)
        def _():
            o_ref[...] = head_ref[...]


def pack_shifted(src, head, n_out, shift):
    K, C = src.shape
    tn = MIX_TILE
    tr = _pick(K, (1024, 512, 256))
    q, r = divmod(shift, tn)
    ew = next(e for e in (128, 256, 512) if e >= tn - r)
    n_head = 0 if head is None else head.shape[1] // tn
    kern = functools.partial(_pack_kernel, n_head=n_head, r=r, tn=tn)
    in_specs = [pl.BlockSpec((tr, tn), lambda i, j: (i, jnp.maximum(j - q - 1, 0))),
                pl.BlockSpec((tr, ew), lambda i, j: (i, jnp.maximum(j - q, 0) * (tn // ew)))]
    args = [src, src]
    if n_head:
        in_specs.insert(0, pl.BlockSpec((tr, tn), lambda i, j: (i, jnp.minimum(j, n_head - 1))))
        args.insert(0, head)
    return pl.pallas_call(
        kern,
        grid=(K // tr, n_out // tn),
        in_specs=in_specs,
        out_specs=pl.BlockSpec((tr, tn), lambda i, j: (i, j)),
        out_shape=jax.ShapeDtypeStruct((K, n_out), BF16),
        compiler_params=_params("parallel", "arbitrary"),
        name="pack_shifted",
    )(*args)


def pack_layer(p, l):
    src, cols, head_w, shift, total = _mix_layout()
    w_in = p["w_in"][l]
    half = MLA_ROPE // 2
    qscale = HEAD_DIM ** -0.5 * LOG2E

    def tile(g, n):
        return jnp.tile(g.astype(F32), n)

    norm_gain = {
        "dq": tile(p["dil_qk_g"][l, 0], DIL_HEADS) * qscale, "dk": tile(p["dil_qk_g"][l, 1], DIL_HEADS),
        "fq": tile(p["diff_qk_g"][l, 0], 2 * DIFF_HEADS) * (DIFF_HD ** -0.5 * LOG2E),
        "fk": tile(p["diff_qk_g"][l, 1], 2 * DIFF_HEADS),
        "wq": tile(p["win_qk_g"][l, 0], WIN_Q_HEADS) * qscale, "wk": tile(p["win_qk_g"][l, 1], WIN_KV_HEADS),
    }
    gain = jnp.ones((total,), F32)
    flag = jnp.zeros((total,), F32)
    for n, g in norm_gain.items():
        gain = gain.at[cols[n]:cols[n] + g.shape[0]].set(g)
        flag = flag.at[cols[n]:cols[n] + g.shape[0]].set(1.0)

    kpe = w_in[:, src["kpe"][0]:src["kpe"][1]]
    x1, x2 = kpe[:, :half], kpe[:, half:]
    head = jnp.concatenate([w_in[:, :src["kpe"][0]], x1, x2, x2, x1,
                            jnp.zeros((D_MODEL, head_w - src["kpe"][0] - LANES), F32)], axis=1).astype(BF16)
    w_mix = pack_shifted(w_in, head, total, shift)
    w_gate = pack_shifted(w_in, None, N_BRANCH * D_MODEL, -MIX_COLS)

    wq3 = p["mla_wq_up"][l].reshape(MLA_Q_LORA, MLA_HEADS, MLA_QK)
    nope, x1, x2 = wq3[:, :, :MLA_NOPE], wq3[:, :, MLA_NOPE:MLA_NOPE + half], wq3[:, :, MLA_NOPE + half:]
    z = jnp.zeros((MLA_Q_LORA, MLA_HEADS, LANES - MLA_ROPE), F32)
    wq_full = jnp.concatenate([nope, x1, x2, z], axis=-1).reshape(MLA_Q_LORA, MLA_HEADS * MLA_QK_PAD)
    wq_sw = jnp.concatenate([x2, x1, z], axis=-1).reshape(MLA_Q_LORA, MLA_HEADS * LANES)
    wkv3 = p["mla_wkv_up"][l].reshape(MLA_KV_LORA, MLA_HEADS, MLA_NOPE + MLA_V)
    zg = jnp.zeros((LANES - MLA_ROPE,), F32)
    qk_g = p["mla_qk_g"][l].astype(F32)

    return {
        "ln_mix_g": p["ln_mix_g"][l],
        "w_mix": w_mix, "w_gate": w_gate, "mix_gain": gain, "mix_flag": flag,
        "mla_qa_g": p["mla_qa_g"][l].reshape(1, -1).astype(F32),
        "mla_kva_g": p["mla_kva_g"][l].reshape(1, -1).astype(F32),
        "mla_wq": wq_full.astype(BF16), "mla_wq_sw": wq_sw.astype(BF16),
        "mla_wk": wkv3[:, :, :MLA_NOPE].reshape(MLA_KV_LORA, -1).astype(BF16),
        "mla_wv": wkv3[:, :, MLA_NOPE:].reshape(MLA_KV_LORA, -1).astype(BF16),
        "mla_gq": (jnp.concatenate([qk_g[0], zg]) * (MLA_QK ** -0.5 * LOG2E)).reshape(1, -1),
        "mla_gk": jnp.concatenate([qk_g[1], zg]).reshape(1, -1),
        "win_sink": p["win_sink"][l], "diff_lambda": p["diff_lambda"][l], "diff_subln_g": p["diff_subln_g"][l],
        "w_branch": p["w_branch"][l].astype(BF16), "w_out": p["w_out"][l].astype(BF16),
        "ln_mem_g": p["ln_mem_g"][l], "mem_ln_g": p["mem_ln_g"][l],
        "mem_wq": p["mem_wq"][l].astype(BF16), "mem_wkv": p["mem_wkv"][l].astype(BF16),
        "mem_q_gain": tile(p["mem_qk_g"][l, 0], MEM_HEADS) * (HEAD_DIM ** -0.5 * LOG2E),
        "mem_kv_gain": jnp.concatenate([tile(p["mem_qk_g"][l, 1], MEM_HEADS), jnp.ones((MEM_W,), F32)]),
        "mem_kv_flag": jnp.concatenate([jnp.ones((MEM_W,), F32), jnp.zeros((MEM_W,), F32)]),
        "mem_wo": p["mem_wo"][l].astype(BF16),
        "ln_ffn_g": p["ln_ffn_g"][l],
        "ffn_w_in": p["ffn_w_in"][l].astype(BF16), "ffn_w_out": p["ffn_w_out"][l].astype(BF16),
    }


def rotary_tables(S):
    half = MLA_ROPE // 2
    inv_freq = ROPE_THETA ** (-jnp.arange(half, dtype=F32) / half)
    ang = jnp.arange(S, dtype=F32)[:, None] * inv_freq[None, :]
    cos, sin = jnp.cos(ang), jnp.sin(ang)
    z = jnp.zeros((S, LANES - MLA_ROPE), F32)
    return jnp.concatenate([cos, cos, z], axis=1), jnp.concatenate([-sin, sin, z], axis=1)


def _trunk(x, mem, layers, tables):
    B, S, D = x.shape
    M = B * S
    _, cols, _, _, _ = _mix_layout()
    ct, st = rotary_tables(S)
    xf = x.reshape(M, D)
    memf = mem.reshape(-1, D)
    dil_tq = _pick(S, (256,))
    for l, lw in enumerate(layers):
        h = rmsnorm(xf, lw["ln_mix_g"])
        mix = mm_segnorm(h, lw["w_mix"], lw["mix_gain"], lw["mix_flag"], tn_prefs=(MIX_TILE,))
        gates = mm_sigmoid(h, lw["w_gate"])

        q, k, v = mla_prep(mix, cols, S, ct, st, lw)
        o_mla = dense_attention(q, k, v, B, S, MLA_HEADS, MLA_QK_PAD, MLA_V)
        o_dil = band_attention(mix, cols["dq"], mix, cols["dk"], mix, cols["dv"], tables["dil_bias"],
                               jnp.full((DIL_HEADS,), NEG_INF, F32), B, S, DIL_HEADS, 1, DIL_HEADS // 2,
                               dil_tq, dil_tq, DIL_RADIUS)
        o_win = band_attention(mix, cols["wq"], mix, cols["wk"], mix, cols["wv"], tables["win_bias"],
                               lw["win_sink"], B, S, WIN_Q_HEADS, WIN_Q_HEADS // WIN_KV_HEADS, WIN_Q_HEADS,
                               2 * WIN_RADIUS, WIN_RADIUS, WIN_RADIUS)
        o_diff = diff_attention(mix, cols, tables["slopes_diff"], lw["diff_lambda"], lw["diff_subln_g"], B, S, l)

        merged = branch_merge((o_mla, o_dil, o_win, o_diff), lw["w_branch"], gates)
        xf = mm_residual(merged, lw["w_out"], xf)

        kvm = mm_segnorm(rmsnorm(memf, lw["mem_ln_g"]), lw["mem_wkv"], lw["mem_kv_gain"], lw["mem_kv_flag"])
        xf, hf = mem_attention(xf, kvm, lw, B, S)

        hid = mm_swiglu(hf, lw["ffn_w_in"], FFN_HIDDEN)
        xf = mm_residual(hid, lw["ffn_w_out"], xf, tm_prefs=(512, 256, 128), tn_prefs=(256, 128))
    return xf.reshape(B, S, D)


def kernel(x_prompt, x_sample, mem_prompt, mem_sample, ln_mix_g, w_in, mla_qa_g, mla_kva_g, mla_wq_up,
           mla_wkv_up, mla_qk_g, dil_qk_g, win_qk_g, win_sink, diff_qk_g, diff_lambda, diff_subln_g,
           w_branch, w_out, ln_mem_g, mem_ln_g, mem_wq, mem_wkv, mem_qk_g, mem_wo, ln_ffn_g, ffn_w_in,
           ffn_w_out):
    p = dict(ln_mix_g=ln_mix_g, w_in=w_in, mla_qa_g=mla_qa_g, mla_kva_g=mla_kva_g, mla_wq_up=mla_wq_up,
             mla_wkv_up=mla_wkv_up, mla_qk_g=mla_qk_g, dil_qk_g=dil_qk_g, win_qk_g=win_qk_g,
             win_sink=win_sink, diff_qk_g=diff_qk_g, diff_lambda=diff_lambda, diff_subln_g=diff_subln_g,
             w_branch=w_branch, w_out=w_out, ln_mem_g=ln_mem_g, mem_ln_g=mem_ln_g, mem_wq=mem_wq,
             mem_wkv=mem_wkv, mem_qk_g=mem_qk_g, mem_wo=mem_wo, ln_ffn_g=ln_ffn_g, ffn_w_in=ffn_w_in,
             ffn_w_out=ffn_w_out)
    layers = [pack_layer(p, l) for l in range(DEPTH)]
    tables = {
        "dil_bias": band_bias(alibi_slopes(DIL_HEADS), dil_multiplicity, 256, 256, DIL_RADIUS),
        "win_bias": band_bias(alibi_slopes(WIN_Q_HEADS), win_multiplicity, 2 * WIN_RADIUS, WIN_RADIUS,
                              WIN_RADIUS),
        "slopes_diff": alibi_slopes(DIFF_HEADS),
    }
    y_prompt = _trunk(x_prompt, mem_prompt, layers, tables)
    y_sample = _trunk(x_sample, mem_sample, layers, tables)
    return (y_prompt, y_sample)
```

```python
import functools
import math

import jax
import jax.numpy as jnp
import numpy as np
from jax import lax
from jax.experimental import pallas as pl
from jax.experimental.pallas import tpu as pltpu

F32 = jnp.float32
BF16 = jnp.bfloat16

D_MODEL = 4096
DEPTH = 2
EPS = 1e-6
NEG_INF = -1e30
N_BRANCH = 4
BRANCH_W = D_MODEL // N_BRANCH
HEAD_DIM = 128

MLA_NOPE = 128
MLA_ROPE = 64
MLA_V = 128
MLA_HEADS = BRANCH_W // MLA_V
MLA_Q_LORA = D_MODEL // 4
MLA_KV_LORA = D_MODEL // 8
MLA_QK = MLA_NOPE + MLA_ROPE
MLA_QK_PAD = 256
ROPE_THETA = 10000.0

DIL_HEADS = BRANCH_W // HEAD_DIM
DIL_PATTERNS = ((128, 1), (512, 4), (2048, 16))
DIL_RADIUS = max(w // 2 for w, _ in DIL_PATTERNS)

WIN_Q_HEADS = BRANCH_W // HEAD_DIM
WIN_KV_HEADS = WIN_Q_HEADS // 4
WIN_RADIUS = 128

DIFF_HD = 128
DIFF_HEADS = BRANCH_W // (2 * DIFF_HD)

MEM_HEADS = 4
MEM_W = MEM_HEADS * HEAD_DIM

FFN_HIDDEN = -(-8 * D_MODEL // (3 * 256)) * 256

MIX_SPLITS = (MLA_Q_LORA, MLA_KV_LORA, MLA_ROPE,
              BRANCH_W, BRANCH_W, BRANCH_W,
              BRANCH_W, WIN_KV_HEADS * HEAD_DIM, WIN_KV_HEADS * HEAD_DIM,
              BRANCH_W, BRANCH_W, BRANCH_W)
MIX_COLS = sum(MIX_SPLITS)
MIX_OFFSETS = tuple(int(o) for o in np.cumsum((0,) + MIX_SPLITS))

LOG2E = math.log2(math.e)
LANES = 128
MIX_TILE = 512
VMEM_LIMIT_BYTES = 56 * 1024 * 1024


def _pick(n, prefs):
    for p in prefs:
        if n % p == 0:
            return p
    raise ValueError(f"no tile in {prefs} divides {n}")


def _params(*sem):
    return pltpu.CompilerParams(dimension_semantics=sem, vmem_limit_bytes=VMEM_LIMIT_BYTES)


def _rmsnorm_kernel(x_ref, g_ref, o_ref):
    x = x_ref[...].astype(F32)
    ms = jnp.mean(x * x, axis=-1, keepdims=True)
    o_ref[...] = (x * lax.rsqrt(ms + EPS) * g_ref[...]).astype(o_ref.dtype)


def rmsnorm(x, g):
    M, D = x.shape
    tm = _pick(M, (512, 256, 128, 64, 8))
    return pl.pallas_call(
        _rmsnorm_kernel,
        grid=(M // tm,),
        in_specs=[pl.BlockSpec((tm, D), lambda i: (i, 0)),
                  pl.BlockSpec((1, D), lambda i: (0, 0))],
        out_specs=pl.BlockSpec((tm, D), lambda i: (i, 0)),
        out_shape=jax.ShapeDtypeStruct((M, D), BF16),
        compiler_params=_params("parallel"),
        name="rmsnorm",
    )(x, g.reshape(1, D).astype(F32))


def _dot(a, b):
    return jnp.dot(a, b, preferred_element_type=F32)


MXU_COLS = 256


def _sigmoid(x):
    return 0.5 * jnp.tanh(0.5 * x) + 0.5


def _col_slabs(n):
    w = MXU_COLS if n % MXU_COLS == 0 else n
    return [slice(c, c + w) for c in range(0, n, w)]


def _mm_sigmoid_kernel(a_ref, b_ref, o_ref):
    for cs in _col_slabs(o_ref.shape[1]):
        o_ref[:, cs] = _sigmoid(_dot(a_ref[...], b_ref[:, cs])).astype(o_ref.dtype)


def _mm_segnorm_kernel(a_ref, b_ref, g_ref, f_ref, o_ref):
    for cs in _col_slabs(o_ref.shape[1]):
        acc = _dot(a_ref[...], b_ref[:, cs])
        for c in range(0, acc.shape[1], LANES):
            seg = slice(cs.start + c, cs.start + c + LANES)
            y = acc[:, c:c + LANES]
            r = lax.rsqrt(jnp.mean(y * y, axis=-1, keepdims=True) + EPS)
            mult = jnp.where(f_ref[:, seg] > 0.0, r, 1.0) * g_ref[:, seg]
            o_ref[:, seg] = (y * mult).astype(o_ref.dtype)


def _mm_residual_kernel(a_ref, b_ref, r_ref, o_ref):
    for cs in _col_slabs(o_ref.shape[1]):
        o_ref[:, cs] = r_ref[:, cs] + _dot(a_ref[...], b_ref[:, cs])


def _mm_swiglu_kernel(a_ref, bg_ref, bu_ref, o_ref):
    half = a_ref.shape[0] // 2
    for rows in (slice(0, half), slice(half, 2 * half)):
        a = a_ref[rows, :]
        g = _dot(a, bg_ref[...])
        u = _dot(a, bu_ref[...])
        o_ref[rows, :] = (g * _sigmoid(g) * u).astype(o_ref.dtype)


def mm_sigmoid(a, w, tm_prefs=(1024, 512, 256, 128), tn_prefs=(512, 256, 128)):
    M, K = a.shape
    N = w.shape[1]
    tm, tn = _pick(M, tm_prefs), _pick(N, tn_prefs)
    return pl.pallas_call(
        _mm_sigmoid_kernel,
        grid=(M // tm, N // tn),
        in_specs=[pl.BlockSpec((tm, K), lambda i, j: (i, 0)),
                  pl.BlockSpec((K, tn), lambda i, j: (0, j))],
        out_specs=pl.BlockSpec((tm, tn), lambda i, j: (i, j)),
        out_shape=jax.ShapeDtypeStruct((M, N), BF16),
        compiler_params=_params("parallel", "arbitrary"),
        name="mm_sigmoid",
    )(a, w)


def mm_segnorm(a, w, gain, flag, tm_prefs=(1024, 512, 256, 128), tn_prefs=(512, 256, 128)):
    M, K = a.shape
    N = gain.shape[0]
    tm, tn = _pick(M, tm_prefs), _pick(N, tn_prefs)
    return pl.pallas_call(
        _mm_segnorm_kernel,
        grid=(M // tm, N // tn),
        in_specs=[pl.BlockSpec((tm, K), lambda i, j: (i, 0)),
                  pl.BlockSpec((K, tn), lambda i, j: (0, j)),
                  pl.BlockSpec((1, tn), lambda i, j: (0, j)),
                  pl.BlockSpec((1, tn), lambda i, j: (0, j))],
        out_specs=pl.BlockSpec((tm, tn), lambda i, j: (i, j)),
        out_shape=jax.ShapeDtypeStruct((M, N), BF16),
        compiler_params=_params("parallel", "arbitrary"),
        name="mm_segnorm",
    )(a, w, gain.reshape(1, N).astype(F32), flag.reshape(1, N).astype(F32))


def mm_residual(a, w, res, tm_prefs=(1024, 512, 256, 128), tn_prefs=(512, 256, 128)):
    M, K = a.shape
    N = w.shape[1]
    tm, tn = _pick(M, tm_prefs), _pick(N, tn_prefs)
    return pl.pallas_call(
        _mm_residual_kernel,
        grid=(M // tm, N // tn),
        in_specs=[pl.BlockSpec((tm, K), lambda i, j: (i, 0)),
                  pl.BlockSpec((K, tn), lambda i, j: (0, j)),
                  pl.BlockSpec((tm, tn), lambda i, j: (i, j))],
        out_specs=pl.BlockSpec((tm, tn), lambda i, j: (i, j)),
        out_shape=jax.ShapeDtypeStruct((M, N), F32),
        compiler_params=_params("parallel", "arbitrary"),
        name="mm_residual",
    )(a, w, res)


def mm_swiglu(a, w, hidden):
    M, K = a.shape
    tm = _pick(M, (1024, 512, 256, 128))
    tn = _pick(hidden, (256, 128))
    nj = hidden // tn
    return pl.pallas_call(
        _mm_swiglu_kernel,
        grid=(M // tm, nj),
        in_specs=[pl.BlockSpec((tm, K), lambda i, j: (i, 0)),
                  pl.BlockSpec((K, tn), lambda i, j: (0, j)),
                  pl.BlockSpec((K, tn), lambda i, j: (0, j + nj))],
        out_specs=pl.BlockSpec((tm, tn), lambda i, j: (i, j)),
        out_shape=jax.ShapeDtypeStruct((M, hidden), BF16),
        compiler_params=_params("parallel", "arbitrary"),
        name="mm_swiglu",
    )(a, w, w)


def _branch_kernel(oa_ref, ob_ref, oc_ref, od_ref, w_ref, ga_ref, gb_ref, gc_ref, gd_ref, o_ref):
    acc = ga_ref[...].astype(F32) * _dot(oa_ref[...], w_ref[0])
    acc += gb_ref[...].astype(F32) * _dot(ob_ref[...], w_ref[1])
    acc += gc_ref[...].astype(F32) * _dot(oc_ref[...], w_ref[2])
    acc += gd_ref[...].astype(F32) * _dot(od_ref[...], w_ref[3])
    o_ref[...] = acc.astype(o_ref.dtype)


def branch_merge(outs, w_branch, gates):
    M, W = outs[0].shape
    D = w_branch.shape[2]
    tm = _pick(M, (1024, 512, 256, 128))
    tn = _pick(D, (512, 256, 128))
    nj = D // tn
    o_spec = pl.BlockSpec((tm, W), lambda i, j: (i, 0))
    g_specs = [pl.BlockSpec((tm, tn), functools.partial(lambda i, j, b: (i, b * nj + j), b=b))
               for b in range(N_BRANCH)]
    return pl.pallas_call(
        _branch_kernel,
        grid=(M // tm, nj),
        in_specs=[o_spec] * N_BRANCH + [pl.BlockSpec((N_BRANCH, W, tn), lambda i, j: (0, 0, j))] + g_specs,
        out_specs=pl.BlockSpec((tm, tn), lambda i, j: (i, j)),
        out_shape=jax.ShapeDtypeStruct((M, D), BF16),
        compiler_params=_params("parallel", "arbitrary"),
        name="branch_merge",
    )(*outs, w_branch, gates, gates, gates, gates)


def _mla_prep_kernel(qa_ref, kva_ref, kpe_ref, ct_ref, st_ref, qag_ref, kvag_ref,
                     wq_ref, wqs_ref, wk_ref, wv_ref, gq_ref, gk_ref,
                     q_ref, k_ref, v_ref):
    def norm(x, g):
        return (x * lax.rsqrt(jnp.mean(x * x, axis=-1, keepdims=True) + EPS) * g).astype(BF16)

    ct = ct_ref[...]
    st = st_ref[...]
    qn = norm(qa_ref[...].astype(F32), qag_ref[...])
    kvn = norm(kva_ref[...].astype(F32), kvag_ref[...])
    qfull = _dot(qn, wq_ref[...])
    qsw = _dot(qn, wqs_ref[...])
    knope = _dot(kvn, wk_ref[...])
    vals = _dot(kvn, wv_ref[...])
    ones = jnp.ones((vals.shape[0], MLA_V), v_ref.dtype)
    for h in range(MLA_HEADS):
        v_ref[:, 2 * h * MLA_V:(2 * h + 1) * MLA_V] = vals[:, h * MLA_V:(h + 1) * MLA_V].astype(v_ref.dtype)
        v_ref[:, (2 * h + 1) * MLA_V:(2 * h + 2) * MLA_V] = ones

    kpe = kpe_ref[...].astype(F32)
    kpe_rot = kpe * ct + pltpu.roll(kpe, 2 * (MLA_ROPE // 2), 1) * st
    kpe_ss = jnp.sum(kpe_rot * kpe_rot, axis=-1, keepdims=True)
    gq_n, gq_r = gq_ref[:, :LANES], gq_ref[:, LANES:]
    gk_n, gk_r = gk_ref[:, :LANES], gk_ref[:, LANES:]
    for h in range(MLA_HEADS):
        lo = h * MLA_QK_PAD
        q_n = qfull[:, lo:lo + LANES]
        q_r = qfull[:, lo + LANES:lo + 2 * LANES] * ct + qsw[:, h * LANES:(h + 1) * LANES] * st
        ss = jnp.sum(q_n * q_n, axis=-1, keepdims=True) + jnp.sum(q_r * q_r, axis=-1, keepdims=True)
        r = lax.rsqrt(ss * (1.0 / MLA_QK) + EPS)
        q_ref[:, lo:lo + LANES] = (q_n * r * gq_n).astype(q_ref.dtype)
        q_ref[:, lo + LANES:lo + 2 * LANES] = (q_r * r * gq_r).astype(q_ref.dtype)
        k_n = knope[:, h * LANES:(h + 1) * LANES]
        ss = jnp.sum(k_n * k_n, axis=-1, keepdims=True) + kpe_ss
        r = lax.rsqrt(ss * (1.0 / MLA_QK) + EPS)
        k_ref[:, lo:lo + LANES] = (k_n * r * gk_n).astype(k_ref.dtype)
        k_ref[:, lo + LANES:lo + 2 * LANES] = (kpe_rot * r * gk_r).astype(k_ref.dtype)


def mla_prep(mix, cols, S, ct, st, lw):
    M = mix.shape[0]
    tm = _pick(S, (256, 128))
    ns = S // tm
    HQ = MLA_HEADS * MLA_QK_PAD
    HV = MLA_HEADS * 2 * MLA_V

    def col(width, off):
        assert off % width == 0
        return pl.BlockSpec((tm, width), lambda i: (i, off // width))

    def whole(a):
        return pl.BlockSpec(a.shape, lambda i: (0,) * a.ndim)

    consts = (lw["mla_qa_g"], lw["mla_kva_g"], lw["mla_wq"], lw["mla_wq_sw"], lw["mla_wk"], lw["mla_wv"],
              lw["mla_gq"], lw["mla_gk"])
    return pl.pallas_call(
        _mla_prep_kernel,
        grid=(M // tm,),
        in_specs=[col(MLA_Q_LORA, cols["qa"]), col(MLA_KV_LORA, cols["kva"]), col(LANES, cols["kpe"]),
                  pl.BlockSpec((tm, LANES), lambda i: (i % ns, 0)),
                  pl.BlockSpec((tm, LANES), lambda i: (i % ns, 0))] + [whole(c) for c in consts],
        out_specs=[pl.BlockSpec((tm, HQ), lambda i: (i, 0)),
                   pl.BlockSpec((tm, HQ), lambda i: (i, 0)),
                   pl.BlockSpec((tm, HV), lambda i: (i, 0))],
        out_shape=[jax.ShapeDtypeStruct((M, HQ), BF16),
                   jax.ShapeDtypeStruct((M, HQ), BF16),
                   jax.ShapeDtypeStruct((M, HV), BF16)],
        compiler_params=_params("parallel"),
        name="mla_prep",
    )(mix, mix, mix, ct, st, *consts)


def _qk(q, k):
    return lax.dot_general(q, k, (((1,), (1,)), ((), ())), preferred_element_type=F32)


def _dense_attn_kernel(q_ref, k_ref, v_ref, o_ref, *, n_sub, chunk):
    dv = o_ref.shape[1]
    ts = q_ref.shape[0] // n_sub
    seq = k_ref.shape[0]
    for t in range(n_sub):
        rows = slice(t * ts, (t + 1) * ts)
        q = q_ref[rows, :]
        pieces = [_qk(q, k_ref[c:c + chunk, :]) for c in range(0, seq, chunk)]
        mt = _lane_fold(pieces[0], jnp.maximum)
        for piece in pieces[1:]:
            mt = jnp.maximum(mt, _lane_fold(piece, jnp.maximum))
        m = jnp.max(mt, axis=-1, keepdims=True)
        o = jnp.zeros((ts, 2 * dv), F32)
        for c, piece in zip(range(0, seq, chunk), pieces):
            o = o + _dot(jnp.exp2(piece - m).astype(BF16), v_ref[c:c + chunk, :])
        o_ref[rows, :] = (o[:, :dv] / o[:, dv:]).astype(o_ref.dtype)


def dense_attention(q, k, v, B, S, H, dk, dv, n_sub=4):
    tq = _pick(S, (1024, 512, 256))
    chunk = _pick(S, (512, 256))
    q3, k3, v3 = (t.reshape(B, S, t.shape[1]) for t in (q, k, v))
    out = pl.pallas_call(
        functools.partial(_dense_attn_kernel, n_sub=n_sub, chunk=chunk),
        grid=(B, H, S // tq),
        in_specs=[pl.BlockSpec((None, tq, dk), lambda b, h, i: (b, i, h)),
                  pl.BlockSpec((None, S, dk), lambda b, h, i: (b, 0, h)),
                  pl.BlockSpec((None, S, 2 * dv), lambda b, h, i: (b, 0, h))],
        out_specs=pl.BlockSpec((None, tq, dv), lambda b, h, i: (b, i, h)),
        out_shape=jax.ShapeDtypeStruct((B, S, H * dv), BF16),
        compiler_params=_params("parallel", "parallel", "arbitrary"),
        name="dense_attention",
    )(q3, k3, v3)
    return out.reshape(B * S, H * dv)


def _lane_fold(x, op):
    out = x[:, :LANES]
    for j in range(1, x.shape[1] // LANES):
        out = op(out, x[:, j * LANES:(j + 1) * LANES])
    return out


def _band_kernel(sink_ref, q_ref, k_ref, v_ref, bias_ref, o_ref, *, tq, tk, radius, seq, n_chunks, hpb, rep):
    hb = pl.program_id(1)
    q0 = pl.program_id(2) * tq
    starts, tiles = [], []
    for c in range(n_chunks):
        start = q0 - radius + c * tk
        inside = jnp.logical_and(start >= 0, start + tk <= seq)
        starts.append(pl.multiple_of(jnp.clip(start, 0, seq - tk), tk))
        tiles.append(jnp.where(inside, c, n_chunks))
    for hh in range(hpb):
        qs = slice(hh * HEAD_DIM, (hh + 1) * HEAD_DIM)
        ks = slice((hh // rep) * HEAD_DIM, (hh // rep + 1) * HEAD_DIM)
        q = q_ref[:, qs]
        scores = [_qk(q, k_ref[pl.ds(sc, tk), ks]) + bias_ref[hh, t] for sc, t in zip(starts, tiles)]
        mt = _lane_fold(scores[0], jnp.maximum)
        for s in scores[1:]:
            mt = jnp.maximum(mt, _lane_fold(s, jnp.maximum))
        sink = sink_ref[hb * hpb + hh] * LOG2E
        m = jnp.maximum(jnp.max(mt, axis=-1, keepdims=True), sink)
        lt = jnp.zeros((tq, LANES), F32)
        acc = jnp.zeros((tq, HEAD_DIM), F32)
        for s, sc in zip(scores, starts):
            p = jnp.exp2(s - m)
            lt = lt + _lane_fold(p, jnp.add)
            acc = acc + _dot(p.astype(BF16), v_ref[pl.ds(sc, tk), ks])
        l = jnp.sum(lt, axis=-1, keepdims=True) + jnp.exp2(sink - m)
        o_ref[:, qs] = (acc / l).astype(o_ref.dtype)


def band_bias(slopes, mult_fn, tq, tk, radius):
    n_chunks = (tq + 2 * radius) // tk
    a = jnp.arange(tq)[None, :, None]
    c = jnp.arange(tk)[None, None, :]
    delta = (jnp.arange(n_chunks)[:, None, None] * tk - radius) + c - a
    mult = mult_fn(delta)
    dist = jnp.abs(delta).astype(F32)
    logm = jnp.log2(jnp.maximum(mult, 1).astype(F32))
    bias = logm[None] - (slopes * LOG2E)[:, None, None, None] * dist[None]
    bias = jnp.where(mult[None] > 0, bias, NEG_INF)
    dead = jnp.full((slopes.shape[0], 1, tq, tk), NEG_INF, F32)
    return jnp.concatenate([bias, dead], axis=1)


def dil_multiplicity(delta):
    m = jnp.zeros(delta.shape, jnp.int32)
    for window, dil in DIL_PATTERNS:
        m = m + ((delta % dil == 0) & (jnp.abs(delta) <= window // 2)).astype(jnp.int32)
    return m


def win_multiplicity(delta):
    return (jnp.abs(delta) <= WIN_RADIUS).astype(jnp.int32)


def band_attention(qarr, q_off, karr, k_off, varr, v_off, bias, sinks, B, S, H, rep, hpb, tq, tk, radius):
    n_chunks = bias.shape[1] - 1
    assert radius % tk == 0 and tq % tk == 0 and S % tq == 0 and S >= tk and hpb % rep == 0 and H % hpb == 0
    wq, wk = hpb * HEAD_DIM, (hpb // rep) * HEAD_DIM
    assert q_off % wq == 0 and k_off % wk == 0 and v_off % wk == 0
    qb, kb, vb = q_off // wq, k_off // wk, v_off // wk
    q3, k3, v3 = (t.reshape(B, S, t.shape[1]) for t in (qarr, karr, varr))
    kern = functools.partial(_band_kernel, tq=tq, tk=tk, radius=radius, seq=S, n_chunks=n_chunks, hpb=hpb, rep=rep)
    out = pl.pallas_call(
        kern,
        grid=(B, H // hpb, S // tq),
        in_specs=[pl.BlockSpec(memory_space=pltpu.SMEM),
                  pl.BlockSpec((None, tq, wq), lambda b, h, i: (b, i, qb + h)),
                  pl.BlockSpec((None, S, wk), lambda b, h, i: (b, 0, kb + h)),
                  pl.BlockSpec((None, S, wk), lambda b, h, i: (b, 0, vb + h)),
                  pl.BlockSpec((hpb, n_chunks + 1, tq, tk), lambda b, h, i: (h, 0, 0, 0))],
        out_specs=pl.BlockSpec((None, tq, wq), lambda b, h, i: (b, i, h)),
        out_shape=jax.ShapeDtypeStruct((B, S, H * HEAD_DIM), BF16),
        compiler_params=_params("parallel", "parallel", "arbitrary"),
        name="band_attention",
    )(sinks.astype(F32), q3, k3, v3, bias)
    return out.reshape(B * S, H * HEAD_DIM)


def _diff_kernel(slope_ref, lam_ref, q_ref, k_ref, v_ref, pos_ref, g_ref, o_ref, *, tq, lam_init, chunk, n_sub):
    h = pl.program_id(1)
    q0 = pl.program_id(2) * tq
    lp = lam_ref[...]
    lam = (jnp.exp(jnp.sum(lp[0:1] * lp[1:2], axis=-1, keepdims=True))
           - jnp.exp(jnp.sum(lp[2:3] * lp[3:4], axis=-1, keepdims=True)) + lam_init)
    slope = slope_ref[h]
    seq = k_ref.shape[0]
    kpos = pos_ref[...] * slope
    ts = tq // n_sub
    for t in range(n_sub):
        rows = slice(t * ts, (t + 1) * ts)
        qrow = (q0 + t * ts + lax.broadcasted_iota(jnp.int32, (ts, LANES), 0)).astype(F32) * slope
        qpos = jnp.concatenate([qrow] * (chunk // LANES), axis=1)
        q1, q2 = q_ref[rows, :DIFF_HD], q_ref[rows, DIFF_HD:]
        t1, t2, mt1, mt2 = [], [], None, None
        for c in range(0, seq, chunk):
            b = jnp.abs(kpos[:, c:c + chunk] - qpos)
            a1 = _qk(q1, k_ref[c:c + chunk, :DIFF_HD]) - b
            a2 = _qk(q2, k_ref[c:c + chunk, DIFF_HD:]) - b
            t1.append(a1)
            t2.append(a2)
            f1, f2 = _lane_fold(a1, jnp.maximum), _lane_fold(a2, jnp.maximum)
            mt1 = f1 if mt1 is None else jnp.maximum(mt1, f1)
            mt2 = f2 if mt2 is None else jnp.maximum(mt2, f2)
        m1 = jnp.max(mt1, axis=-1, keepdims=True)
        m2 = jnp.max(mt2, axis=-1, keepdims=True)
        lt1 = jnp.zeros((ts, LANES), F32)
        lt2 = jnp.zeros((ts, LANES), F32)
        o1 = jnp.zeros((ts, 2 * DIFF_HD), F32)
        o2 = jnp.zeros((ts, 2 * DIFF_HD), F32)
        for i, c in enumerate(range(0, seq, chunk)):
            p1, p2 = jnp.exp2(t1[i] - m1), jnp.exp2(t2[i] - m2)
            lt1 = lt1 + _lane_fold(p1, jnp.add)
            lt2 = lt2 + _lane_fold(p2, jnp.add)
            o1 = o1 + _dot(p1.astype(BF16), v_ref[c:c + chunk, :])
            o2 = o2 + _dot(p2.astype(BF16), v_ref[c:c + chunk, :])
        l1 = jnp.sum(lt1, axis=-1, keepdims=True)
        l2 = jnp.sum(lt2, axis=-1, keepdims=True)
        o = o1 * (1.0 / l1) - o2 * (lam / l2)
        r = lax.rsqrt(jnp.mean(o * o, axis=-1, keepdims=True) + EPS)
        o_ref[rows, :] = (o * r * g_ref[...] * (1.0 - lam_init)).astype(o_ref.dtype)


def diff_attention(mix, cols, slopes, lam_p, subln_g, B, S, layer):
    tq = _pick(S, (512, 256, 128))
    lam_init = 0.8 - 0.6 * math.exp(-0.3 * layer)
    W = 2 * DIFF_HD
    qb, kb, vb = (cols[n] // W for n in ("fq", "fk", "fv"))
    m3 = mix.reshape(B, S, mix.shape[1])
    pos = jnp.arange(S, dtype=F32).reshape(1, S)
    kern = functools.partial(_diff_kernel, tq=tq, lam_init=lam_init, chunk=_pick(S, (512, 256)),
                             n_sub=2 if S > 2048 else 4)
    out = pl.pallas_call(
        kern,
        grid=(B, DIFF_HEADS, S // tq),
        in_specs=[pl.BlockSpec(memory_space=pltpu.SMEM),
                  pl.BlockSpec((4, DIFF_HD), lambda b, h, i: (0, 0)),
                  pl.BlockSpec((None, tq, W), lambda b, h, i: (b, i, qb + h)),
                  pl.BlockSpec((None, S, W), lambda b, h, i: (b, 0, kb + h)),
                  pl.BlockSpec((None, S, W), lambda b, h, i: (b, 0, vb + h)),
                  pl.BlockSpec((1, S), lambda b, h, i: (0, 0)),
                  pl.BlockSpec((1, W), lambda b, h, i: (0, 0))],
        out_specs=pl.BlockSpec((None, tq, W), lambda b, h, i: (b, i, h)),
        out_shape=jax.ShapeDtypeStruct((B, S, DIFF_HEADS * W), BF16),
        compiler_params=_params("parallel", "parallel", "arbitrary"),
        name="diff_attention",
    )((slopes * LOG2E).astype(F32), lam_p.astype(F32), m3, m3, m3, pos, subln_g.reshape(1, W).astype(F32))
    return out.reshape(B * S, DIFF_HEADS * W)


def _mem_attn_kernel(x_ref, gm_ref, wq_ref, gq_ref, kv_ref, wo_ref, gf_ref, o_ref, hf_ref):
    def rms(t):
        return lax.rsqrt(jnp.mean(t * t, axis=-1, keepdims=True) + EPS)

    x = x_ref[...]
    hm = (x * rms(x) * gm_ref[...]).astype(BF16)
    qacc = _dot(hm, wq_ref[...])
    heads = []
    for h in range(MEM_HEADS):
        seg = slice(h * HEAD_DIM, (h + 1) * HEAD_DIM)
        y = qacc[:, seg]
        q = (y * rms(y) * gq_ref[:, seg]).astype(BF16)
        s = _qk(q, kv_ref[:, seg])
        p = jnp.exp2(s - jnp.max(s, axis=-1, keepdims=True))
        l = jnp.sum(p, axis=-1, keepdims=True)
        vseg = slice(MEM_W + h * HEAD_DIM, MEM_W + (h + 1) * HEAD_DIM)
        heads.append((_dot(p.astype(BF16), kv_ref[:, vseg]) / l).astype(BF16))
    x2 = x + _dot(jnp.concatenate(heads, axis=-1), wo_ref[...])
    o_ref[...] = x2
    hf_ref[...] = (x2 * rms(x2) * gf_ref[...]).astype(hf_ref.dtype)


def mem_attention(x, kv, lw, B, S):
    D = x.shape[1]
    Mt = kv.shape[0] // B
    tq = _pick(S, (256, 128))
    kv3 = kv.reshape(B, Mt, 2 * MEM_W)
    x3 = x.reshape(B, S, D)

    def row(n):
        return pl.BlockSpec((1, n), lambda b, i: (0, 0))

    out, hf = pl.pallas_call(
        _mem_attn_kernel,
        grid=(B, S // tq),
        in_specs=[pl.BlockSpec((None, tq, D), lambda b, i: (b, i, 0)),
                  row(D),
                  pl.BlockSpec((D, MEM_W), lambda b, i: (0, 0)),
                  row(MEM_W),
                  pl.BlockSpec((None, Mt, 2 * MEM_W), lambda b, i: (b, 0, 0)),
                  pl.BlockSpec((MEM_W, D), lambda b, i: (0, 0)),
                  row(D)],
        out_specs=[pl.BlockSpec((None, tq, D), lambda b, i: (b, i, 0)),
                   pl.BlockSpec((None, tq, D), lambda b, i: (b, i, 0))],
        out_shape=[jax.ShapeDtypeStruct((B, S, D), F32), jax.ShapeDtypeStruct((B, S, D), BF16)],
        compiler_params=_params("parallel", "arbitrary"),
        name="mem_attention",
    )(x3, lw["ln_mem_g"].reshape(1, D).astype(F32), lw["mem_wq"], lw["mem_q_gain"].reshape(1, MEM_W),
      kv3, lw["mem_wo"], lw["ln_ffn_g"].reshape(1, D).astype(F32))
    return out.reshape(B * S, D), hf.reshape(B * S, D)


def alibi_slopes(n):
    return 2.0 ** (-8.0 * jnp.arange(1, n + 1, dtype=F32) / n)


MIX_NAMES = ("qa", "kva", "kpe", "dq", "dk", "dv", "wq", "wk", "wv", "fq", "fk", "fv")


def _mix_layout():
    src = {n: (MIX_OFFSETS[i], MIX_OFFSETS[i + 1]) for i, n in enumerate(MIX_NAMES)}
    head = -(-(src["kpe"][0] + LANES) // MIX_TILE) * MIX_TILE
    shift = head - src["dq"][0]
    cols = {n: src[n][0] + (0 if n in ("qa", "kva", "kpe") else shift) for n in MIX_NAMES}
    total = -(-(MIX_COLS + shift) // MIX_TILE) * MIX_TILE
    return src, cols, head, shift, total


def _pack_kernel(*refs, n_head, r, tn):
    if n_head:
        head_ref, main_ref, extra_ref, o_ref = refs
    else:
        main_ref, extra_ref, o_ref = refs
    j = pl.program_id(1)

    @pl.when(j >= n_head)
    def _():
        x = jnp.concatenate([main_ref[...], extra_ref[...]], axis=1)
        o_ref[...] = x[:, tn - r:2 * tn - r].astype(o_ref.dtype)

    if n_head:
        @pl.when(j < n_head)
        def _():
            o_ref[...] = head_ref[...]


def _cast_kernel(x_ref, o_ref):
    o_ref[...] = x_ref[...].astype(o_ref.dtype)


def cast_layer(w, l):
    _, R, C = w.shape
    tc = _pick(C, (2048, 1024, 512, 256, 128))
    tr = _pick(R, tuple(t for t in (2048, 1024, 512, 256, 128, 64, 8) if t * tc <= 2 ** 21))
    return pl.pallas_call(
        _cast_kernel,
        grid=(R // tr, C // tc),
        in_specs=[pl.BlockSpec((None, tr, tc), lambda i, j: (l, i, j))],
        out_specs=pl.BlockSpec((tr, tc), lambda i, j: (i, j)),
        out_shape=jax.ShapeDtypeStruct((R, C), BF16),
        compiler_params=_params("parallel", "arbitrary"),
        name="cast_layer",
    )(w)


def pack_shifted(src, l, head, n_out, shift):
    _, K, C = src.shape
    tn = MIX_TILE
    tr = _pick(K, (1024, 512, 256))
    q, r = divmod(shift, tn)
    ew = next(e for e in (128, 256, 512) if e >= tn - r)
    n_head = 0 if head is None else head.shape[1] // tn
    kern = functools.partial(_pack_kernel, n_head=n_head, r=r, tn=tn)
    in_specs = [pl.BlockSpec((None, tr, tn), lambda i, j: (l, i, jnp.maximum(j - q - 1, 0))),
                pl.BlockSpec((None, tr, ew), lambda i, j: (l, i, jnp.maximum(j - q, 0) * (tn // ew)))]
    args = [src, src]
    if n_head:
        in_specs.insert(0, pl.BlockSpec((tr, tn), lambda i, j: (i, jnp.minimum(j, n_head - 1))))
        args.insert(0, head)
    return pl.pallas_call(
        kern,
        grid=(K // tr, n_out // tn),
        in_specs=in_specs,
        out_specs=pl.BlockSpec((tr, tn), lambda i, j: (i, j)),
        out_shape=jax.ShapeDtypeStruct((K, n_out), BF16),
        compiler_params=_params("parallel", "arbitrary"),
        name="pack_shifted",
    )(*args)


def pack_layer(p, l):
    src, cols, head_w, shift, total = _mix_layout()
    half = MLA_ROPE // 2
    qscale = HEAD_DIM ** -0.5 * LOG2E

    def tile(g, n):
        return jnp.tile(g.astype(F32), n)

    norm_gain = {
        "dq": tile(p["dil_qk_g"][l, 0], DIL_HEADS) * qscale, "dk": tile(p["dil_qk_g"][l, 1], DIL_HEADS),
        "fq": tile(p["diff_qk_g"][l, 0], 2 * DIFF_HEADS) * (DIFF_HD ** -0.5 * LOG2E),
        "fk": tile(p["diff_qk_g"][l, 1], 2 * DIFF_HEADS),
        "wq": tile(p["win_qk_g"][l, 0], WIN_Q_HEADS) * qscale, "wk": tile(p["win_qk_g"][l, 1], WIN_KV_HEADS),
    }
    gain = jnp.ones((total,), F32)
    flag = jnp.zeros((total,), F32)
    for n, g in norm_gain.items():
        gain = gain.at[cols[n]:cols[n] + g.shape[0]].set(g)
        flag = flag.at[cols[n]:cols[n] + g.shape[0]].set(1.0)

    lead = p["w_in"][l, :, :src["kpe"][1]].astype(BF16)
    x1, x2 = lead[:, src["kpe"][0]:src["kpe"][0] + half], lead[:, src["kpe"][0] + half:]
    head = jnp.concatenate([lead[:, :src["kpe"][0]], x1, x2, x2, x1,
                            jnp.zeros((D_MODEL, head_w - src["kpe"][0] - LANES), BF16)], axis=1)
    w_mix = pack_shifted(p["w_in"], l, head, total, shift)
    w_gate = pack_shifted(p["w_in"], l, None, N_BRANCH * D_MODEL, -MIX_COLS)

    wq3 = p["mla_wq_up"][l].reshape(MLA_Q_LORA, MLA_HEADS, MLA_QK)
    nope, x1, x2 = wq3[:, :, :MLA_NOPE], wq3[:, :, MLA_NOPE:MLA_NOPE + half], wq3[:, :, MLA_NOPE + half:]
    z = jnp.zeros((MLA_Q_LORA, MLA_HEADS, LANES - MLA_ROPE), F32)
    wq_full = jnp.concatenate([nope, x1, x2, z], axis=-1).reshape(MLA_Q_LORA, MLA_HEADS * MLA_QK_PAD)
    wq_sw = jnp.concatenate([x2, x1, z], axis=-1).reshape(MLA_Q_LORA, MLA_HEADS * LANES)
    wkv3 = p["mla_wkv_up"][l].reshape(MLA_KV_LORA, MLA_HEADS, MLA_NOPE + MLA_V)
    zg = jnp.zeros((LANES - MLA_ROPE,), F32)
    qk_g = p["mla_qk_g"][l].astype(F32)

    return {
        "ln_mix_g": p["ln_mix_g"][l],
        "w_mix": w_mix, "w_gate": w_gate, "mix_gain": gain, "mix_flag": flag,
        "mla_qa_g": p["mla_qa_g"][l].reshape(1, -1).astype(F32),
        "mla_kva_g": p["mla_kva_g"][l].reshape(1, -1).astype(F32),
        "mla_wq": wq_full.astype(BF16), "mla_wq_sw": wq_sw.astype(BF16),
        "mla_wk": wkv3[:, :, :MLA_NOPE].reshape(MLA_KV_LORA, -1).astype(BF16),
        "mla_wv": wkv3[:, :, MLA_NOPE:].reshape(MLA_KV_LORA, -1).astype(BF16),
        "mla_gq": (jnp.concatenate([qk_g[0], zg]) * (MLA_QK ** -0.5 * LOG2E)).reshape(1, -1),
        "mla_gk": jnp.concatenate([qk_g[1], zg]).reshape(1, -1),
        "win_sink": p["win_sink"][l], "diff_lambda": p["diff_lambda"][l], "diff_subln_g": p["diff_subln_g"][l],
        "w_branch": cast_layer(p["w_branch"].reshape(DEPTH, N_BRANCH * BRANCH_W, D_MODEL), l).reshape(
            N_BRANCH, BRANCH_W, D_MODEL),
        "w_out": cast_layer(p["w_out"], l),
        "ln_mem_g": p["ln_mem_g"][l], "mem_ln_g": p["mem_ln_g"][l],
        "mem_wq": cast_layer(p["mem_wq"], l), "mem_wkv": cast_layer(p["mem_wkv"], l),
        "mem_q_gain": tile(p["mem_qk_g"][l, 0], MEM_HEADS) * (HEAD_DIM ** -0.5 * LOG2E),
        "mem_kv_gain": jnp.concatenate([tile(p["mem_qk_g"][l, 1], MEM_HEADS), jnp.ones((MEM_W,), F32)]),
        "mem_kv_flag": jnp.concatenate([jnp.ones((MEM_W,), F32), jnp.zeros((MEM_W,), F32)]),
        "mem_wo": cast_layer(p["mem_wo"], l),
        "ln_ffn_g": p["ln_ffn_g"][l],
        "ffn_w_in": cast_layer(p["ffn_w_in"], l), "ffn_w_out": cast_layer(p["ffn_w_out"], l),
    }


def rotary_tables(S):
    half = MLA_ROPE // 2
    inv_freq = ROPE_THETA ** (-jnp.arange(half, dtype=F32) / half)
    ang = jnp.arange(S, dtype=F32)[:, None] * inv_freq[None, :]
    cos, sin = jnp.cos(ang), jnp.sin(ang)
    z = jnp.zeros((S, LANES - MLA_ROPE), F32)
    return jnp.concatenate([cos, cos, z], axis=1), jnp.concatenate([-sin, sin, z], axis=1)


def _trunk(x, mem, layers, tables):
    B, S, D = x.shape
    M = B * S
    _, cols, _, _, _ = _mix_layout()
    ct, st = rotary_tables(S)
    xf = x.reshape(M, D)
    memf = mem.reshape(-1, D)
    dil_tq = _pick(S, (256,))
    for l, lw in enumerate(layers):
        h = rmsnorm(xf, lw["ln_mix_g"])
        mix = mm_segnorm(h, lw["w_mix"], lw["mix_gain"], lw["mix_flag"], tn_prefs=(MIX_TILE,))
        gates = mm_sigmoid(h, lw["w_gate"])

        q, k, v = mla_prep(mix, cols, S, ct, st, lw)
        o_mla = dense_attention(q, k, v, B, S, MLA_HEADS, MLA_QK_PAD, MLA_V)
        o_dil = band_attention(mix, cols["dq"], mix, cols["dk"], mix, cols["dv"], tables["dil_bias"],
                               jnp.full((DIL_HEADS,), NEG_INF, F32), B, S, DIL_HEADS, 1, DIL_HEADS // 2,
                               dil_tq, dil_tq, DIL_RADIUS)
        o_win = band_attention(mix, cols["wq"], mix, cols["wk"], mix, cols["wv"], tables["win_bias"],
                               lw["win_sink"], B, S, WIN_Q_HEADS, WIN_Q_HEADS // WIN_KV_HEADS, WIN_Q_HEADS,
                               2 * WIN_RADIUS, WIN_RADIUS, WIN_RADIUS)
        o_diff = diff_attention(mix, cols, tables["slopes_diff"], lw["diff_lambda"], lw["diff_subln_g"], B, S, l)

        merged = branch_merge((o_mla, o_dil, o_win, o_diff), lw["w_branch"], gates)
        xf = mm_residual(merged, lw["w_out"], xf)

        kvm = mm_segnorm(rmsnorm(memf, lw["mem_ln_g"]), lw["mem_wkv"], lw["mem_kv_gain"], lw["mem_kv_flag"])
        xf, hf = mem_attention(xf, kvm, lw, B, S)

        hid = mm_swiglu(hf, lw["ffn_w_in"], FFN_HIDDEN)
        xf = mm_residual(hid, lw["ffn_w_out"], xf, tm_prefs=(512, 256, 128), tn_prefs=(256, 128))
    return xf.reshape(B, S, D)


def kernel(x_prompt, x_sample, mem_prompt, mem_sample, ln_mix_g, w_in, mla_qa_g, mla_kva_g, mla_wq_up,
           mla_wkv_up, mla_qk_g, dil_qk_g, win_qk_g, win_sink, diff_qk_g, diff_lambda, diff_subln_g,
           w_branch, w_out, ln_mem_g, mem_ln_g, mem_wq, mem_wkv, mem_qk_g, mem_wo, ln_ffn_g, ffn_w_in,
           ffn_w_out):
    p = dict(ln_mix_g=ln_mix_g, w_in=w_in, mla_qa_g=mla_qa_g, mla_kva_g=mla_kva_g, mla_wq_up=mla_wq_up,
             mla_wkv_up=mla_wkv_up, mla_qk_g=mla_qk_g, dil_qk_g=dil_qk_g, win_qk_g=win_qk_g,
             win_sink=win_sink, diff_qk_g=diff_qk_g, diff_lambda=diff_lambda, diff_subln_g=diff_subln_g,
             w_branch=w_branch, w_out=w_out, ln_mem_g=ln_mem_g, mem_ln_g=mem_ln_g, mem_wq=mem_wq,
             mem_wkv=mem_wkv, mem_qk_g=mem_qk_g, mem_wo=mem_wo, ln_ffn_g=ln_ffn_g, ffn_w_in=ffn_w_in,
             ffn_w_out=ffn_w_out)
    layers = [pack_layer(p, l) for l in range(DEPTH)]
    tables = {
        "dil_bias": band_bias(alibi_slopes(DIL_HEADS), dil_multiplicity, 256, 256, DIL_RADIUS),
        "win_bias": band_bias(alibi_slopes(WIN_Q_HEADS), win_multiplicity, 2 * WIN_RADIUS, WIN_RADIUS,
                              WIN_RADIUS),
        "slopes_diff": alibi_slopes(DIFF_HEADS),
    }
    y_prompt = _trunk(x_prompt, mem_prompt, layers, tables)
    y_sample = _trunk(x_sample, mem_sample, layers, tables)
    return (y_prompt, y_sample)
```

```python
import functools
import math

import jax
import jax.numpy as jnp
import numpy as np
from jax import lax
from jax.experimental import pallas as pl
from jax.experimental.pallas import tpu as pltpu

F32 = jnp.float32
BF16 = jnp.bfloat16

D_MODEL = 4096
DEPTH = 2
EPS = 1e-6
NEG_INF = -1e30
N_BRANCH = 4
BRANCH_W = D_MODEL // N_BRANCH
HEAD_DIM = 128

MLA_NOPE = 128
MLA_ROPE = 64
MLA_V = 128
MLA_HEADS = BRANCH_W // MLA_V
MLA_Q_LORA = D_MODEL // 4
MLA_KV_LORA = D_MODEL // 8
MLA_QK = MLA_NOPE + MLA_ROPE
MLA_QK_PAD = 256
ROPE_THETA = 10000.0

DIL_HEADS = BRANCH_W // HEAD_DIM
DIL_PATTERNS = ((128, 1), (512, 4), (2048, 16))
DIL_RADIUS = max(w // 2 for w, _ in DIL_PATTERNS)

WIN_Q_HEADS = BRANCH_W // HEAD_DIM
WIN_KV_HEADS = WIN_Q_HEADS // 4
WIN_RADIUS = 128

DIFF_HD = 128
DIFF_HEADS = BRANCH_W // (2 * DIFF_HD)

MEM_HEADS = 4
MEM_W = MEM_HEADS * HEAD_DIM

FFN_HIDDEN = -(-8 * D_MODEL // (3 * 256)) * 256

MIX_SPLITS = (MLA_Q_LORA, MLA_KV_LORA, MLA_ROPE,
              BRANCH_W, BRANCH_W, BRANCH_W,
              BRANCH_W, WIN_KV_HEADS * HEAD_DIM, WIN_KV_HEADS * HEAD_DIM,
              BRANCH_W, BRANCH_W, BRANCH_W)
MIX_COLS = sum(MIX_SPLITS)
MIX_OFFSETS = tuple(int(o) for o in np.cumsum((0,) + MIX_SPLITS))

LOG2E = math.log2(math.e)
LANES = 128
MIX_TILE = 512
VMEM_LIMIT_BYTES = 56 * 1024 * 1024


def _pick(n, prefs):
    for p in prefs:
        if n % p == 0:
            return p
    raise ValueError(f"no tile in {prefs} divides {n}")


def _params(*sem):
    return pltpu.CompilerParams(dimension_semantics=sem, vmem_limit_bytes=VMEM_LIMIT_BYTES)


def _rmsnorm_kernel(x_ref, g_ref, o_ref):
    x = x_ref[...].astype(F32)
    ms = jnp.mean(x * x, axis=-1, keepdims=True)
    o_ref[...] = (x * lax.rsqrt(ms + EPS) * g_ref[...]).astype(o_ref.dtype)


def rmsnorm(x, g):
    M, D = x.shape
    tm = _pick(M, (512, 256, 128, 64, 8))
    return pl.pallas_call(
        _rmsnorm_kernel,
        grid=(M // tm,),
        in_specs=[pl.BlockSpec((tm, D), lambda i: (i, 0)),
                  pl.BlockSpec((1, D), lambda i: (0, 0))],
        out_specs=pl.BlockSpec((tm, D), lambda i: (i, 0)),
        out_shape=jax.ShapeDtypeStruct((M, D), BF16),
        compiler_params=_params("parallel"),
        name="rmsnorm",
    )(x, g.reshape(1, D).astype(F32))


def _dot(a, b):
    return jnp.dot(a, b, preferred_element_type=F32)


MXU_COLS = 256


def _sigmoid(x):
    return 0.5 * jnp.tanh(0.5 * x) + 0.5


def _col_slabs(n):
    w = MXU_COLS if n % MXU_COLS == 0 else n
    return [slice(c, c + w) for c in range(0, n, w)]


def _mm_sigmoid_kernel(a_ref, b_ref, o_ref):
    for cs in _col_slabs(o_ref.shape[1]):
        o_ref[:, cs] = _sigmoid(_dot(a_ref[...], b_ref[:, cs])).astype(o_ref.dtype)


def _mm_segnorm_kernel(a_ref, b_ref, g_ref, f_ref, o_ref):
    for cs in _col_slabs(o_ref.shape[1]):
        acc = _dot(a_ref[...], b_ref[:, cs])
        for c in range(0, acc.shape[1], LANES):
            seg = slice(cs.start + c, cs.start + c + LANES)
            y = acc[:, c:c + LANES]
            r = lax.rsqrt(jnp.mean(y * y, axis=-1, keepdims=True) + EPS)
            mult = jnp.where(f_ref[:, seg] > 0.0, r, 1.0) * g_ref[:, seg]
            o_ref[:, seg] = (y * mult).astype(o_ref.dtype)


def _mm_residual_kernel(a_ref, b_ref, r_ref, o_ref):
    for cs in _col_slabs(o_ref.shape[1]):
        o_ref[:, cs] = r_ref[:, cs] + _dot(a_ref[...], b_ref[:, cs])


def _mm_swiglu_kernel(a_ref, bg_ref, bu_ref, o_ref):
    half = a_ref.shape[0] // 2
    for rows in (slice(0, half), slice(half, 2 * half)):
        a = a_ref[rows, :]
        g = _dot(a, bg_ref[...])
        u = _dot(a, bu_ref[...])
        o_ref[rows, :] = (g * _sigmoid(g) * u).astype(o_ref.dtype)


def mm_sigmoid(a, w, tm_prefs=(1024, 512, 256, 128), tn_prefs=(512, 256, 128)):
    M, K = a.shape
    N = w.shape[1]
    tm, tn = _pick(M, tm_prefs), _pick(N, tn_prefs)
    return pl.pallas_call(
        _mm_sigmoid_kernel,
        grid=(M // tm, N // tn),
        in_specs=[pl.BlockSpec((tm, K), lambda i, j: (i, 0)),
                  pl.BlockSpec((K, tn), lambda i, j: (0, j))],
        out_specs=pl.BlockSpec((tm, tn), lambda i, j: (i, j)),
        out_shape=jax.ShapeDtypeStruct((M, N), BF16),
        compiler_params=_params("parallel", "arbitrary"),
        name="mm_sigmoid",
    )(a, w)


def mm_segnorm(a, w, gain, flag, tm_prefs=(1024, 512, 256, 128), tn_prefs=(512, 256, 128)):
    M, K = a.shape
    N = gain.shape[0]
    tm, tn = _pick(M, tm_prefs), _pick(N, tn_prefs)
    return pl.pallas_call(
        _mm_segnorm_kernel,
        grid=(M // tm, N // tn),
        in_specs=[pl.BlockSpec((tm, K), lambda i, j: (i, 0)),
                  pl.BlockSpec((K, tn), lambda i, j: (0, j)),
                  pl.BlockSpec((1, tn), lambda i, j: (0, j)),
                  pl.BlockSpec((1, tn), lambda i, j: (0, j))],
        out_specs=pl.BlockSpec((tm, tn), lambda i, j: (i, j)),
        out_shape=jax.ShapeDtypeStruct((M, N), BF16),
        compiler_params=_params("parallel", "arbitrary"),
        name="mm_segnorm",
    )(a, w, gain.reshape(1, N).astype(F32), flag.reshape(1, N).astype(F32))


def mm_residual(a, w, res, tm_prefs=(1024, 512, 256, 128), tn_prefs=(512, 256, 128)):
    M, K = a.shape
    N = w.shape[1]
    tm, tn = _pick(M, tm_prefs), _pick(N, tn_prefs)
    return pl.pallas_call(
        _mm_residual_kernel,
        grid=(M // tm, N // tn),
        in_specs=[pl.BlockSpec((tm, K), lambda i, j: (i, 0)),
                  pl.BlockSpec((K, tn), lambda i, j: (0, j)),
                  pl.BlockSpec((tm, tn), lambda i, j: (i, j))],
        out_specs=pl.BlockSpec((tm, tn), lambda i, j: (i, j)),
        out_shape=jax.ShapeDtypeStruct((M, N), F32),
        compiler_params=_params("parallel", "arbitrary"),
        name="mm_residual",
    )(a, w, res)


def _mm_residual_ksplit_kernel(a_ref, b_ref, r_ref, o_ref, acc_ref, *, nk):
    k = pl.program_id(1)
    j = pl.program_id(2)
    part = _dot(a_ref[...], b_ref[...])

    @pl.when(k == 0)
    def _():
        acc_ref[j] = part

    if nk > 2:
        @pl.when(jnp.logical_and(k > 0, k < nk - 1))
        def _():
            acc_ref[j] += part

    @pl.when(k == nk - 1)
    def _():
        o_ref[...] = r_ref[...] + acc_ref[j] + part


def mm_residual_ksplit(a, w, res, nk, tm, tn):
    M, K = a.shape
    N = w.shape[1]
    tk = K // nk
    assert K % nk == 0 and tk % LANES == 0 and M % tm == 0 and N % tn == 0
    last = nk - 1
    return pl.pallas_call(
        functools.partial(_mm_residual_ksplit_kernel, nk=nk),
        grid=(M // tm, nk, N // tn),
        in_specs=[pl.BlockSpec((tm, tk), lambda i, k, j: (i, k)),
                  pl.BlockSpec((tk, tn), lambda i, k, j: (k, j)),
                  pl.BlockSpec((tm, tn), lambda i, k, j: (i, jnp.where(k == last, j, 0)))],
        out_specs=pl.BlockSpec((tm, tn), lambda i, k, j: (i, jnp.where(k == last, j, 0))),
        out_shape=jax.ShapeDtypeStruct((M, N), F32),
        scratch_shapes=[pltpu.VMEM((N // tn, tm, tn), F32)],
        compiler_params=_params("parallel", "arbitrary", "arbitrary"),
        name="mm_residual_ksplit",
    )(a, w, res)


def mm_swiglu(a, w, hidden):
    M, K = a.shape
    tm = _pick(M, (1024, 512, 256, 128))
    tn = _pick(hidden, (256, 128))
    nj = hidden // tn
    return pl.pallas_call(
        _mm_swiglu_kernel,
        grid=(M // tm, nj),
        in_specs=[pl.BlockSpec((tm, K), lambda i, j: (i, 0)),
                  pl.BlockSpec((K, tn), lambda i, j: (0, j)),
                  pl.BlockSpec((K, tn), lambda i, j: (0, j + nj))],
        out_specs=pl.BlockSpec((tm, tn), lambda i, j: (i, j)),
        out_shape=jax.ShapeDtypeStruct((M, hidden), BF16),
        compiler_params=_params("parallel", "arbitrary"),
        name="mm_swiglu",
    )(a, w, w)


def _branch_kernel(oa_ref, ob_ref, oc_ref, od_ref, w_ref, ga_ref, gb_ref, gc_ref, gd_ref, o_ref):
    for cs in _col_slabs(o_ref.shape[1]):
        acc = ga_ref[:, cs].astype(F32) * _dot(oa_ref[...], w_ref[0, :, cs])
        acc += gb_ref[:, cs].astype(F32) * _dot(ob_ref[...], w_ref[1, :, cs])
        acc += gc_ref[:, cs].astype(F32) * _dot(oc_ref[...], w_ref[2, :, cs])
        acc += gd_ref[:, cs].astype(F32) * _dot(od_ref[...], w_ref[3, :, cs])
        o_ref[:, cs] = acc.astype(o_ref.dtype)


def branch_merge(outs, w_branch, gates):
    M, W = outs[0].shape
    D = w_branch.shape[2]
    tm = _pick(M, (1024, 512, 256, 128))
    tn = _pick(D, (512, 256, 128))
    nj = D // tn
    o_spec = pl.BlockSpec((tm, W), lambda i, j: (i, 0))
    g_specs = [pl.BlockSpec((tm, tn), functools.partial(lambda i, j, b: (i, b * nj + j), b=b))
               for b in range(N_BRANCH)]
    return pl.pallas_call(
        _branch_kernel,
        grid=(M // tm, nj),
        in_specs=[o_spec] * N_BRANCH + [pl.BlockSpec((N_BRANCH, W, tn), lambda i, j: (0, 0, j))] + g_specs,
        out_specs=pl.BlockSpec((tm, tn), lambda i, j: (i, j)),
        out_shape=jax.ShapeDtypeStruct((M, D), BF16),
        compiler_params=_params("parallel", "arbitrary"),
        name="branch_merge",
    )(*outs, w_branch, gates, gates, gates, gates)


def _mla_prep_kernel(qa_ref, kva_ref, kpe_ref, ct_ref, st_ref, qag_ref, kvag_ref,
                     wq_ref, wqs_ref, wk_ref, wv_ref, gq_ref, gk_ref,
                     q_ref, k_ref, v_ref):
    def norm(x, g):
        return (x * lax.rsqrt(jnp.mean(x * x, axis=-1, keepdims=True) + EPS) * g).astype(BF16)

    ct = ct_ref[...]
    st = st_ref[...]
    qn = norm(qa_ref[...].astype(F32), qag_ref[...])
    kvn = norm(kva_ref[...].astype(F32), kvag_ref[...])
    qfull = _dot(qn, wq_ref[...])
    qsw = _dot(qn, wqs_ref[...])
    knope = _dot(kvn, wk_ref[...])
    vals = _dot(kvn, wv_ref[...])
    ones = jnp.ones((vals.shape[0], MLA_V), v_ref.dtype)
    for h in range(MLA_HEADS):
        v_ref[:, 2 * h * MLA_V:(2 * h + 1) * MLA_V] = vals[:, h * MLA_V:(h + 1) * MLA_V].astype(v_ref.dtype)
        v_ref[:, (2 * h + 1) * MLA_V:(2 * h + 2) * MLA_V] = ones

    kpe = kpe_ref[...].astype(F32)
    kpe_rot = kpe * ct + pltpu.roll(kpe, 2 * (MLA_ROPE // 2), 1) * st
    kpe_ss = jnp.sum(kpe_rot * kpe_rot, axis=-1, keepdims=True)
    gq_n, gq_r = gq_ref[:, :LANES], gq_ref[:, LANES:]
    gk_n, gk_r = gk_ref[:, :LANES], gk_ref[:, LANES:]
    for h in range(MLA_HEADS):
        lo = h * MLA_QK_PAD
        q_n = qfull[:, lo:lo + LANES]
        q_r = qfull[:, lo + LANES:lo + 2 * LANES] * ct + qsw[:, h * LANES:(h + 1) * LANES] * st
        ss = jnp.sum(q_n * q_n, axis=-1, keepdims=True) + jnp.sum(q_r * q_r, axis=-1, keepdims=True)
        r = lax.rsqrt(ss * (1.0 / MLA_QK) + EPS)
        q_ref[:, lo:lo + LANES] = (q_n * r * gq_n).astype(q_ref.dtype)
        q_ref[:, lo + LANES:lo + 2 * LANES] = (q_r * r * gq_r).astype(q_ref.dtype)
        k_n = knope[:, h * LANES:(h + 1) * LANES]
        ss = jnp.sum(k_n * k_n, axis=-1, keepdims=True) + kpe_ss
        r = lax.rsqrt(ss * (1.0 / MLA_QK) + EPS)
        k_ref[:, lo:lo + LANES] = (k_n * r * gk_n).astype(k_ref.dtype)
        k_ref[:, lo + LANES:lo + 2 * LANES] = (kpe_rot * r * gk_r).astype(k_ref.dtype)


def mla_prep(mix, cols, S, ct, st, lw):
    M = mix.shape[0]
    tm = _pick(S, (256, 128))
    ns = S // tm
    HQ = MLA_HEADS * MLA_QK_PAD
    HV = MLA_HEADS * 2 * MLA_V

    def col(width, off):
        assert off % width == 0
        return pl.BlockSpec((tm, width), lambda i: (i, off // width))

    def whole(a):
        return pl.BlockSpec(a.shape, lambda i: (0,) * a.ndim)

    consts = (lw["mla_qa_g"], lw["mla_kva_g"], lw["mla_wq"], lw["mla_wq_sw"], lw["mla_wk"], lw["mla_wv"],
              lw["mla_gq"], lw["mla_gk"])
    return pl.pallas_call(
        _mla_prep_kernel,
        grid=(M // tm,),
        in_specs=[col(MLA_Q_LORA, cols["qa"]), col(MLA_KV_LORA, cols["kva"]), col(LANES, cols["kpe"]),
                  pl.BlockSpec((tm, LANES), lambda i: (i % ns, 0)),
                  pl.BlockSpec((tm, LANES), lambda i: (i % ns, 0))] + [whole(c) for c in consts],
        out_specs=[pl.BlockSpec((tm, HQ), lambda i: (i, 0)),
                   pl.BlockSpec((tm, HQ), lambda i: (i, 0)),
                   pl.BlockSpec((tm, HV), lambda i: (i, 0))],
        out_shape=[jax.ShapeDtypeStruct((M, HQ), BF16),
                   jax.ShapeDtypeStruct((M, HQ), BF16),
                   jax.ShapeDtypeStruct((M, HV), BF16)],
        compiler_params=_params("parallel"),
        name="mla_prep",
    )(mix, mix, mix, ct, st, *consts)


def _qk(q, k):
    return lax.dot_general(q, k, (((1,), (1,)), ((), ())), preferred_element_type=F32)


def _dense_attn_kernel(q_ref, k_ref, v_ref, o_ref, *, n_sub, chunk):
    dv = o_ref.shape[1]
    ts = q_ref.shape[0] // n_sub
    seq = k_ref.shape[0]
    for t in range(n_sub):
        rows = slice(t * ts, (t + 1) * ts)
        q = q_ref[rows, :]
        pieces = [_qk(q, k_ref[c:c + chunk, :]) for c in range(0, seq, chunk)]
        mt = _lane_fold(pieces[0], jnp.maximum)
        for piece in pieces[1:]:
            mt = jnp.maximum(mt, _lane_fold(piece, jnp.maximum))
        m = jnp.max(mt, axis=-1, keepdims=True)
        o = jnp.zeros((ts, 2 * dv), F32)
        for c, piece in zip(range(0, seq, chunk), pieces):
            o = o + _dot(jnp.exp2(piece - m).astype(BF16), v_ref[c:c + chunk, :])
        o_ref[rows, :] = (o[:, :dv] / o[:, dv:]).astype(o_ref.dtype)


def dense_attention(q, k, v, B, S, H, dk, dv, n_sub=4):
    tq = _pick(S, (1024, 512, 256))
    chunk = _pick(S, (512, 256))
    q3, k3, v3 = (t.reshape(B, S, t.shape[1]) for t in (q, k, v))
    out = pl.pallas_call(
        functools.partial(_dense_attn_kernel, n_sub=n_sub, chunk=chunk),
        grid=(B, H, S // tq),
        in_specs=[pl.BlockSpec((None, tq, dk), lambda b, h, i: (b, i, h)),
                  pl.BlockSpec((None, S, dk), lambda b, h, i: (b, 0, h)),
                  pl.BlockSpec((None, S, 2 * dv), lambda b, h, i: (b, 0, h))],
        out_specs=pl.BlockSpec((None, tq, dv), lambda b, h, i: (b, i, h)),
        out_shape=jax.ShapeDtypeStruct((B, S, H * dv), BF16),
        compiler_params=_params("parallel", "parallel", "arbitrary"),
        name="dense_attention",
    )(q3, k3, v3)
    return out.reshape(B * S, H * dv)


def _lane_fold(x, op):
    out = x[:, :LANES]
    for j in range(1, x.shape[1] // LANES):
        out = op(out, x[:, j * LANES:(j + 1) * LANES])
    return out


def _band_kernel(sink_ref, q_ref, k_ref, v_ref, bias_ref, o_ref, *, tq, tk, radius, seq, n_chunks, hpb, rep):
    hb = pl.program_id(1)
    q0 = pl.program_id(2) * tq
    starts, tiles = [], []
    for c in range(n_chunks):
        start = q0 - radius + c * tk
        inside = jnp.logical_and(start >= 0, start + tk <= seq)
        starts.append(pl.multiple_of(jnp.clip(start, 0, seq - tk), tk))
        tiles.append(jnp.where(inside, c, n_chunks))
    for hh in range(hpb):
        qs = slice(hh * HEAD_DIM, (hh + 1) * HEAD_DIM)
        ks = slice((hh // rep) * HEAD_DIM, (hh // rep + 1) * HEAD_DIM)
        q = q_ref[:, qs]
        scores = [_qk(q, k_ref[pl.ds(sc, tk), ks]) + bias_ref[hh, t] for sc, t in zip(starts, tiles)]
        mt = _lane_fold(scores[0], jnp.maximum)
        for s in scores[1:]:
            mt = jnp.maximum(mt, _lane_fold(s, jnp.maximum))
        sink = sink_ref[hb * hpb + hh] * LOG2E
        m = jnp.maximum(jnp.max(mt, axis=-1, keepdims=True), sink)
        lt = jnp.zeros((tq, LANES), F32)
        acc = jnp.zeros((tq, HEAD_DIM), F32)
        for s, sc in zip(scores, starts):
            p = jnp.exp2(s - m)
            lt = lt + _lane_fold(p, jnp.add)
            acc = acc + _dot(p.astype(BF16), v_ref[pl.ds(sc, tk), ks])
        l = jnp.sum(lt, axis=-1, keepdims=True) + jnp.exp2(sink - m)
        o_ref[:, qs] = (acc / l).astype(o_ref.dtype)


def band_bias(slopes, mult_fn, tq, tk, radius):
    n_chunks = (tq + 2 * radius) // tk
    a = jnp.arange(tq)[None, :, None]
    c = jnp.arange(tk)[None, None, :]
    delta = (jnp.arange(n_chunks)[:, None, None] * tk - radius) + c - a
    mult = mult_fn(delta)
    dist = jnp.abs(delta).astype(F32)
    logm = jnp.log2(jnp.maximum(mult, 1).astype(F32))
    bias = logm[None] - (slopes * LOG2E)[:, None, None, None] * dist[None]
    bias = jnp.where(mult[None] > 0, bias, NEG_INF)
    dead = jnp.full((slopes.shape[0], 1, tq, tk), NEG_INF, F32)
    return jnp.concatenate([bias, dead], axis=1)


def dil_multiplicity(delta):
    m = jnp.zeros(delta.shape, jnp.int32)
    for window, dil in DIL_PATTERNS:
        m = m + ((delta % dil == 0) & (jnp.abs(delta) <= window // 2)).astype(jnp.int32)
    return m


def win_multiplicity(delta):
    return (jnp.abs(delta) <= WIN_RADIUS).astype(jnp.int32)


def band_attention(qarr, q_off, karr, k_off, varr, v_off, bias, sinks, B, S, H, rep, hpb, tq, tk, radius):
    n_chunks = bias.shape[1] - 1
    assert radius % tk == 0 and tq % tk == 0 and S % tq == 0 and S >= tk and hpb % rep == 0 and H % hpb == 0
    wq, wk = hpb * HEAD_DIM, (hpb // rep) * HEAD_DIM
    assert q_off % wq == 0 and k_off % wk == 0 and v_off % wk == 0
    qb, kb, vb = q_off // wq, k_off // wk, v_off // wk
    q3, k3, v3 = (t.reshape(B, S, t.shape[1]) for t in (qarr, karr, varr))
    kern = functools.partial(_band_kernel, tq=tq, tk=tk, radius=radius, seq=S, n_chunks=n_chunks, hpb=hpb, rep=rep)
    out = pl.pallas_call(
        kern,
        grid=(B, H // hpb, S // tq),
        in_specs=[pl.BlockSpec(memory_space=pltpu.SMEM),
                  pl.BlockSpec((None, tq, wq), lambda b, h, i: (b, i, qb + h)),
                  pl.BlockSpec((None, S, wk), lambda b, h, i: (b, 0, kb + h)),
                  pl.BlockSpec((None, S, wk), lambda b, h, i: (b, 0, vb + h)),
                  pl.BlockSpec((hpb, n_chunks + 1, tq, tk), lambda b, h, i: (h, 0, 0, 0))],
        out_specs=pl.BlockSpec((None, tq, wq), lambda b, h, i: (b, i, h)),
        out_shape=jax.ShapeDtypeStruct((B, S, H * HEAD_DIM), BF16),
        compiler_params=_params("parallel", "parallel", "arbitrary"),
        name="band_attention",
    )(sinks.astype(F32), q3, k3, v3, bias)
    return out.reshape(B * S, H * HEAD_DIM)


def _diff_kernel(slope_ref, lam_ref, q_ref, k_ref, v_ref, pos_ref, g_ref, o_ref, *, tq, lam_init, chunk, n_sub):
    h = pl.program_id(1)
    q0 = pl.program_id(2) * tq
    lp = lam_ref[...]
    lam = (jnp.exp(jnp.sum(lp[0:1] * lp[1:2], axis=-1, keepdims=True))
           - jnp.exp(jnp.sum(lp[2:3] * lp[3:4], axis=-1, keepdims=True)) + lam_init)
    slope = slope_ref[h]
    seq = k_ref.shape[0]
    kpos = pos_ref[...] * slope
    ts = tq // n_sub
    for t in range(n_sub):
        rows = slice(t * ts, (t + 1) * ts)
        qrow = (q0 + t * ts + lax.broadcasted_iota(jnp.int32, (ts, LANES), 0)).astype(F32) * slope
        qpos = jnp.concatenate([qrow] * (chunk // LANES), axis=1)
        q1, q2 = q_ref[rows, :DIFF_HD], q_ref[rows, DIFF_HD:]
        t1, t2, mt1, mt2 = [], [], None, None
        for c in range(0, seq, chunk):
            b = jnp.abs(kpos[:, c:c + chunk] - qpos)
            a1 = _qk(q1, k_ref[c:c + chunk, :DIFF_HD]) - b
            a2 = _qk(q2, k_ref[c:c + chunk, DIFF_HD:]) - b
            t1.append(a1)
            t2.append(a2)
            f1, f2 = _lane_fold(a1, jnp.maximum), _lane_fold(a2, jnp.maximum)
            mt1 = f1 if mt1 is None else jnp.maximum(mt1, f1)
            mt2 = f2 if mt2 is None else jnp.maximum(mt2, f2)
        m1 = jnp.max(mt1, axis=-1, keepdims=True)
        m2 = jnp.max(mt2, axis=-1, keepdims=True)
        lt1 = jnp.zeros((ts, LANES), F32)
        lt2 = jnp.zeros((ts, LANES), F32)
        o1 = jnp.zeros((ts, 2 * DIFF_HD), F32)
        o2 = jnp.zeros((ts, 2 * DIFF_HD), F32)
        for i, c in enumerate(range(0, seq, chunk)):
            p1, p2 = jnp.exp2(t1[i] - m1), jnp.exp2(t2[i] - m2)
            lt1 = lt1 + _lane_fold(p1, jnp.add)
            lt2 = lt2 + _lane_fold(p2, jnp.add)
            o1 = o1 + _dot(p1.astype(BF16), v_ref[c:c + chunk, :])
            o2 = o2 + _dot(p2.astype(BF16), v_ref[c:c + chunk, :])
        l1 = jnp.sum(lt1, axis=-1, keepdims=True)
        l2 = jnp.sum(lt2, axis=-1, keepdims=True)
        o = o1 * (1.0 / l1) - o2 * (lam / l2)
        r = lax.rsqrt(jnp.mean(o * o, axis=-1, keepdims=True) + EPS)
        o_ref[rows, :] = (o * r * g_ref[...] * (1.0 - lam_init)).astype(o_ref.dtype)


def diff_attention(mix, cols, slopes, lam_p, subln_g, B, S, layer):
    tq = _pick(S, (512, 256, 128))
    lam_init = 0.8 - 0.6 * math.exp(-0.3 * layer)
    W = 2 * DIFF_HD
    qb, kb, vb = (cols[n] // W for n in ("fq", "fk", "fv"))
    m3 = mix.reshape(B, S, mix.shape[1])
    pos = jnp.arange(S, dtype=F32).reshape(1, S)
    kern = functools.partial(_diff_kernel, tq=tq, lam_init=lam_init, chunk=_pick(S, (512, 256)),
                             n_sub=2 if S > 2048 else 4)
    out = pl.pallas_call(
        kern,
        grid=(B, DIFF_HEADS, S // tq),
        in_specs=[pl.BlockSpec(memory_space=pltpu.SMEM),
                  pl.BlockSpec((4, DIFF_HD), lambda b, h, i: (0, 0)),
                  pl.BlockSpec((None, tq, W), lambda b, h, i: (b, i, qb + h)),
                  pl.BlockSpec((None, S, W), lambda b, h, i: (b, 0, kb + h)),
                  pl.BlockSpec((None, S, W), lambda b, h, i: (b, 0, vb + h)),
                  pl.BlockSpec((1, S), lambda b, h, i: (0, 0)),
                  pl.BlockSpec((1, W), lambda b, h, i: (0, 0))],
        out_specs=pl.BlockSpec((None, tq, W), lambda b, h, i: (b, i, h)),
        out_shape=jax.ShapeDtypeStruct((B, S, DIFF_HEADS * W), BF16),
        compiler_params=_params("parallel", "parallel", "arbitrary"),
        name="diff_attention",
    )((slopes * LOG2E).astype(F32), lam_p.astype(F32), m3, m3, m3, pos, subln_g.reshape(1, W).astype(F32))
    return out.reshape(B * S, DIFF_HEADS * W)


def _mem_attn_kernel(x_ref, gm_ref, wq_ref, gq_ref, kv_ref, wo_ref, gf_ref, o_ref, hf_ref):
    def rms(t):
        return lax.rsqrt(jnp.mean(t * t, axis=-1, keepdims=True) + EPS)

    x = x_ref[...]
    hm = (x * rms(x) * gm_ref[...]).astype(BF16)
    qacc = _dot(hm, wq_ref[...])
    heads = []
    for h in range(MEM_HEADS):
        seg = slice(h * HEAD_DIM, (h + 1) * HEAD_DIM)
        y = qacc[:, seg]
        q = (y * rms(y) * gq_ref[:, seg]).astype(BF16)
        s = _qk(q, kv_ref[:, seg])
        p = jnp.exp2(s - jnp.max(s, axis=-1, keepdims=True))
        l = jnp.sum(p, axis=-1, keepdims=True)
        vseg = slice(MEM_W + h * HEAD_DIM, MEM_W + (h + 1) * HEAD_DIM)
        heads.append((_dot(p.astype(BF16), kv_ref[:, vseg]) / l).astype(BF16))
    x2 = x + _dot(jnp.concatenate(heads, axis=-1), wo_ref[...])
    o_ref[...] = x2
    hf_ref[...] = (x2 * rms(x2) * gf_ref[...]).astype(hf_ref.dtype)


def mem_attention(x, kv, lw, B, S):
    D = x.shape[1]
    Mt = kv.shape[0] // B
    tq = _pick(S, (256, 128))
    kv3 = kv.reshape(B, Mt, 2 * MEM_W)
    x3 = x.reshape(B, S, D)

    def row(n):
        return pl.BlockSpec((1, n), lambda b, i: (0, 0))

    out, hf = pl.pallas_call(
        _mem_attn_kernel,
        grid=(B, S // tq),
        in_specs=[pl.BlockSpec((None, tq, D), lambda b, i: (b, i, 0)),
                  row(D),
                  pl.BlockSpec((D, MEM_W), lambda b, i: (0, 0)),
                  row(MEM_W),
                  pl.BlockSpec((None, Mt, 2 * MEM_W), lambda b, i: (b, 0, 0)),
                  pl.BlockSpec((MEM_W, D), lambda b, i: (0, 0)),
                  row(D)],
        out_specs=[pl.BlockSpec((None, tq, D), lambda b, i: (b, i, 0)),
                   pl.BlockSpec((None, tq, D), lambda b, i: (b, i, 0))],
        out_shape=[jax.ShapeDtypeStruct((B, S, D), F32), jax.ShapeDtypeStruct((B, S, D), BF16)],
        compiler_params=_params("parallel", "arbitrary"),
        name="mem_attention",
    )(x3, lw["ln_mem_g"].reshape(1, D).astype(F32), lw["mem_wq"], lw["mem_q_gain"].reshape(1, MEM_W),
      kv3, lw["mem_wo"], lw["ln_ffn_g"].reshape(1, D).astype(F32))
    return out.reshape(B * S, D), hf.reshape(B * S, D)


def alibi_slopes(n):
    return 2.0 ** (-8.0 * jnp.arange(1, n + 1, dtype=F32) / n)


MIX_NAMES = ("qa", "kva", "kpe", "dq", "dk", "dv", "wq", "wk", "wv", "fq", "fk", "fv")


def _mix_layout():
    src = {n: (MIX_OFFSETS[i], MIX_OFFSETS[i + 1]) for i, n in enumerate(MIX_NAMES)}
    head = -(-(src["kpe"][0] + LANES) // MIX_TILE) * MIX_TILE
    shift = head - src["dq"][0]
    cols = {n: src[n][0] + (0 if n in ("qa", "kva", "kpe") else shift) for n in MIX_NAMES}
    total = -(-(MIX_COLS + shift) // MIX_TILE) * MIX_TILE
    return src, cols, head, shift, total


def _pack_kernel(main_ref, extra_ref, o_ref, *, n_head, kpe_off, r, tn):
    j = pl.program_id(1)

    @pl.when(j >= n_head)
    def _():
        x = jnp.concatenate([main_ref[...], extra_ref[...]], axis=1)
        o_ref[...] = x[:, tn - r:2 * tn - r].astype(o_ref.dtype)

    if n_head:
        @pl.when(j < n_head - 1)
        def _():
            o_ref[...] = main_ref[...].astype(o_ref.dtype)

        @pl.when(j == n_head - 1)
        def _():
            quarter = MLA_ROPE // 2
            blk = main_ref[:, kpe_off:kpe_off + LANES]
            lane = lax.broadcasted_iota(jnp.int32, blk.shape, 1)
            kpe = jnp.where(lane < 2 * quarter, blk,
                            jnp.where(lane < 3 * quarter, pltpu.roll(blk, quarter, 1),
                                      pltpu.roll(blk, 3 * quarter, 1)))
            if kpe_off:
                o_ref[:, :kpe_off] = main_ref[:, :kpe_off].astype(o_ref.dtype)
            o_ref[:, kpe_off:kpe_off + LANES] = kpe.astype(o_ref.dtype)
            if kpe_off + LANES < tn:
                o_ref[:, kpe_off + LANES:] = jnp.zeros((o_ref.shape[0], tn - kpe_off - LANES), o_ref.dtype)


def _cast_kernel(x_ref, o_ref):
    o_ref[...] = x_ref[...].astype(o_ref.dtype)


def cast_layer(w, l):
    _, R, C = w.shape
    tc = _pick(C, (2048, 1024, 512, 256, 128))
    tr = _pick(R, tuple(t for t in (2048, 1024, 512, 256, 128, 64, 8) if t * tc <= 2 ** 21))
    return pl.pallas_call(
        _cast_kernel,
        grid=(R // tr, C // tc),
        in_specs=[pl.BlockSpec((None, tr, tc), lambda i, j: (l, i, j))],
        out_specs=pl.BlockSpec((tr, tc), lambda i, j: (i, j)),
        out_shape=jax.ShapeDtypeStruct((R, C), BF16),
        compiler_params=_params("parallel", "arbitrary"),
        name="cast_layer",
    )(w)


def pack_shifted(src, l, n_out, shift, head_cols=0, kpe_col=0):
    _, K, C = src.shape
    tn = MIX_TILE
    tr = _pick(K, (1024, 512, 256))
    q, r = divmod(shift, tn)
    ew = next(e for e in (128, 256, 512) if e >= tn - r)
    n_head = head_cols // tn
    assert head_cols % tn == 0 and (n_head == 0 or kpe_col // tn == n_head - 1)
    kern = functools.partial(_pack_kernel, n_head=n_head, kpe_off=kpe_col % tn, r=r, tn=tn)
    return pl.pallas_call(
        kern,
        grid=(K // tr, n_out // tn),
        in_specs=[pl.BlockSpec((None, tr, tn),
                               lambda i, j: (l, i, jnp.where(j < n_head, j, jnp.maximum(j - q - 1, 0)))),
                  pl.BlockSpec((None, tr, ew), lambda i, j: (l, i, jnp.maximum(j - q, 0) * (tn // ew)))],
        out_specs=pl.BlockSpec((tr, tn), lambda i, j: (i, j)),
        out_shape=jax.ShapeDtypeStruct((K, n_out), BF16),
        compiler_params=_params("parallel", "arbitrary"),
        name="pack_shifted",
    )(src, src)


def pack_layer(p, l):
    src, cols, head_w, shift, total = _mix_layout()
    half = MLA_ROPE // 2
    qscale = HEAD_DIM ** -0.5 * LOG2E

    def tile(g, n):
        return jnp.tile(g.astype(F32), n)

    norm_gain = {
        "dq": tile(p["dil_qk_g"][l, 0], DIL_HEADS) * qscale, "dk": tile(p["dil_qk_g"][l, 1], DIL_HEADS),
        "fq": tile(p["diff_qk_g"][l, 0], 2 * DIFF_HEADS) * (DIFF_HD ** -0.5 * LOG2E),
        "fk": tile(p["diff_qk_g"][l, 1], 2 * DIFF_HEADS),
        "wq": tile(p["win_qk_g"][l, 0], WIN_Q_HEADS) * qscale, "wk": tile(p["win_qk_g"][l, 1], WIN_KV_HEADS),
    }
    gain = jnp.ones((total,), F32)
    flag = jnp.zeros((total,), F32)
    for n, g in norm_gain.items():
        gain = gain.at[cols[n]:cols[n] + g.shape[0]].set(g)
        flag = flag.at[cols[n]:cols[n] + g.shape[0]].set(1.0)

    w_mix = pack_shifted(p["w_in"], l, total, shift, head_cols=head_w, kpe_col=src["kpe"][0])
    w_gate = pack_shifted(p["w_in"], l, N_BRANCH * D_MODEL, -MIX_COLS)

    wq3 = p["mla_wq_up"][l].reshape(MLA_Q_LORA, MLA_HEADS, MLA_QK)
    nope, x1, x2 = wq3[:, :, :MLA_NOPE], wq3[:, :, MLA_NOPE:MLA_NOPE + half], wq3[:, :, MLA_NOPE + half:]
    z = jnp.zeros((MLA_Q_LORA, MLA_HEADS, LANES - MLA_ROPE), F32)
    wq_full = jnp.concatenate([nope, x1, x2, z], axis=-1).reshape(MLA_Q_LORA, MLA_HEADS * MLA_QK_PAD)
    wq_sw = jnp.concatenate([x2, x1, z], axis=-1).reshape(MLA_Q_LORA, MLA_HEADS * LANES)
    wkv3 = p["mla_wkv_up"][l].reshape(MLA_KV_LORA, MLA_HEADS, MLA_NOPE + MLA_V)
    zg = jnp.zeros((LANES - MLA_ROPE,), F32)
    qk_g = p["mla_qk_g"][l].astype(F32)

    return {
        "ln_mix_g": p["ln_mix_g"][l],
        "w_mix": w_mix, "w_gate": w_gate, "mix_gain": gain, "mix_flag": flag,
        "mla_qa_g": p["mla_qa_g"][l].reshape(1, -1).astype(F32),
        "mla_kva_g": p["mla_kva_g"][l].reshape(1, -1).astype(F32),
        "mla_wq": wq_full.astype(BF16), "mla_wq_sw": wq_sw.astype(BF16),
        "mla_wk": wkv3[:, :, :MLA_NOPE].reshape(MLA_KV_LORA, -1).astype(BF16),
        "mla_wv": wkv3[:, :, MLA_NOPE:].reshape(MLA_KV_LORA, -1).astype(BF16),
        "mla_gq": (jnp.concatenate([qk_g[0], zg]) * (MLA_QK ** -0.5 * LOG2E)).reshape(1, -1),
        "mla_gk": jnp.concatenate([qk_g[1], zg]).reshape(1, -1),
        "win_sink": p["win_sink"][l], "diff_lambda": p["diff_lambda"][l], "diff_subln_g": p["diff_subln_g"][l],
        "w_branch": cast_layer(p["w_branch"].reshape(DEPTH, N_BRANCH * BRANCH_W, D_MODEL), l).reshape(
            N_BRANCH, BRANCH_W, D_MODEL),
        "w_out": cast_layer(p["w_out"], l),
        "ln_mem_g": p["ln_mem_g"][l], "mem_ln_g": p["mem_ln_g"][l],
        "mem_wq": cast_layer(p["mem_wq"], l), "mem_wkv": cast_layer(p["mem_wkv"], l),
        "mem_q_gain": tile(p["mem_qk_g"][l, 0], MEM_HEADS) * (HEAD_DIM ** -0.5 * LOG2E),
        "mem_kv_gain": jnp.concatenate([tile(p["mem_qk_g"][l, 1], MEM_HEADS), jnp.ones((MEM_W,), F32)]),
        "mem_kv_flag": jnp.concatenate([jnp.ones((MEM_W,), F32), jnp.zeros((MEM_W,), F32)]),
        "mem_wo": cast_layer(p["mem_wo"], l),
        "ln_ffn_g": p["ln_ffn_g"][l],
        "ffn_w_in": cast_layer(p["ffn_w_in"], l), "ffn_w_out": cast_layer(p["ffn_w_out"], l),
    }


def rotary_tables(S):
    half = MLA_ROPE // 2
    inv_freq = ROPE_THETA ** (-jnp.arange(half, dtype=F32) / half)
    ang = jnp.arange(S, dtype=F32)[:, None] * inv_freq[None, :]
    cos, sin = jnp.cos(ang), jnp.sin(ang)
    z = jnp.zeros((S, LANES - MLA_ROPE), F32)
    return jnp.concatenate([cos, cos, z], axis=1), jnp.concatenate([-sin, sin, z], axis=1)


def _trunk(x, mem, layers, tables):
    B, S, D = x.shape
    M = B * S
    _, cols, _, _, _ = _mix_layout()
    ct, st = rotary_tables(S)
    xf = x.reshape(M, D)
    memf = mem.reshape(-1, D)
    dil_tq = _pick(S, (256,))
    for l, lw in enumerate(layers):
        h = rmsnorm(xf, lw["ln_mix_g"])
        mix = mm_segnorm(h, lw["w_mix"], lw["mix_gain"], lw["mix_flag"], tn_prefs=(MIX_TILE,))
        gates = mm_sigmoid(h, lw["w_gate"])

        q, k, v = mla_prep(mix, cols, S, ct, st, lw)
        o_mla = dense_attention(q, k, v, B, S, MLA_HEADS, MLA_QK_PAD, MLA_V)
        o_dil = band_attention(mix, cols["dq"], mix, cols["dk"], mix, cols["dv"], tables["dil_bias"],
                               jnp.full((DIL_HEADS,), NEG_INF, F32), B, S, DIL_HEADS, 1, DIL_HEADS // 2,
                               dil_tq, dil_tq, DIL_RADIUS)
        o_win = band_attention(mix, cols["wq"], mix, cols["wk"], mix, cols["wv"], tables["win_bias"],
                               lw["win_sink"], B, S, WIN_Q_HEADS, WIN_Q_HEADS // WIN_KV_HEADS, WIN_Q_HEADS,
                               2 * WIN_RADIUS, WIN_RADIUS, WIN_RADIUS)
        o_diff = diff_attention(mix, cols, tables["slopes_diff"], lw["diff_lambda"], lw["diff_subln_g"], B, S, l)

        merged = branch_merge((o_mla, o_dil, o_win, o_diff), lw["w_branch"], gates)
        xf = mm_residual(merged, lw["w_out"], xf)

        kvm = mm_segnorm(rmsnorm(memf, lw["mem_ln_g"]), lw["mem_wkv"], lw["mem_kv_gain"], lw["mem_kv_flag"])
        xf, hf = mem_attention(xf, kvm, lw, B, S)

        hid = mm_swiglu(hf, lw["ffn_w_in"], FFN_HIDDEN)
        xf = mm_residual_ksplit(hid, lw["ffn_w_out"], xf, nk=2, tm=_pick(M, (1024, 512, 256)), tn=256)
    return xf.reshape(B, S, D)


def kernel(x_prompt, x_sample, mem_prompt, mem_sample, ln_mix_g, w_in, mla_qa_g, mla_kva_g, mla_wq_up,
           mla_wkv_up, mla_qk_g, dil_qk_g, win_qk_g, win_sink, diff_qk_g, diff_lambda, diff_subln_g,
           w_branch, w_out, ln_mem_g, mem_ln_g, mem_wq, mem_wkv, mem_qk_g, mem_wo, ln_ffn_g, ffn_w_in,
           ffn_w_out):
    p = dict(ln_mix_g=ln_mix_g, w_in=w_in, mla_qa_g=mla_qa_g, mla_kva_g=mla_kva_g, mla_wq_up=mla_wq_up,
             mla_wkv_up=mla_wkv_up, mla_qk_g=mla_qk_g, dil_qk_g=dil_qk_g, win_qk_g=win_qk_g,
             win_sink=win_sink, diff_qk_g=diff_qk_g, diff_lambda=diff_lambda, diff_subln_g=diff_subln_g,
             w_branch=w_branch, w_out=w_out, ln_mem_g=ln_mem_g, mem_ln_g=mem_ln_g, mem_wq=mem_wq,
             mem_wkv=mem_wkv, mem_qk_g=mem_qk_g, mem_wo=mem_wo, ln_ffn_g=ln_ffn_g, ffn_w_in=ffn_w_in,
             ffn_w_out=ffn_w_out)
    layers = [pack_layer(p, l) for l in range(DEPTH)]
    tables = {
        "dil_bias": band_bias(alibi_slopes(DIL_HEADS), dil_multiplicity, 256, 256, DIL_RADIUS),
        "win_bias": band_bias(alibi_slopes(WIN_Q_HEADS), win_multiplicity, 2 * WIN_RADIUS, WIN_RADIUS,
                              WIN_RADIUS),
        "slopes_diff": alibi_slopes(DIFF_HEADS),
    }
    y_prompt = _trunk(x_prompt, mem_prompt, layers, tables)
    y_sample = _trunk(x_sample, mem_sample, layers, tables)
    return (y_prompt, y_sample)
```

```python
import functools
import math

import jax
import jax.numpy as jnp
import numpy as np
from jax import lax
from jax.experimental import pallas as pl
from jax.experimental.pallas import tpu as pltpu

F32 = jnp.float32
BF16 = jnp.bfloat16

D_MODEL = 4096
DEPTH = 2
EPS = 1e-6
NEG_INF = -1e30
N_BRANCH = 4
BRANCH_W = D_MODEL // N_BRANCH
HEAD_DIM = 128

MLA_NOPE = 128
MLA_ROPE = 64
MLA_V = 128
MLA_HEADS = BRANCH_W // MLA_V
MLA_Q_LORA = D_MODEL // 4
MLA_KV_LORA = D_MODEL // 8
MLA_QK = MLA_NOPE + MLA_ROPE
MLA_QK_PAD = 256
ROPE_THETA = 10000.0

DIL_HEADS = BRANCH_W // HEAD_DIM
DIL_PATTERNS = ((128, 1), (512, 4), (2048, 16))
DIL_RADIUS = max(w // 2 for w, _ in DIL_PATTERNS)

WIN_Q_HEADS = BRANCH_W // HEAD_DIM
WIN_KV_HEADS = WIN_Q_HEADS // 4
WIN_RADIUS = 128

DIFF_HD = 128
DIFF_HEADS = BRANCH_W // (2 * DIFF_HD)

MEM_HEADS = 4
MEM_W = MEM_HEADS * HEAD_DIM

FFN_HIDDEN = -(-8 * D_MODEL // (3 * 256)) * 256

MIX_SPLITS = (MLA_Q_LORA, MLA_KV_LORA, MLA_ROPE,
              BRANCH_W, BRANCH_W, BRANCH_W,
              BRANCH_W, WIN_KV_HEADS * HEAD_DIM, WIN_KV_HEADS * HEAD_DIM,
              BRANCH_W, BRANCH_W, BRANCH_W)
MIX_COLS = sum(MIX_SPLITS)
MIX_OFFSETS = tuple(int(o) for o in np.cumsum((0,) + MIX_SPLITS))

LOG2E = math.log2(math.e)
LANES = 128
MIX_TILE = 512
VMEM_LIMIT_BYTES = 56 * 1024 * 1024


def _pick(n, prefs):
    for p in prefs:
        if n % p == 0:
            return p
    raise ValueError(f"no tile in {prefs} divides {n}")


def _params(*sem):
    return pltpu.CompilerParams(dimension_semantics=sem, vmem_limit_bytes=VMEM_LIMIT_BYTES)


def _rmsnorm_kernel(x_ref, g_ref, o_ref):
    x = x_ref[...].astype(F32)
    ms = jnp.mean(x * x, axis=-1, keepdims=True)
    o_ref[...] = (x * lax.rsqrt(ms + EPS) * g_ref[...]).astype(o_ref.dtype)


def rmsnorm(x, g):
    M, D = x.shape
    tm = _pick(M, (512, 256, 128, 64, 8))
    return pl.pallas_call(
        _rmsnorm_kernel,
        grid=(M // tm,),
        in_specs=[pl.BlockSpec((tm, D), lambda i: (i, 0)),
                  pl.BlockSpec((1, D), lambda i: (0, 0))],
        out_specs=pl.BlockSpec((tm, D), lambda i: (i, 0)),
        out_shape=jax.ShapeDtypeStruct((M, D), BF16),
        compiler_params=_params("parallel"),
        name="rmsnorm",
    )(x, g.reshape(1, D).astype(F32))


def _dot(a, b):
    return jnp.dot(a, b, preferred_element_type=F32)


MXU_COLS = 256


def _sigmoid(x):
    return 0.5 * jnp.tanh(0.5 * x) + 0.5


def _col_slabs(n):
    w = MXU_COLS if n % MXU_COLS == 0 else n
    return [slice(c, c + w) for c in range(0, n, w)]


def _dot_nt(a, bt):
    return lax.dot_general(a, bt, (((1,), (1,)), ((), ())), preferred_element_type=F32)


def _mm_sigmoid_kernel(a_ref, bt_ref, o_ref):
    for cs in _col_slabs(o_ref.shape[1]):
        o_ref[:, cs] = _sigmoid(_dot_nt(a_ref[...], bt_ref[cs, :])).astype(o_ref.dtype)


def _mm_segnorm_kernel(a_ref, b_ref, g_ref, f_ref, o_ref, *, w_output_major):
    for cs in _col_slabs(o_ref.shape[1]):
        acc = _dot_nt(a_ref[...], b_ref[cs, :]) if w_output_major else _dot(a_ref[...], b_ref[:, cs])
        for c in range(0, acc.shape[1], LANES):
            seg = slice(cs.start + c, cs.start + c + LANES)
            y = acc[:, c:c + LANES]
            r = lax.rsqrt(jnp.mean(y * y, axis=-1, keepdims=True) + EPS)
            mult = jnp.where(f_ref[:, seg] > 0.0, r, 1.0) * g_ref[:, seg]
            o_ref[:, seg] = (y * mult).astype(o_ref.dtype)


def _mm_residual_kernel(a_ref, b_ref, r_ref, o_ref):
    for cs in _col_slabs(o_ref.shape[1]):
        o_ref[:, cs] = r_ref[:, cs] + _dot(a_ref[...], b_ref[:, cs])


def _mm_swiglu_kernel(a_ref, bg_ref, bu_ref, o_ref):
    half = a_ref.shape[0] // 2
    for rows in (slice(0, half), slice(half, 2 * half)):
        a = a_ref[rows, :]
        g = _dot(a, bg_ref[...])
        u = _dot(a, bu_ref[...])
        o_ref[rows, :] = (g * _sigmoid(g) * u).astype(o_ref.dtype)


def mm_sigmoid(a, wt, tm_prefs=(1024, 512, 256, 128), tn_prefs=(512, 256, 128)):
    M, K = a.shape
    N = wt.shape[0]
    tm, tn = _pick(M, tm_prefs), _pick(N, tn_prefs)
    return pl.pallas_call(
        _mm_sigmoid_kernel,
        grid=(M // tm, N // tn),
        in_specs=[pl.BlockSpec((tm, K), lambda i, j: (i, 0)),
                  pl.BlockSpec((tn, K), lambda i, j: (j, 0))],
        out_specs=pl.BlockSpec((tm, tn), lambda i, j: (i, j)),
        out_shape=jax.ShapeDtypeStruct((M, N), BF16),
        compiler_params=_params("parallel", "arbitrary"),
        name="mm_sigmoid",
    )(a, wt)


def mm_segnorm(a, w, gain, flag, w_output_major=False, tm_prefs=(1024, 512, 256, 128), tn_prefs=(512, 256, 128)):
    M, K = a.shape
    N = gain.shape[0]
    tm, tn = _pick(M, tm_prefs), _pick(N, tn_prefs)
    w_spec = (pl.BlockSpec((tn, K), lambda i, j: (j, 0)) if w_output_major
              else pl.BlockSpec((K, tn), lambda i, j: (0, j)))
    return pl.pallas_call(
        functools.partial(_mm_segnorm_kernel, w_output_major=w_output_major),
        grid=(M // tm, N // tn),
        in_specs=[pl.BlockSpec((tm, K), lambda i, j: (i, 0)),
                  w_spec,
                  pl.BlockSpec((1, tn), lambda i, j: (0, j)),
                  pl.BlockSpec((1, tn), lambda i, j: (0, j))],
        out_specs=pl.BlockSpec((tm, tn), lambda i, j: (i, j)),
        out_shape=jax.ShapeDtypeStruct((M, N), BF16),
        compiler_params=_params("parallel", "arbitrary"),
        name="mm_segnorm",
    )(a, w, gain.reshape(1, N).astype(F32), flag.reshape(1, N).astype(F32))


def mm_residual(a, w, res, tm_prefs=(1024, 512, 256, 128), tn_prefs=(512, 256, 128)):
    M, K = a.shape
    N = w.shape[1]
    tm, tn = _pick(M, tm_prefs), _pick(N, tn_prefs)
    return pl.pallas_call(
        _mm_residual_kernel,
        grid=(M // tm, N // tn),
        in_specs=[pl.BlockSpec((tm, K), lambda i, j: (i, 0)),
                  pl.BlockSpec((K, tn), lambda i, j: (0, j)),
                  pl.BlockSpec((tm, tn), lambda i, j: (i, j))],
        out_specs=pl.BlockSpec((tm, tn), lambda i, j: (i, j)),
        out_shape=jax.ShapeDtypeStruct((M, N), F32),
        compiler_params=_params("parallel", "arbitrary"),
        name="mm_residual",
    )(a, w, res)


def mm_swiglu(a, w, hidden):
    M, K = a.shape
    tm = _pick(M, (1024, 512, 256, 128))
    tn = _pick(hidden, (256, 128))
    nj = hidden // tn
    return pl.pallas_call(
        _mm_swiglu_kernel,
        grid=(M // tm, nj),
        in_specs=[pl.BlockSpec((tm, K), lambda i, j: (i, 0)),
                  pl.BlockSpec((K, tn), lambda i, j: (0, j)),
                  pl.BlockSpec((K, tn), lambda i, j: (0, j + nj))],
        out_specs=pl.BlockSpec((tm, tn), lambda i, j: (i, j)),
        out_shape=jax.ShapeDtypeStruct((M, hidden), BF16),
        compiler_params=_params("parallel", "arbitrary"),
        name="mm_swiglu",
    )(a, w, w)


def _branch_kernel(oa_ref, ob_ref, oc_ref, od_ref, w_ref, ga_ref, gb_ref, gc_ref, gd_ref, o_ref):
    for cs in _col_slabs(o_ref.shape[1]):
        acc = ga_ref[:, cs].astype(F32) * _dot(oa_ref[...], w_ref[0, :, cs])
        acc += gb_ref[:, cs].astype(F32) * _dot(ob_ref[...], w_ref[1, :, cs])
        acc += gc_ref[:, cs].astype(F32) * _dot(oc_ref[...], w_ref[2, :, cs])
        acc += gd_ref[:, cs].astype(F32) * _dot(od_ref[...], w_ref[3, :, cs])
        o_ref[:, cs] = acc.astype(o_ref.dtype)


def branch_merge(outs, w_branch, gates):
    M, W = outs[0].shape
    D = w_branch.shape[2]
    tm = _pick(M, (1024, 512, 256, 128))
    tn = _pick(D, (512, 256, 128))
    nj = D // tn
    o_spec = pl.BlockSpec((tm, W), lambda i, j: (i, 0))
    g_specs = [pl.BlockSpec((tm, tn), functools.partial(lambda i, j, b: (i, b * nj + j), b=b))
               for b in range(N_BRANCH)]
    return pl.pallas_call(
        _branch_kernel,
        grid=(M // tm, nj),
        in_specs=[o_spec] * N_BRANCH + [pl.BlockSpec((N_BRANCH, W, tn), lambda i, j: (0, 0, j))] + g_specs,
        out_specs=pl.BlockSpec((tm, tn), lambda i, j: (i, j)),
        out_shape=jax.ShapeDtypeStruct((M, D), BF16),
        compiler_params=_params("parallel", "arbitrary"),
        name="branch_merge",
    )(*outs, w_branch, gates, gates, gates, gates)


def _mla_prep_kernel(qa_ref, kva_ref, kpe_ref, ct_ref, st_ref, qag_ref, kvag_ref,
                     wq_ref, wqs_ref, wk_ref, wv_ref, gq_ref, gk_ref,
                     q_ref, k_ref, v_ref):
    def norm(x, g):
        return (x * lax.rsqrt(jnp.mean(x * x, axis=-1, keepdims=True) + EPS) * g).astype(BF16)

    ct = ct_ref[...]
    st = st_ref[...]
    qn = norm(qa_ref[...].astype(F32), qag_ref[...])
    kvn = norm(kva_ref[...].astype(F32), kvag_ref[...])
    qfull = _dot(qn, wq_ref[...])
    qsw = _dot(qn, wqs_ref[...])
    knope = _dot(kvn, wk_ref[...])
    vals = _dot(kvn, wv_ref[...])
    ones = jnp.ones((vals.shape[0], MLA_V), v_ref.dtype)
    for h in range(MLA_HEADS):
        v_ref[:, 2 * h * MLA_V:(2 * h + 1) * MLA_V] = vals[:, h * MLA_V:(h + 1) * MLA_V].astype(v_ref.dtype)
        v_ref[:, (2 * h + 1) * MLA_V:(2 * h + 2) * MLA_V] = ones

    kpe = kpe_ref[...].astype(F32)
    kpe_rot = kpe * ct + pltpu.roll(kpe, 2 * (MLA_ROPE // 2), 1) * st
    kpe_ss = jnp.sum(kpe_rot * kpe_rot, axis=-1, keepdims=True)
    gq_n, gq_r = gq_ref[:, :LANES], gq_ref[:, LANES:]
    gk_n, gk_r = gk_ref[:, :LANES], gk_ref[:, LANES:]
    for h in range(MLA_HEADS):
        lo = h * MLA_QK_PAD
        q_n = qfull[:, lo:lo + LANES]
        q_r = qfull[:, lo + LANES:lo + 2 * LANES] * ct + qsw[:, h * LANES:(h + 1) * LANES] * st
        ss = jnp.sum(q_n * q_n, axis=-1, keepdims=True) + jnp.sum(q_r * q_r, axis=-1, keepdims=True)
        r = lax.rsqrt(ss * (1.0 / MLA_QK) + EPS)
        q_ref[:, lo:lo + LANES] = (q_n * r * gq_n).astype(q_ref.dtype)
        q_ref[:, lo + LANES:lo + 2 * LANES] = (q_r * r * gq_r).astype(q_ref.dtype)
        k_n = knope[:, h * LANES:(h + 1) * LANES]
        ss = jnp.sum(k_n * k_n, axis=-1, keepdims=True) + kpe_ss
        r = lax.rsqrt(ss * (1.0 / MLA_QK) + EPS)
        k_ref[:, lo:lo + LANES] = (k_n * r * gk_n).astype(k_ref.dtype)
        k_ref[:, lo + LANES:lo + 2 * LANES] = (kpe_rot * r * gk_r).astype(k_ref.dtype)


def mla_prep(mix, cols, S, ct, st, lw):
    M = mix.shape[0]
    tm = _pick(S, (256, 128))
    ns = S // tm
    HQ = MLA_HEADS * MLA_QK_PAD
    HV = MLA_HEADS * 2 * MLA_V

    def col(width, off):
        assert off % width == 0
        return pl.BlockSpec((tm, width), lambda i: (i, off // width))

    def whole(a):
        return pl.BlockSpec(a.shape, lambda i: (0,) * a.ndim)

    consts = (lw["mla_qa_g"], lw["mla_kva_g"], lw["mla_wq"], lw["mla_wq_sw"], lw["mla_wk"], lw["mla_wv"],
              lw["mla_gq"], lw["mla_gk"])
    return pl.pallas_call(
        _mla_prep_kernel,
        grid=(M // tm,),
        in_specs=[col(MLA_Q_LORA, cols["qa"]), col(MLA_KV_LORA, cols["kva"]), col(LANES, cols["kpe"]),
                  pl.BlockSpec((tm, LANES), lambda i: (i % ns, 0)),
                  pl.BlockSpec((tm, LANES), lambda i: (i % ns, 0))] + [whole(c) for c in consts],
        out_specs=[pl.BlockSpec((tm, HQ), lambda i: (i, 0)),
                   pl.BlockSpec((tm, HQ), lambda i: (i, 0)),
                   pl.BlockSpec((tm, HV), lambda i: (i, 0))],
        out_shape=[jax.ShapeDtypeStruct((M, HQ), BF16),
                   jax.ShapeDtypeStruct((M, HQ), BF16),
                   jax.ShapeDtypeStruct((M, HV), BF16)],
        compiler_params=_params("parallel"),
        name="mla_prep",
    )(mix, mix, mix, ct, st, *consts)


def _qk(q, k):
    return lax.dot_general(q, k, (((1,), (1,)), ((), ())), preferred_element_type=F32)


def _dense_attn_kernel(q_ref, k_ref, v_ref, o_ref, *, n_sub, chunk):
    dv = o_ref.shape[1]
    ts = q_ref.shape[0] // n_sub
    seq = k_ref.shape[0]
    for t in range(n_sub):
        rows = slice(t * ts, (t + 1) * ts)
        q = q_ref[rows, :]
        pieces = [_qk(q, k_ref[c:c + chunk, :]) for c in range(0, seq, chunk)]
        mt = _lane_fold(pieces[0], jnp.maximum)
        for piece in pieces[1:]:
            mt = jnp.maximum(mt, _lane_fold(piece, jnp.maximum))
        m = jnp.max(mt, axis=-1, keepdims=True)
        o = jnp.zeros((ts, 2 * dv), F32)
        for c, piece in zip(range(0, seq, chunk), pieces):
            o = o + _dot(jnp.exp2(piece - m).astype(BF16), v_ref[c:c + chunk, :])
        o_ref[rows, :] = (o[:, :dv] / o[:, dv:]).astype(o_ref.dtype)


def dense_attention(q, k, v, B, S, H, dk, dv, n_sub=4):
    tq = _pick(S, (1024, 512, 256))
    chunk = _pick(S, (512, 256))
    q3, k3, v3 = (t.reshape(B, S, t.shape[1]) for t in (q, k, v))
    out = pl.pallas_call(
        functools.partial(_dense_attn_kernel, n_sub=n_sub, chunk=chunk),
        grid=(B, H, S // tq),
        in_specs=[pl.BlockSpec((None, tq, dk), lambda b, h, i: (b, i, h)),
                  pl.BlockSpec((None, S, dk), lambda b, h, i: (b, 0, h)),
                  pl.BlockSpec((None, S, 2 * dv), lambda b, h, i: (b, 0, h))],
        out_specs=pl.BlockSpec((None, tq, dv), lambda b, h, i: (b, i, h)),
        out_shape=jax.ShapeDtypeStruct((B, S, H * dv), BF16),
        compiler_params=_params("parallel", "parallel", "arbitrary"),
        name="dense_attention",
    )(q3, k3, v3)
    return out.reshape(B * S, H * dv)


def _lane_fold(x, op):
    out = x[:, :LANES]
    for j in range(1, x.shape[1] // LANES):
        out = op(out, x[:, j * LANES:(j + 1) * LANES])
    return out


def _band_kernel(sink_ref, q_ref, k_ref, v_ref, bias_ref, o_ref, *, tq, tk, radius, seq, n_chunks, hpb, rep):
    hb = pl.program_id(1)
    q0 = pl.program_id(2) * tq
    starts, tiles = [], []
    for c in range(n_chunks):
        start = q0 - radius + c * tk
        inside = jnp.logical_and(start >= 0, start + tk <= seq)
        starts.append(pl.multiple_of(jnp.clip(start, 0, seq - tk), tk))
        tiles.append(jnp.where(inside, c, n_chunks))
    for hh in range(hpb):
        qs = slice(hh * HEAD_DIM, (hh + 1) * HEAD_DIM)
        ks = slice((hh // rep) * HEAD_DIM, (hh // rep + 1) * HEAD_DIM)
        q = q_ref[:, qs]
        scores = [_qk(q, k_ref[pl.ds(sc, tk), ks]) + bias_ref[hh, t] for sc, t in zip(starts, tiles)]
        mt = _lane_fold(scores[0], jnp.maximum)
        for s in scores[1:]:
            mt = jnp.maximum(mt, _lane_fold(s, jnp.maximum))
        sink = sink_ref[hb * hpb + hh] * LOG2E
        m = jnp.maximum(jnp.max(mt, axis=-1, keepdims=True), sink)
        lt = jnp.zeros((tq, LANES), F32)
        acc = jnp.zeros((tq, HEAD_DIM), F32)
        for s, sc in zip(scores, starts):
            p = jnp.exp2(s - m)
            lt = lt + _lane_fold(p, jnp.add)
            acc = acc + _dot(p.astype(BF16), v_ref[pl.ds(sc, tk), ks])
        l = jnp.sum(lt, axis=-1, keepdims=True) + jnp.exp2(sink - m)
        o_ref[:, qs] = (acc / l).astype(o_ref.dtype)


def band_bias(slopes, mult_fn, tq, tk, radius):
    n_chunks = (tq + 2 * radius) // tk
    a = jnp.arange(tq)[None, :, None]
    c = jnp.arange(tk)[None, None, :]
    delta = (jnp.arange(n_chunks)[:, None, None] * tk - radius) + c - a
    mult = mult_fn(delta)
    dist = jnp.abs(delta).astype(F32)
    logm = jnp.log2(jnp.maximum(mult, 1).astype(F32))
    bias = logm[None] - (slopes * LOG2E)[:, None, None, None] * dist[None]
    bias = jnp.where(mult[None] > 0, bias, NEG_INF)
    dead = jnp.full((slopes.shape[0], 1, tq, tk), NEG_INF, F32)
    return jnp.concatenate([bias, dead], axis=1)


def dil_multiplicity(delta):
    m = jnp.zeros(delta.shape, jnp.int32)
    for window, dil in DIL_PATTERNS:
        m = m + ((delta % dil == 0) & (jnp.abs(delta) <= window // 2)).astype(jnp.int32)
    return m


def win_multiplicity(delta):
    return (jnp.abs(delta) <= WIN_RADIUS).astype(jnp.int32)


def band_attention(qarr, q_off, karr, k_off, varr, v_off, bias, sinks, B, S, H, rep, hpb, tq, tk, radius):
    n_chunks = bias.shape[1] - 1
    assert radius % tk == 0 and tq % tk == 0 and S % tq == 0 and S >= tk and hpb % rep == 0 and H % hpb == 0
    wq, wk = hpb * HEAD_DIM, (hpb // rep) * HEAD_DIM
    assert q_off % wq == 0 and k_off % wk == 0 and v_off % wk == 0
    qb, kb, vb = q_off // wq, k_off // wk, v_off // wk
    q3, k3, v3 = (t.reshape(B, S, t.shape[1]) for t in (qarr, karr, varr))
    kern = functools.partial(_band_kernel, tq=tq, tk=tk, radius=radius, seq=S, n_chunks=n_chunks, hpb=hpb, rep=rep)
    out = pl.pallas_call(
        kern,
        grid=(B, H // hpb, S // tq),
        in_specs=[pl.BlockSpec(memory_space=pltpu.SMEM),
                  pl.BlockSpec((None, tq, wq), lambda b, h, i: (b, i, qb + h)),
                  pl.BlockSpec((None, S, wk), lambda b, h, i: (b, 0, kb + h)),
                  pl.BlockSpec((None, S, wk), lambda b, h, i: (b, 0, vb + h)),
                  pl.BlockSpec((hpb, n_chunks + 1, tq, tk), lambda b, h, i: (h, 0, 0, 0))],
        out_specs=pl.BlockSpec((None, tq, wq), lambda b, h, i: (b, i, h)),
        out_shape=jax.ShapeDtypeStruct((B, S, H * HEAD_DIM), BF16),
        compiler_params=_params("parallel", "parallel", "arbitrary"),
        name="band_attention",
    )(sinks.astype(F32), q3, k3, v3, bias)
    return out.reshape(B * S, H * HEAD_DIM)


def _diff_kernel(slope_ref, lam_ref, q_ref, k_ref, v_ref, pos_ref, g_ref, o_ref, *, tq, lam_init, chunk, n_sub):
    h = pl.program_id(1)
    q0 = pl.program_id(2) * tq
    lp = lam_ref[...]
    lam = (jnp.exp(jnp.sum(lp[0:1] * lp[1:2], axis=-1, keepdims=True))
           - jnp.exp(jnp.sum(lp[2:3] * lp[3:4], axis=-1, keepdims=True)) + lam_init)
    slope = slope_ref[h]
    seq = k_ref.shape[0]
    kpos = pos_ref[...] * slope
    ts = tq // n_sub
    for t in range(n_sub):
        rows = slice(t * ts, (t + 1) * ts)
        qrow = (q0 + t * ts + lax.broadcasted_iota(jnp.int32, (ts, LANES), 0)).astype(F32) * slope
        qpos = jnp.concatenate([qrow] * (chunk // LANES), axis=1)
        q1, q2 = q_ref[rows, :DIFF_HD], q_ref[rows, DIFF_HD:]
        t1, t2, mt1, mt2 = [], [], None, None
        for c in range(0, seq, chunk):
            b = jnp.abs(kpos[:, c:c + chunk] - qpos)
            a1 = _qk(q1, k_ref[c:c + chunk, :DIFF_HD]) - b
            a2 = _qk(q2, k_ref[c:c + chunk, DIFF_HD:]) - b
            t1.append(a1)
            t2.append(a2)
            f1, f2 = _lane_fold(a1, jnp.maximum), _lane_fold(a2, jnp.maximum)
            mt1 = f1 if mt1 is None else jnp.maximum(mt1, f1)
            mt2 = f2 if mt2 is None else jnp.maximum(mt2, f2)
        m1 = jnp.max(mt1, axis=-1, keepdims=True)
        m2 = jnp.max(mt2, axis=-1, keepdims=True)
        lt1 = jnp.zeros((ts, LANES), F32)
        lt2 = jnp.zeros((ts, LANES), F32)
        o1 = jnp.zeros((ts, 2 * DIFF_HD), F32)
        o2 = jnp.zeros((ts, 2 * DIFF_HD), F32)
        for i, c in enumerate(range(0, seq, chunk)):
            p1, p2 = jnp.exp2(t1[i] - m1), jnp.exp2(t2[i] - m2)
            lt1 = lt1 + _lane_fold(p1, jnp.add)
            lt2 = lt2 + _lane_fold(p2, jnp.add)
            o1 = o1 + _dot(p1.astype(BF16), v_ref[c:c + chunk, :])
            o2 = o2 + _dot(p2.astype(BF16), v_ref[c:c + chunk, :])
        l1 = jnp.sum(lt1, axis=-1, keepdims=True)
        l2 = jnp.sum(lt2, axis=-1, keepdims=True)
        o = o1 * (1.0 / l1) - o2 * (lam / l2)
        r = lax.rsqrt(jnp.mean(o * o, axis=-1, keepdims=True) + EPS)
        o_ref[rows, :] = (o * r * g_ref[...] * (1.0 - lam_init)).astype(o_ref.dtype)


def diff_attention(mix, cols, slopes, lam_p, subln_g, B, S, layer):
    tq = _pick(S, (512, 256, 128))
    lam_init = 0.8 - 0.6 * math.exp(-0.3 * layer)
    W = 2 * DIFF_HD
    qb, kb, vb = (cols[n] // W for n in ("fq", "fk", "fv"))
    m3 = mix.reshape(B, S, mix.shape[1])
    pos = jnp.arange(S, dtype=F32).reshape(1, S)
    kern = functools.partial(_diff_kernel, tq=tq, lam_init=lam_init, chunk=_pick(S, (512, 256)),
                             n_sub=2 if S > 2048 else 4)
    out = pl.pallas_call(
        kern,
        grid=(B, DIFF_HEADS, S // tq),
        in_specs=[pl.BlockSpec(memory_space=pltpu.SMEM),
                  pl.BlockSpec((4, DIFF_HD), lambda b, h, i: (0, 0)),
                  pl.BlockSpec((None, tq, W), lambda b, h, i: (b, i, qb + h)),
                  pl.BlockSpec((None, S, W), lambda b, h, i: (b, 0, kb + h)),
                  pl.BlockSpec((None, S, W), lambda b, h, i: (b, 0, vb + h)),
                  pl.BlockSpec((1, S), lambda b, h, i: (0, 0)),
                  pl.BlockSpec((1, W), lambda b, h, i: (0, 0))],
        out_specs=pl.BlockSpec((None, tq, W), lambda b, h, i: (b, i, h)),
        out_shape=jax.ShapeDtypeStruct((B, S, DIFF_HEADS * W), BF16),
        compiler_params=_params("parallel", "parallel", "arbitrary"),
        name="diff_attention",
    )((slopes * LOG2E).astype(F32), lam_p.astype(F32), m3, m3, m3, pos, subln_g.reshape(1, W).astype(F32))
    return out.reshape(B * S, DIFF_HEADS * W)


def _mem_attn_kernel(x_ref, gm_ref, wq_ref, gq_ref, kv_ref, wo_ref, gf_ref, o_ref, hf_ref):
    def rms(t):
        return lax.rsqrt(jnp.mean(t * t, axis=-1, keepdims=True) + EPS)

    x = x_ref[...]
    hm = (x * rms(x) * gm_ref[...]).astype(BF16)
    qacc = _dot(hm, wq_ref[...])
    heads = []
    for h in range(MEM_HEADS):
        seg = slice(h * HEAD_DIM, (h + 1) * HEAD_DIM)
        y = qacc[:, seg]
        q = (y * rms(y) * gq_ref[:, seg]).astype(BF16)
        s = _qk(q, kv_ref[:, seg])
        p = jnp.exp2(s - jnp.max(s, axis=-1, keepdims=True))
        l = jnp.sum(p, axis=-1, keepdims=True)
        vseg = slice(MEM_W + h * HEAD_DIM, MEM_W + (h + 1) * HEAD_DIM)
        heads.append((_dot(p.astype(BF16), kv_ref[:, vseg]) / l).astype(BF16))
    x2 = x + _dot(jnp.concatenate(heads, axis=-1), wo_ref[...])
    o_ref[...] = x2
    hf_ref[...] = (x2 * rms(x2) * gf_ref[...]).astype(hf_ref.dtype)


def mem_attention(x, kv, lw, B, S):
    D = x.shape[1]
    Mt = kv.shape[0] // B
    tq = _pick(S, (256, 128))
    kv3 = kv.reshape(B, Mt, 2 * MEM_W)
    x3 = x.reshape(B, S, D)

    def row(n):
        return pl.BlockSpec((1, n), lambda b, i: (0, 0))

    out, hf = pl.pallas_call(
        _mem_attn_kernel,
        grid=(B, S // tq),
        in_specs=[pl.BlockSpec((None, tq, D), lambda b, i: (b, i, 0)),
                  row(D),
                  pl.BlockSpec((D, MEM_W), lambda b, i: (0, 0)),
                  row(MEM_W),
                  pl.BlockSpec((None, Mt, 2 * MEM_W), lambda b, i: (b, 0, 0)),
                  pl.BlockSpec((MEM_W, D), lambda b, i: (0, 0)),
                  row(D)],
        out_specs=[pl.BlockSpec((None, tq, D), lambda b, i: (b, i, 0)),
                   pl.BlockSpec((None, tq, D), lambda b, i: (b, i, 0))],
        out_shape=[jax.ShapeDtypeStruct((B, S, D), F32), jax.ShapeDtypeStruct((B, S, D), BF16)],
        compiler_params=_params("parallel", "arbitrary"),
        name="mem_attention",
    )(x3, lw["ln_mem_g"].reshape(1, D).astype(F32), lw["mem_wq"], lw["mem_q_gain"].reshape(1, MEM_W),
      kv3, lw["mem_wo"], lw["ln_ffn_g"].reshape(1, D).astype(F32))
    return out.reshape(B * S, D), hf.reshape(B * S, D)


def alibi_slopes(n):
    return 2.0 ** (-8.0 * jnp.arange(1, n + 1, dtype=F32) / n)


MIX_NAMES = ("qa", "kva", "kpe", "dq", "dk", "dv", "wq", "wk", "wv", "fq", "fk", "fv")


def _mix_layout():
    src = {n: (MIX_OFFSETS[i], MIX_OFFSETS[i + 1]) for i, n in enumerate(MIX_NAMES)}
    head = -(-(src["kpe"][0] + LANES) // MIX_TILE) * MIX_TILE
    shift = head - src["dq"][0]
    cols = {n: src[n][0] + (0 if n in ("qa", "kva", "kpe") else shift) for n in MIX_NAMES}
    total = -(-(MIX_COLS + shift) // MIX_TILE) * MIX_TILE
    return src, cols, head, shift, total


def _pack_kernel(src3_ref, o_ref, *, n_head, kpe_off, tn):
    src_ref = src3_ref.at[0]
    j = pl.program_id(0)
    if not n_head:
        o_ref[...] = src_ref[...].astype(o_ref.dtype)
        return

    @pl.when(j != n_head - 1)
    def _():
        o_ref[...] = src_ref[...].astype(o_ref.dtype)

    @pl.when(j == n_head - 1)
    def _():
        quarter = MLA_ROPE // 2
        x1 = src_ref[kpe_off:kpe_off + quarter, :].astype(o_ref.dtype)
        x2 = src_ref[kpe_off + quarter:kpe_off + 2 * quarter, :].astype(o_ref.dtype)
        if kpe_off:
            o_ref[:kpe_off, :] = src_ref[:kpe_off, :].astype(o_ref.dtype)
        for t, piece in enumerate((x1, x2, x2, x1)):
            o_ref[kpe_off + t * quarter:kpe_off + (t + 1) * quarter, :] = piece
        if kpe_off + LANES < tn:
            o_ref[kpe_off + LANES:, :] = jnp.zeros((tn - kpe_off - LANES, o_ref.shape[1]), o_ref.dtype)


def _cast_kernel(x_ref, o_ref):
    o_ref[...] = x_ref[...].astype(o_ref.dtype)


def cast_layer(w, l):
    _, R, C = w.shape
    tc = _pick(C, (2048, 1024, 512, 256, 128))
    tr = _pick(R, tuple(t for t in (2048, 1024, 512, 256, 128, 64, 8) if t * tc <= 2 ** 21))
    return pl.pallas_call(
        _cast_kernel,
        grid=(R // tr, C // tc),
        in_specs=[pl.BlockSpec((None, tr, tc), lambda i, j: (l, i, j))],
        out_specs=pl.BlockSpec((tr, tc), lambda i, j: (i, j)),
        out_shape=jax.ShapeDtypeStruct((R, C), BF16),
        compiler_params=_params("parallel", "arbitrary"),
        name="cast_layer",
    )(w)


def pack_shifted(src_t, l, n_out, shift, head_rows=0, kpe_row=0):
    _, C, K = src_t.shape
    tn = MIX_TILE
    n_head = head_rows // tn
    assert shift % 8 == 0 and head_rows % tn == 0 and (n_head == 0 or kpe_row // tn == n_head - 1)
    kern = functools.partial(_pack_kernel, n_head=n_head, kpe_off=kpe_row % tn, tn=tn)
    return pl.pallas_call(
        kern,
        grid=(n_out // tn,),
        in_specs=[pl.BlockSpec((pl.Element(1), pl.Element(tn), pl.Element(K)),
                               lambda j: (l, pl.multiple_of(jnp.where(j < n_head, j * tn, j * tn - shift), 8), 0))],
        out_specs=pl.BlockSpec((tn, K), lambda j: (j, 0)),
        out_shape=jax.ShapeDtypeStruct((n_out, K), BF16),
        compiler_params=_params("arbitrary"),
        name="pack_shifted",
    )(src_t)


def pack_layer(p, l):
    src, cols, head_w, shift, total = _mix_layout()
    half = MLA_ROPE // 2
    qscale = HEAD_DIM ** -0.5 * LOG2E

    def tile(g, n):
        return jnp.tile(g.astype(F32), n)

    norm_gain = {
        "dq": tile(p["dil_qk_g"][l, 0], DIL_HEADS) * qscale, "dk": tile(p["dil_qk_g"][l, 1], DIL_HEADS),
        "fq": tile(p["diff_qk_g"][l, 0], 2 * DIFF_HEADS) * (DIFF_HD ** -0.5 * LOG2E),
        "fk": tile(p["diff_qk_g"][l, 1], 2 * DIFF_HEADS),
        "wq": tile(p["win_qk_g"][l, 0], WIN_Q_HEADS) * qscale, "wk": tile(p["win_qk_g"][l, 1], WIN_KV_HEADS),
    }
    gain = jnp.ones((total,), F32)
    flag = jnp.zeros((total,), F32)
    for n, g in norm_gain.items():
        gain = gain.at[cols[n]:cols[n] + g.shape[0]].set(g)
        flag = flag.at[cols[n]:cols[n] + g.shape[0]].set(1.0)

    w_in_t = jnp.swapaxes(p["w_in"], 1, 2)
    w_mix = pack_shifted(w_in_t, l, total, shift, head_rows=head_w, kpe_row=src["kpe"][0])
    w_gate = pack_shifted(w_in_t, l, N_BRANCH * D_MODEL, -MIX_COLS)

    wq3 = p["mla_wq_up"][l].reshape(MLA_Q_LORA, MLA_HEADS, MLA_QK)
    nope, x1, x2 = wq3[:, :, :MLA_NOPE], wq3[:, :, MLA_NOPE:MLA_NOPE + half], wq3[:, :, MLA_NOPE + half:]
    z = jnp.zeros((MLA_Q_LORA, MLA_HEADS, LANES - MLA_ROPE), F32)
    wq_full = jnp.concatenate([nope, x1, x2, z], axis=-1).reshape(MLA_Q_LORA, MLA_HEADS * MLA_QK_PAD)
    wq_sw = jnp.concatenate([x2, x1, z], axis=-1).reshape(MLA_Q_LORA, MLA_HEADS * LANES)
    wkv3 = p["mla_wkv_up"][l].reshape(MLA_KV_LORA, MLA_HEADS, MLA_NOPE + MLA_V)
    zg = jnp.zeros((LANES - MLA_ROPE,), F32)
    qk_g = p["mla_qk_g"][l].astype(F32)

    return {
        "ln_mix_g": p["ln_mix_g"][l],
        "w_mix": w_mix, "w_gate": w_gate, "mix_gain": gain, "mix_flag": flag,
        "mla_qa_g": p["mla_qa_g"][l].reshape(1, -1).astype(F32),
        "mla_kva_g": p["mla_kva_g"][l].reshape(1, -1).astype(F32),
        "mla_wq": wq_full.astype(BF16), "mla_wq_sw": wq_sw.astype(BF16),
        "mla_wk": wkv3[:, :, :MLA_NOPE].reshape(MLA_KV_LORA, -1).astype(BF16),
        "mla_wv": wkv3[:, :, MLA_NOPE:].reshape(MLA_KV_LORA, -1).astype(BF16),
        "mla_gq": (jnp.concatenate([qk_g[0], zg]) * (MLA_QK ** -0.5 * LOG2E)).reshape(1, -1),
        "mla_gk": jnp.concatenate([qk_g[1], zg]).reshape(1, -1),
        "win_sink": p["win_sink"][l], "diff_lambda": p["diff_lambda"][l], "diff_subln_g": p["diff_subln_g"][l],
        "w_branch": cast_layer(p["w_branch"].reshape(DEPTH, N_BRANCH * BRANCH_W, D_MODEL), l).reshape(
            N_BRANCH, BRANCH_W, D_MODEL),
        "w_out": cast_layer(p["w_out"], l),
        "ln_mem_g": p["ln_mem_g"][l], "mem_ln_g": p["mem_ln_g"][l],
        "mem_wq": cast_layer(p["mem_wq"], l), "mem_wkv": cast_layer(p["mem_wkv"], l),
        "mem_q_gain": tile(p["mem_qk_g"][l, 0], MEM_HEADS) * (HEAD_DIM ** -0.5 * LOG2E),
        "mem_kv_gain": jnp.concatenate([tile(p["mem_qk_g"][l, 1], MEM_HEADS), jnp.ones((MEM_W,), F32)]),
        "mem_kv_flag": jnp.concatenate([jnp.ones((MEM_W,), F32), jnp.zeros((MEM_W,), F32)]),
        "mem_wo": cast_layer(p["mem_wo"], l),
        "ln_ffn_g": p["ln_ffn_g"][l],
        "ffn_w_in": cast_layer(p["ffn_w_in"], l), "ffn_w_out": cast_layer(p["ffn_w_out"], l),
    }


def rotary_tables(S):
    half = MLA_ROPE // 2
    inv_freq = ROPE_THETA ** (-jnp.arange(half, dtype=F32) / half)
    ang = jnp.arange(S, dtype=F32)[:, None] * inv_freq[None, :]
    cos, sin = jnp.cos(ang), jnp.sin(ang)
    z = jnp.zeros((S, LANES - MLA_ROPE), F32)
    return jnp.concatenate([cos, cos, z], axis=1), jnp.concatenate([-sin, sin, z], axis=1)


def _trunk(x, mem, layers, tables):
    B, S, D = x.shape
    M = B * S
    _, cols, _, _, _ = _mix_layout()
    ct, st = rotary_tables(S)
    xf = x.reshape(M, D)
    memf = mem.reshape(-1, D)
    dil_tq = _pick(S, (256,))
    for l, lw in enumerate(layers):
        h = rmsnorm(xf, lw["ln_mix_g"])
        mix = mm_segnorm(h, lw["w_mix"], lw["mix_gain"], lw["mix_flag"], w_output_major=True,
                         tn_prefs=(MIX_TILE,))
        gates = mm_sigmoid(h, lw["w_gate"])

        q, k, v = mla_prep(mix, cols, S, ct, st, lw)
        o_mla = dense_attention(q, k, v, B, S, MLA_HEADS, MLA_QK_PAD, MLA_V)
        o_dil = band_attention(mix, cols["dq"], mix, cols["dk"], mix, cols["dv"], tables["dil_bias"],
                               jnp.full((DIL_HEADS,), NEG_INF, F32), B, S, DIL_HEADS, 1, DIL_HEADS // 2,
                               dil_tq, dil_tq, DIL_RADIUS)
        o_win = band_attention(mix, cols["wq"], mix, cols["wk"], mix, cols["wv"], tables["win_bias"],
                               lw["win_sink"], B, S, WIN_Q_HEADS, WIN_Q_HEADS // WIN_KV_HEADS, WIN_Q_HEADS,
                               2 * WIN_RADIUS, WIN_RADIUS, WIN_RADIUS)
        o_diff = diff_attention(mix, cols, tables["slopes_diff"], lw["diff_lambda"], lw["diff_subln_g"], B, S, l)

        merged = branch_merge((o_mla, o_dil, o_win, o_diff), lw["w_branch"], gates)
        xf = mm_residual(merged, lw["w_out"], xf)

        kvm = mm_segnorm(rmsnorm(memf, lw["mem_ln_g"]), lw["mem_wkv"], lw["mem_kv_gain"], lw["mem_kv_flag"])
        xf, hf = mem_attention(xf, kvm, lw, B, S)

        hid = mm_swiglu(hf, lw["ffn_w_in"], FFN_HIDDEN)
        xf = mm_residual(hid, lw["ffn_w_out"], xf, tm_prefs=(512, 256, 128), tn_prefs=(256, 128))
    return xf.reshape(B, S, D)


def kernel(x_prompt, x_sample, mem_prompt, mem_sample, ln_mix_g, w_in, mla_qa_g, mla_kva_g, mla_wq_up,
           mla_wkv_up, mla_qk_g, dil_qk_g, win_qk_g, win_sink, diff_qk_g, diff_lambda, diff_subln_g,
           w_branch, w_out, ln_mem_g, mem_ln_g, mem_wq, mem_wkv, mem_qk_g, mem_wo, ln_ffn_g, ffn_w_in,
           ffn_w_out):
    p = dict(ln_mix_g=ln_mix_g, w_in=w_in, mla_qa_g=mla_qa_g, mla_kva_g=mla_kva_g, mla_wq_up=mla_wq_up,
             mla_wkv_up=mla_wkv_up, mla_qk_g=mla_qk_g, dil_qk_g=dil_qk_g, win_qk_g=win_qk_g,
             win_sink=win_sink, diff_qk_g=diff_qk_g, diff_lambda=diff_lambda, diff_subln_g=diff_subln_g,
             w_branch=w_branch, w_out=w_out, ln_mem_g=ln_mem_g, mem_ln_g=mem_ln_g, mem_wq=mem_wq,
             mem_wkv=mem_wkv, mem_qk_g=mem_qk_g, mem_wo=mem_wo, ln_ffn_g=ln_ffn_g, ffn_w_in=ffn_w_in,
             ffn_w_out=ffn_w_out)
    layers = [pack_layer(p, l) for l in range(DEPTH)]
    tables = {
        "dil_bias": band_bias(alibi_slopes(DIL_HEADS), dil_multiplicity, 256, 256, DIL_RADIUS),
        "win_bias": band_bias(alibi_slopes(WIN_Q_HEADS), win_multiplicity, 2 * WIN_RADIUS, WIN_RADIUS,
                              WIN_RADIUS),
        "slopes_diff": alibi_slopes(DIFF_HEADS),
    }
    y_prompt = _trunk(x_prompt, mem_prompt, layers, tables)
    y_sample = _trunk(x_sample, mem_sample, layers, tables)
    return (y_prompt, y_sample)
```

```python
import functools
import math

import jax
import jax.numpy as jnp
import numpy as np
from jax import lax
from jax.experimental import pallas as pl
from jax.experimental.pallas import tpu as pltpu

F32 = jnp.float32
BF16 = jnp.bfloat16

D_MODEL = 4096
DEPTH = 2
EPS = 1e-6
NEG_INF = -1e30
N_BRANCH = 4
BRANCH_W = D_MODEL // N_BRANCH
HEAD_DIM = 128

MLA_NOPE = 128
MLA_ROPE = 64
MLA_V = 128
MLA_HEADS = BRANCH_W // MLA_V
MLA_Q_LORA = D_MODEL // 4
MLA_KV_LORA = D_MODEL // 8
MLA_QK = MLA_NOPE + MLA_ROPE
MLA_QK_PAD = 256
ROPE_THETA = 10000.0

DIL_HEADS = BRANCH_W // HEAD_DIM
DIL_PATTERNS = ((128, 1), (512, 4), (2048, 16))
DIL_RADIUS = max(w // 2 for w, _ in DIL_PATTERNS)

WIN_Q_HEADS = BRANCH_W // HEAD_DIM
WIN_KV_HEADS = WIN_Q_HEADS // 4
WIN_RADIUS = 128

DIFF_HD = 128
DIFF_HEADS = BRANCH_W // (2 * DIFF_HD)

MEM_HEADS = 4
MEM_W = MEM_HEADS * HEAD_DIM

FFN_HIDDEN = -(-8 * D_MODEL // (3 * 256)) * 256

MIX_SPLITS = (MLA_Q_LORA, MLA_KV_LORA, MLA_ROPE,
              BRANCH_W, BRANCH_W, BRANCH_W,
              BRANCH_W, WIN_KV_HEADS * HEAD_DIM, WIN_KV_HEADS * HEAD_DIM,
              BRANCH_W, BRANCH_W, BRANCH_W)
MIX_COLS = sum(MIX_SPLITS)
MIX_OFFSETS = tuple(int(o) for o in np.cumsum((0,) + MIX_SPLITS))

LOG2E = math.log2(math.e)
LANES = 128
MIX_TILE = 512
VMEM_LIMIT_BYTES = 56 * 1024 * 1024


def _pick(n, prefs):
    for p in prefs:
        if n % p == 0:
            return p
    raise ValueError(f"no tile in {prefs} divides {n}")


def _params(*sem):
    return pltpu.CompilerParams(dimension_semantics=sem, vmem_limit_bytes=VMEM_LIMIT_BYTES)


def _rmsnorm_kernel(x_ref, g_ref, o_ref):
    x = x_ref[...].astype(F32)
    ms = jnp.mean(x * x, axis=-1, keepdims=True)
    o_ref[...] = (x * lax.rsqrt(ms + EPS) * g_ref[...]).astype(o_ref.dtype)


def rmsnorm(x, g):
    M, D = x.shape
    tm = _pick(M, (512, 256, 128, 64, 8))
    return pl.pallas_call(
        _rmsnorm_kernel,
        grid=(M // tm,),
        in_specs=[pl.BlockSpec((tm, D), lambda i: (i, 0)),
                  pl.BlockSpec((1, D), lambda i: (0, 0))],
        out_specs=pl.BlockSpec((tm, D), lambda i: (i, 0)),
        out_shape=jax.ShapeDtypeStruct((M, D), BF16),
        compiler_params=_params("parallel"),
        name="rmsnorm",
    )(x, g.reshape(1, D).astype(F32))


def _dot(a, b):
    return jnp.dot(a, b, preferred_element_type=F32)


MXU_COLS = 256


def _sigmoid(x):
    return 0.5 * jnp.tanh(0.5 * x) + 0.5


def _col_slabs(n):
    w = MXU_COLS if n % MXU_COLS == 0 else n
    return [slice(c, c + w) for c in range(0, n, w)]


def _dot_nt(a, bt):
    return lax.dot_general(a, bt, (((1,), (1,)), ((), ())), preferred_element_type=F32)


def _mm_sigmoid_kernel(a_ref, bt_ref, o_ref):
    for cs in _col_slabs(o_ref.shape[1]):
        o_ref[:, cs] = _sigmoid(_dot_nt(a_ref[...], bt_ref[cs, :])).astype(o_ref.dtype)


def _mm_segnorm_kernel(tile_norm_ref, a_ref, b_ref, g_ref, f_ref, o_ref, *, w_output_major):
    def slab(cs):
        return _dot_nt(a_ref[...], b_ref[cs, :]) if w_output_major else _dot(a_ref[...], b_ref[:, cs])

    has_norm = tile_norm_ref[pl.program_id(1)] > 0

    @pl.when(has_norm)
    def _():
        for cs in _col_slabs(o_ref.shape[1]):
            acc = slab(cs)
            for c in range(0, acc.shape[1], LANES):
                seg = slice(cs.start + c, cs.start + c + LANES)
                y = acc[:, c:c + LANES]
                r = lax.rsqrt(jnp.mean(y * y, axis=-1, keepdims=True) + EPS)
                mult = jnp.where(f_ref[:, seg] > 0.0, r, 1.0) * g_ref[:, seg]
                o_ref[:, seg] = (y * mult).astype(o_ref.dtype)

    @pl.when(jnp.logical_not(has_norm))
    def _():
        for cs in _col_slabs(o_ref.shape[1]):
            o_ref[:, cs] = (slab(cs) * g_ref[:, cs]).astype(o_ref.dtype)


def _mm_residual_kernel(a_ref, b_ref, r_ref, o_ref):
    for cs in _col_slabs(o_ref.shape[1]):
        o_ref[:, cs] = r_ref[:, cs] + _dot(a_ref[...], b_ref[:, cs])


def _mm_swiglu_kernel(a_ref, bg_ref, bu_ref, o_ref):
    half = a_ref.shape[0] // 2
    for rows in (slice(0, half), slice(half, 2 * half)):
        a = a_ref[rows, :]
        g = _dot(a, bg_ref[...])
        u = _dot(a, bu_ref[...])
        o_ref[rows, :] = (g * _sigmoid(g) * u).astype(o_ref.dtype)


def mm_sigmoid(a, wt, tm_prefs=(2048, 1024, 512, 256, 128), tn_prefs=(512, 256, 128)):
    M, K = a.shape
    N = wt.shape[0]
    tm, tn = _pick(M, tm_prefs), _pick(N, tn_prefs)
    return pl.pallas_call(
        _mm_sigmoid_kernel,
        grid=(M // tm, N // tn),
        in_specs=[pl.BlockSpec((tm, K), lambda i, j: (i, 0)),
                  pl.BlockSpec((tn, K), lambda i, j: (j, 0))],
        out_specs=pl.BlockSpec((tm, tn), lambda i, j: (i, j)),
        out_shape=jax.ShapeDtypeStruct((M, N), BF16),
        compiler_params=_params("parallel", "arbitrary"),
        name="mm_sigmoid",
    )(a, wt)


def mm_segnorm(a, w, gain, flag, w_output_major=False, tm_prefs=(1024, 512, 256, 128), tn_prefs=(512, 256, 128)):
    M, K = a.shape
    N = gain.shape[0]
    tm, tn = _pick(M, tm_prefs), _pick(N, tn_prefs)
    flag = np.asarray(flag, np.float32)
    tile_norm = jnp.asarray(flag.reshape(N // tn, tn).max(axis=1) > 0, jnp.int32)
    w_spec = (pl.BlockSpec((tn, K), lambda i, j: (j, 0)) if w_output_major
              else pl.BlockSpec((K, tn), lambda i, j: (0, j)))
    return pl.pallas_call(
        functools.partial(_mm_segnorm_kernel, w_output_major=w_output_major),
        grid=(M // tm, N // tn),
        in_specs=[pl.BlockSpec(memory_space=pltpu.SMEM),
                  pl.BlockSpec((tm, K), lambda i, j: (i, 0)),
                  w_spec,
                  pl.BlockSpec((1, tn), lambda i, j: (0, j)),
                  pl.BlockSpec((1, tn), lambda i, j: (0, j))],
        out_specs=pl.BlockSpec((tm, tn), lambda i, j: (i, j)),
        out_shape=jax.ShapeDtypeStruct((M, N), BF16),
        compiler_params=_params("parallel", "arbitrary"),
        name="mm_segnorm",
    )(tile_norm, a, w, gain.reshape(1, N).astype(F32), jnp.asarray(flag).reshape(1, N))


def mm_residual(a, w, res, tm_prefs=(1024, 512, 256, 128), tn_prefs=(512, 256, 128)):
    M, K = a.shape
    N = w.shape[1]
    tm, tn = _pick(M, tm_prefs), _pick(N, tn_prefs)
    return pl.pallas_call(
        _mm_residual_kernel,
        grid=(M // tm, N // tn),
        in_specs=[pl.BlockSpec((tm, K), lambda i, j: (i, 0)),
                  pl.BlockSpec((K, tn), lambda i, j: (0, j)),
                  pl.BlockSpec((tm, tn), lambda i, j: (i, j))],
        out_specs=pl.BlockSpec((tm, tn), lambda i, j: (i, j)),
        out_shape=jax.ShapeDtypeStruct((M, N), F32),
        compiler_params=_params("parallel", "arbitrary"),
        name="mm_residual",
    )(a, w, res)


def mm_swiglu(a, w, hidden):
    M, K = a.shape
    tm = _pick(M, (2048, 1024, 512, 256, 128))
    tn = _pick(hidden, (256, 128))
    nj = hidden // tn
    return pl.pallas_call(
        _mm_swiglu_kernel,
        grid=(M // tm, nj),
        in_specs=[pl.BlockSpec((tm, K), lambda i, j: (i, 0)),
                  pl.BlockSpec((K, tn), lambda i, j: (0, j)),
                  pl.BlockSpec((K, tn), lambda i, j: (0, j + nj))],
        out_specs=pl.BlockSpec((tm, tn), lambda i, j: (i, j)),
        out_shape=jax.ShapeDtypeStruct((M, hidden), BF16),
        compiler_params=_params("parallel", "arbitrary"),
        name="mm_swiglu",
    )(a, w, w)


def _branch_kernel(oa_ref, ob_ref, oc_ref, od_ref, w_ref, ga_ref, gb_ref, gc_ref, gd_ref, o_ref):
    for cs in _col_slabs(o_ref.shape[1]):
        acc = ga_ref[:, cs].astype(F32) * _dot(oa_ref[...], w_ref[0, :, cs])
        acc += gb_ref[:, cs].astype(F32) * _dot(ob_ref[...], w_ref[1, :, cs])
        acc += gc_ref[:, cs].astype(F32) * _dot(oc_ref[...], w_ref[2, :, cs])
        acc += gd_ref[:, cs].astype(F32) * _dot(od_ref[...], w_ref[3, :, cs])
        o_ref[:, cs] = acc.astype(o_ref.dtype)


def branch_merge(outs, w_branch, gates):
    M, W = outs[0].shape
    D = w_branch.shape[2]
    tm = _pick(M, (1024, 512, 256, 128))
    tn = _pick(D, (512, 256, 128))
    nj = D // tn
    o_spec = pl.BlockSpec((tm, W), lambda i, j: (i, 0))
    g_specs = [pl.BlockSpec((tm, tn), functools.partial(lambda i, j, b: (i, b * nj + j), b=b))
               for b in range(N_BRANCH)]
    return pl.pallas_call(
        _branch_kernel,
        grid=(M // tm, nj),
        in_specs=[o_spec] * N_BRANCH + [pl.BlockSpec((N_BRANCH, W, tn), lambda i, j: (0, 0, j))] + g_specs,
        out_specs=pl.BlockSpec((tm, tn), lambda i, j: (i, j)),
        out_shape=jax.ShapeDtypeStruct((M, D), BF16),
        compiler_params=_params("parallel", "arbitrary"),
        name="branch_merge",
    )(*outs, w_branch, gates, gates, gates, gates)


def _mla_prep_kernel(qa_ref, kva_ref, kpe_ref, ct_ref, st_ref, qag_ref, kvag_ref,
                     wq_ref, wqs_ref, wk_ref, wv_ref, gq_ref, gk_ref,
                     q_ref, k_ref, v_ref):
    def norm(x, g):
        return (x * lax.rsqrt(jnp.mean(x * x, axis=-1, keepdims=True) + EPS) * g).astype(BF16)

    ct = ct_ref[...]
    st = st_ref[...]
    qn = norm(qa_ref[...].astype(F32), qag_ref[...])
    kvn = norm(kva_ref[...].astype(F32), kvag_ref[...])
    qfull = _dot(qn, wq_ref[...])
    qsw = _dot(qn, wqs_ref[...])
    knope = _dot(kvn, wk_ref[...])
    vals = _dot(kvn, wv_ref[...])
    ones = jnp.ones((vals.shape[0], MLA_V), v_ref.dtype)
    for h in range(MLA_HEADS):
        v_ref[:, 2 * h * MLA_V:(2 * h + 1) * MLA_V] = vals[:, h * MLA_V:(h + 1) * MLA_V].astype(v_ref.dtype)
        v_ref[:, (2 * h + 1) * MLA_V:(2 * h + 2) * MLA_V] = ones

    kpe = kpe_ref[...].astype(F32)
    kpe_rot = kpe * ct + pltpu.roll(kpe, 2 * (MLA_ROPE // 2), 1) * st
    kpe_ss = jnp.sum(kpe_rot * kpe_rot, axis=-1, keepdims=True)
    gq_n, gq_r = gq_ref[:, :LANES], gq_ref[:, LANES:]
    gk_n, gk_r = gk_ref[:, :LANES], gk_ref[:, LANES:]
    for h in range(MLA_HEADS):
        lo = h * MLA_QK_PAD
        q_n = qfull[:, lo:lo + LANES]
        q_r = qfull[:, lo + LANES:lo + 2 * LANES] * ct + qsw[:, h * LANES:(h + 1) * LANES] * st
        ss = jnp.sum(q_n * q_n, axis=-1, keepdims=True) + jnp.sum(q_r * q_r, axis=-1, keepdims=True)
        r = lax.rsqrt(ss * (1.0 / MLA_QK) + EPS)
        q_ref[:, lo:lo + LANES] = (q_n * r * gq_n).astype(q_ref.dtype)
        q_ref[:, lo + LANES:lo + 2 * LANES] = (q_r * r * gq_r).astype(q_ref.dtype)
        k_n = knope[:, h * LANES:(h + 1) * LANES]
        ss = jnp.sum(k_n * k_n, axis=-1, keepdims=True) + kpe_ss
        r = lax.rsqrt(ss * (1.0 / MLA_QK) + EPS)
        k_ref[:, lo:lo + LANES] = (k_n * r * gk_n).astype(k_ref.dtype)
        k_ref[:, lo + LANES:lo + 2 * LANES] = (kpe_rot * r * gk_r).astype(k_ref.dtype)


def mla_prep(mix, cols, S, ct, st, lw):
    M = mix.shape[0]
    tm = _pick(S, (256, 128))
    ns = S // tm
    HQ = MLA_HEADS * MLA_QK_PAD
    HV = MLA_HEADS * 2 * MLA_V

    def col(width, off):
        assert off % width == 0
        return pl.BlockSpec((tm, width), lambda i: (i, off // width))

    def whole(a):
        return pl.BlockSpec(a.shape, lambda i: (0,) * a.ndim)

    consts = (lw["mla_qa_g"], lw["mla_kva_g"], lw["mla_wq"], lw["mla_wq_sw"], lw["mla_wk"], lw["mla_wv"],
              lw["mla_gq"], lw["mla_gk"])
    return pl.pallas_call(
        _mla_prep_kernel,
        grid=(M // tm,),
        in_specs=[col(MLA_Q_LORA, cols["qa"]), col(MLA_KV_LORA, cols["kva"]), col(LANES, cols["kpe"]),
                  pl.BlockSpec((tm, LANES), lambda i: (i % ns, 0)),
                  pl.BlockSpec((tm, LANES), lambda i: (i % ns, 0))] + [whole(c) for c in consts],
        out_specs=[pl.BlockSpec((tm, HQ), lambda i: (i, 0)),
                   pl.BlockSpec((tm, HQ), lambda i: (i, 0)),
                   pl.BlockSpec((tm, HV), lambda i: (i, 0))],
        out_shape=[jax.ShapeDtypeStruct((M, HQ), BF16),
                   jax.ShapeDtypeStruct((M, HQ), BF16),
                   jax.ShapeDtypeStruct((M, HV), BF16)],
        compiler_params=_params("parallel"),
        name="mla_prep",
    )(mix, mix, mix, ct, st, *consts)


def _qk(q, k):
    return lax.dot_general(q, k, (((1,), (1,)), ((), ())), preferred_element_type=F32)


def _dense_attn_kernel(q_ref, k_ref, v_ref, o_ref, *, n_sub, chunk):
    dv = o_ref.shape[1]
    ts = q_ref.shape[0] // n_sub
    seq = k_ref.shape[0]
    for t in range(n_sub):
        rows = slice(t * ts, (t + 1) * ts)
        q = q_ref[rows, :]
        pieces = [_qk(q, k_ref[c:c + chunk, :]) for c in range(0, seq, chunk)]
        mt = _lane_fold(pieces[0], jnp.maximum)
        for piece in pieces[1:]:
            mt = jnp.maximum(mt, _lane_fold(piece, jnp.maximum))
        m = jnp.max(mt, axis=-1, keepdims=True)
        o = jnp.zeros((ts, 2 * dv), F32)
        for c, piece in zip(range(0, seq, chunk), pieces):
            o = o + _dot(jnp.exp2(piece - m).astype(BF16), v_ref[c:c + chunk, :])
        o_ref[rows, :] = (o[:, :dv] / o[:, dv:]).astype(o_ref.dtype)


def dense_attention(q, k, v, B, S, H, dk, dv, n_sub=4):
    tq = _pick(S, (1024, 512, 256))
    chunk = _pick(S, (512, 256))
    q3, k3, v3 = (t.reshape(B, S, t.shape[1]) for t in (q, k, v))
    out = pl.pallas_call(
        functools.partial(_dense_attn_kernel, n_sub=n_sub, chunk=chunk),
        grid=(B, H, S // tq),
        in_specs=[pl.BlockSpec((None, tq, dk), lambda b, h, i: (b, i, h)),
                  pl.BlockSpec((None, S, dk), lambda b, h, i: (b, 0, h)),
                  pl.BlockSpec((None, S, 2 * dv), lambda b, h, i: (b, 0, h))],
        out_specs=pl.BlockSpec((None, tq, dv), lambda b, h, i: (b, i, h)),
        out_shape=jax.ShapeDtypeStruct((B, S, H * dv), BF16),
        compiler_params=_params("parallel", "parallel", "arbitrary"),
        name="dense_attention",
    )(q3, k3, v3)
    return out.reshape(B * S, H * dv)


def _lane_fold(x, op):
    out = x[:, :LANES]
    for j in range(1, x.shape[1] // LANES):
        out = op(out, x[:, j * LANES:(j + 1) * LANES])
    return out


def _band_kernel(sink_ref, q_ref, k_ref, v_ref, bias_ref, o_ref, *, tq, tk, radius, seq, n_chunks, hpb, rep):
    hb = pl.program_id(1)
    q0 = pl.program_id(2) * tq
    starts, tiles = [], []
    for c in range(n_chunks):
        start = q0 - radius + c * tk
        inside = jnp.logical_and(start >= 0, start + tk <= seq)
        starts.append(pl.multiple_of(jnp.clip(start, 0, seq - tk), tk))
        tiles.append(jnp.where(inside, c, n_chunks))
    for hh in range(hpb):
        qs = slice(hh * HEAD_DIM, (hh + 1) * HEAD_DIM)
        ks = slice((hh // rep) * HEAD_DIM, (hh // rep + 1) * HEAD_DIM)
        q = q_ref[:, qs]
        scores = [_qk(q, k_ref[pl.ds(sc, tk), ks]) + bias_ref[hh, t] for sc, t in zip(starts, tiles)]
        mt = _lane_fold(scores[0], jnp.maximum)
        for s in scores[1:]:
            mt = jnp.maximum(mt, _lane_fold(s, jnp.maximum))
        sink = sink_ref[hb * hpb + hh] * LOG2E
        m = jnp.maximum(jnp.max(mt, axis=-1, keepdims=True), sink)
        lt = jnp.zeros((tq, LANES), F32)
        acc = jnp.zeros((tq, HEAD_DIM), F32)
        for s, sc in zip(scores, starts):
            p = jnp.exp2(s - m)
            lt = lt + _lane_fold(p, jnp.add)
            acc = acc + _dot(p.astype(BF16), v_ref[pl.ds(sc, tk), ks])
        l = jnp.sum(lt, axis=-1, keepdims=True) + jnp.exp2(sink - m)
        o_ref[:, qs] = (acc / l).astype(o_ref.dtype)


def band_bias(slopes, mult_fn, tq, tk, radius):
    n_chunks = (tq + 2 * radius) // tk
    a = jnp.arange(tq)[None, :, None]
    c = jnp.arange(tk)[None, None, :]
    delta = (jnp.arange(n_chunks)[:, None, None] * tk - radius) + c - a
    mult = mult_fn(delta)
    dist = jnp.abs(delta).astype(F32)
    logm = jnp.log2(jnp.maximum(mult, 1).astype(F32))
    bias = logm[None] - (slopes * LOG2E)[:, None, None, None] * dist[None]
    bias = jnp.where(mult[None] > 0, bias, NEG_INF)
    dead = jnp.full((slopes.shape[0], 1, tq, tk), NEG_INF, F32)
    return jnp.concatenate([bias, dead], axis=1)


def dil_multiplicity(delta):
    m = jnp.zeros(delta.shape, jnp.int32)
    for window, dil in DIL_PATTERNS:
        m = m + ((delta % dil == 0) & (jnp.abs(delta) <= window // 2)).astype(jnp.int32)
    return m


def win_multiplicity(delta):
    return (jnp.abs(delta) <= WIN_RADIUS).astype(jnp.int32)


def band_attention(qarr, q_off, karr, k_off, varr, v_off, bias, sinks, B, S, H, rep, hpb, tq, tk, radius):
    n_chunks = bias.shape[1] - 1
    assert radius % tk == 0 and tq % tk == 0 and S % tq == 0 and S >= tk and hpb % rep == 0 and H % hpb == 0
    wq, wk = hpb * HEAD_DIM, (hpb // rep) * HEAD_DIM
    assert q_off % wq == 0 and k_off % wk == 0 and v_off % wk == 0
    qb, kb, vb = q_off // wq, k_off // wk, v_off // wk
    q3, k3, v3 = (t.reshape(B, S, t.shape[1]) for t in (qarr, karr, varr))
    kern = functools.partial(_band_kernel, tq=tq, tk=tk, radius=radius, seq=S, n_chunks=n_chunks, hpb=hpb, rep=rep)
    out = pl.pallas_call(
        kern,
        grid=(B, H // hpb, S // tq),
        in_specs=[pl.BlockSpec(memory_space=pltpu.SMEM),
                  pl.BlockSpec((None, tq, wq), lambda b, h, i: (b, i, qb + h)),
                  pl.BlockSpec((None, S, wk), lambda b, h, i: (b, 0, kb + h)),
                  pl.BlockSpec((None, S, wk), lambda b, h, i: (b, 0, vb + h)),
                  pl.BlockSpec((hpb, n_chunks + 1, tq, tk), lambda b, h, i: (h, 0, 0, 0))],
        out_specs=pl.BlockSpec((None, tq, wq), lambda b, h, i: (b, i, h)),
        out_shape=jax.ShapeDtypeStruct((B, S, H * HEAD_DIM), BF16),
        compiler_params=_params("parallel", "parallel", "arbitrary"),
        name="band_attention",
    )(sinks.astype(F32), q3, k3, v3, bias)
    return out.reshape(B * S, H * HEAD_DIM)


def _diff_kernel(slope_ref, lam_ref, q_ref, k_ref, v_ref, pos_ref, g_ref, o_ref, *, tq, lam_init, chunk, n_sub):
    h = pl.program_id(1)
    q0 = pl.program_id(2) * tq
    lp = lam_ref[...]
    lam = (jnp.exp(jnp.sum(lp[0:1] * lp[1:2], axis=-1, keepdims=True))
           - jnp.exp(jnp.sum(lp[2:3] * lp[3:4], axis=-1, keepdims=True)) + lam_init)
    slope = slope_ref[h]
    seq = k_ref.shape[0]
    kpos = pos_ref[...] * slope
    ts = tq // n_sub
    for t in range(n_sub):
        rows = slice(t * ts, (t + 1) * ts)
        qrow = (q0 + t * ts + lax.broadcasted_iota(jnp.int32, (ts, LANES), 0)).astype(F32) * slope
        qpos = jnp.concatenate([qrow] * (chunk // LANES), axis=1)
        q1, q2 = q_ref[rows, :DIFF_HD], q_ref[rows, DIFF_HD:]
        t1, t2, mt1, mt2 = [], [], None, None
        for c in range(0, seq, chunk):
            b = jnp.abs(kpos[:, c:c + chunk] - qpos)
            a1 = _qk(q1, k_ref[c:c + chunk, :DIFF_HD]) - b
            a2 = _qk(q2, k_ref[c:c + chunk, DIFF_HD:]) - b
            t1.append(a1)
            t2.append(a2)
            f1, f2 = _lane_fold(a1, jnp.maximum), _lane_fold(a2, jnp.maximum)
            mt1 = f1 if mt1 is None else jnp.maximum(mt1, f1)
            mt2 = f2 if mt2 is None else jnp.maximum(mt2, f2)
        m1 = jnp.max(mt1, axis=-1, keepdims=True)
        m2 = jnp.max(mt2, axis=-1, keepdims=True)
        lt1 = jnp.zeros((ts, LANES), F32)
        lt2 = jnp.zeros((ts, LANES), F32)
        o1 = jnp.zeros((ts, 2 * DIFF_HD), F32)
        o2 = jnp.zeros((ts, 2 * DIFF_HD), F32)
        for i, c in enumerate(range(0, seq, chunk)):
            p1, p2 = jnp.exp2(t1[i] - m1), jnp.exp2(t2[i] - m2)
            lt1 = lt1 + _lane_fold(p1, jnp.add)
            lt2 = lt2 + _lane_fold(p2, jnp.add)
            o1 = o1 + _dot(p1.astype(BF16), v_ref[c:c + chunk, :])
            o2 = o2 + _dot(p2.astype(BF16), v_ref[c:c + chunk, :])
        l1 = jnp.sum(lt1, axis=-1, keepdims=True)
        l2 = jnp.sum(lt2, axis=-1, keepdims=True)
        o = o1 * (1.0 / l1) - o2 * (lam / l2)
        r = lax.rsqrt(jnp.mean(o * o, axis=-1, keepdims=True) + EPS)
        o_ref[rows, :] = (o * r * g_ref[...] * (1.0 - lam_init)).astype(o_ref.dtype)


def diff_attention(mix, cols, slopes, lam_p, subln_g, B, S, layer):
    tq = _pick(S, (512, 256, 128))
    lam_init = 0.8 - 0.6 * math.exp(-0.3 * layer)
    W = 2 * DIFF_HD
    qb, kb, vb = (cols[n] // W for n in ("fq", "fk", "fv"))
    m3 = mix.reshape(B, S, mix.shape[1])
    pos = jnp.arange(S, dtype=F32).reshape(1, S)
    kern = functools.partial(_diff_kernel, tq=tq, lam_init=lam_init, chunk=_pick(S, (512, 256)),
                             n_sub=2 if S > 2048 else 4)
    out = pl.pallas_call(
        kern,
        grid=(B, DIFF_HEADS, S // tq),
        in_specs=[pl.BlockSpec(memory_space=pltpu.SMEM),
                  pl.BlockSpec((4, DIFF_HD), lambda b, h, i: (0, 0)),
                  pl.BlockSpec((None, tq, W), lambda b, h, i: (b, i, qb + h)),
                  pl.BlockSpec((None, S, W), lambda b, h, i: (b, 0, kb + h)),
                  pl.BlockSpec((None, S, W), lambda b, h, i: (b, 0, vb + h)),
                  pl.BlockSpec((1, S), lambda b, h, i: (0, 0)),
                  pl.BlockSpec((1, W), lambda b, h, i: (0, 0))],
        out_specs=pl.BlockSpec((None, tq, W), lambda b, h, i: (b, i, h)),
        out_shape=jax.ShapeDtypeStruct((B, S, DIFF_HEADS * W), BF16),
        compiler_params=_params("parallel", "parallel", "arbitrary"),
        name="diff_attention",
    )((slopes * LOG2E).astype(F32), lam_p.astype(F32), m3, m3, m3, pos, subln_g.reshape(1, W).astype(F32))
    return out.reshape(B * S, DIFF_HEADS * W)


def _mem_attn_kernel(x_ref, gm_ref, wq_ref, gq_ref, kv_ref, wo_ref, gf_ref, o_ref, hf_ref):
    def rms(t):
        return lax.rsqrt(jnp.mean(t * t, axis=-1, keepdims=True) + EPS)

    x = x_ref[...]
    hm = (x * rms(x) * gm_ref[...]).astype(BF16)
    qacc = _dot(hm, wq_ref[...])
    heads = []
    for h in range(MEM_HEADS):
        seg = slice(h * HEAD_DIM, (h + 1) * HEAD_DIM)
        y = qacc[:, seg]
        q = (y * rms(y) * gq_ref[:, seg]).astype(BF16)
        s = _qk(q, kv_ref[:, seg])
        p = jnp.exp2(s - jnp.max(s, axis=-1, keepdims=True))
        l = jnp.sum(p, axis=-1, keepdims=True)
        vseg = slice(MEM_W + h * HEAD_DIM, MEM_W + (h + 1) * HEAD_DIM)
        heads.append((_dot(p.astype(BF16), kv_ref[:, vseg]) / l).astype(BF16))
    x2 = x + _dot(jnp.concatenate(heads, axis=-1), wo_ref[...])
    o_ref[...] = x2
    hf_ref[...] = (x2 * rms(x2) * gf_ref[...]).astype(hf_ref.dtype)


def mem_attention(x, kv, lw, B, S):
    D = x.shape[1]
    Mt = kv.shape[0] // B
    tq = _pick(S, (256, 128))
    kv3 = kv.reshape(B, Mt, 2 * MEM_W)
    x3 = x.reshape(B, S, D)

    def row(n):
        return pl.BlockSpec((1, n), lambda b, i: (0, 0))

    out, hf = pl.pallas_call(
        _mem_attn_kernel,
        grid=(B, S // tq),
        in_specs=[pl.BlockSpec((None, tq, D), lambda b, i: (b, i, 0)),
                  row(D),
                  pl.BlockSpec((D, MEM_W), lambda b, i: (0, 0)),
                  row(MEM_W),
                  pl.BlockSpec((None, Mt, 2 * MEM_W), lambda b, i: (b, 0, 0)),
                  pl.BlockSpec((MEM_W, D), lambda b, i: (0, 0)),
                  row(D)],
        out_specs=[pl.BlockSpec((None, tq, D), lambda b, i: (b, i, 0)),
                   pl.BlockSpec((None, tq, D), lambda b, i: (b, i, 0))],
        out_shape=[jax.ShapeDtypeStruct((B, S, D), F32), jax.ShapeDtypeStruct((B, S, D), BF16)],
        compiler_params=_params("parallel", "arbitrary"),
        name="mem_attention",
    )(x3, lw["ln_mem_g"].reshape(1, D).astype(F32), lw["mem_wq"], lw["mem_q_gain"].reshape(1, MEM_W),
      kv3, lw["mem_wo"], lw["ln_ffn_g"].reshape(1, D).astype(F32))
    return out.reshape(B * S, D), hf.reshape(B * S, D)


def alibi_slopes(n):
    return 2.0 ** (-8.0 * jnp.arange(1, n + 1, dtype=F32) / n)


MIX_NAMES = ("qa", "kva", "kpe", "dq", "dk", "dv", "wq", "wk", "wv", "fq", "fk", "fv")


def _mix_layout():
    src = {n: (MIX_OFFSETS[i], MIX_OFFSETS[i + 1]) for i, n in enumerate(MIX_NAMES)}
    head = -(-(src["kpe"][0] + LANES) // MIX_TILE) * MIX_TILE
    shift = head - src["dq"][0]
    cols = {n: src[n][0] + (0 if n in ("qa", "kva", "kpe") else shift) for n in MIX_NAMES}
    total = -(-(MIX_COLS + shift) // MIX_TILE) * MIX_TILE
    return src, cols, head, shift, total


def _pack_kernel(src3_ref, o_ref, *, n_head, kpe_off, tn):
    src_ref = src3_ref.at[0]
    j = pl.program_id(0)
    if not n_head:
        o_ref[...] = src_ref[...].astype(o_ref.dtype)
        return

    @pl.when(j != n_head - 1)
    def _():
        o_ref[...] = src_ref[...].astype(o_ref.dtype)

    @pl.when(j == n_head - 1)
    def _():
        quarter = MLA_ROPE // 2
        x1 = src_ref[kpe_off:kpe_off + quarter, :].astype(o_ref.dtype)
        x2 = src_ref[kpe_off + quarter:kpe_off + 2 * quarter, :].astype(o_ref.dtype)
        if kpe_off:
            o_ref[:kpe_off, :] = src_ref[:kpe_off, :].astype(o_ref.dtype)
        for t, piece in enumerate((x1, x2, x2, x1)):
            o_ref[kpe_off + t * quarter:kpe_off + (t + 1) * quarter, :] = piece
        if kpe_off + LANES < tn:
            o_ref[kpe_off + LANES:, :] = jnp.zeros((tn - kpe_off - LANES, o_ref.shape[1]), o_ref.dtype)


def _cast_kernel(x_ref, o_ref):
    o_ref[...] = x_ref[...].astype(o_ref.dtype)


def cast_layer(w, l):
    _, R, C = w.shape
    tc = _pick(C, (2048, 1024, 512, 256, 128))
    tr = _pick(R, tuple(t for t in (2048, 1024, 512, 256, 128, 64, 8) if t * tc <= 2 ** 21))
    return pl.pallas_call(
        _cast_kernel,
        grid=(R // tr, C // tc),
        in_specs=[pl.BlockSpec((None, tr, tc), lambda i, j: (l, i, j))],
        out_specs=pl.BlockSpec((tr, tc), lambda i, j: (i, j)),
        out_shape=jax.ShapeDtypeStruct((R, C), BF16),
        compiler_params=_params("parallel", "arbitrary"),
        name="cast_layer",
    )(w)


def pack_shifted(src_t, l, n_out, shift, head_rows=0, kpe_row=0):
    _, C, K = src_t.shape
    tn = MIX_TILE
    n_head = head_rows // tn
    assert shift % 8 == 0 and head_rows % tn == 0 and (n_head == 0 or kpe_row // tn == n_head - 1)
    kern = functools.partial(_pack_kernel, n_head=n_head, kpe_off=kpe_row % tn, tn=tn)
    return pl.pallas_call(
        kern,
        grid=(n_out // tn,),
        in_specs=[pl.BlockSpec((pl.Element(1), pl.Element(tn), pl.Element(K)),
                               lambda j: (l, pl.multiple_of(jnp.where(j < n_head, j * tn, j * tn - shift), 8), 0))],
        out_specs=pl.BlockSpec((tn, K), lambda j: (j, 0)),
        out_shape=jax.ShapeDtypeStruct((n_out, K), BF16),
        compiler_params=_params("arbitrary"),
        name="pack_shifted",
    )(src_t)


def pack_layer(p, l):
    src, cols, head_w, shift, total = _mix_layout()
    half = MLA_ROPE // 2
    qscale = HEAD_DIM ** -0.5 * LOG2E

    def tile(g, n):
        return jnp.tile(g.astype(F32), n)

    norm_gain = {
        "dq": tile(p["dil_qk_g"][l, 0], DIL_HEADS) * qscale, "dk": tile(p["dil_qk_g"][l, 1], DIL_HEADS),
        "fq": tile(p["diff_qk_g"][l, 0], 2 * DIFF_HEADS) * (DIFF_HD ** -0.5 * LOG2E),
        "fk": tile(p["diff_qk_g"][l, 1], 2 * DIFF_HEADS),
        "wq": tile(p["win_qk_g"][l, 0], WIN_Q_HEADS) * qscale, "wk": tile(p["win_qk_g"][l, 1], WIN_KV_HEADS),
    }
    gain = jnp.ones((total,), F32)
    flag = np.zeros((total,), np.float32)
    for n, g in norm_gain.items():
        gain = gain.at[cols[n]:cols[n] + g.shape[0]].set(g)
        flag[cols[n]:cols[n] + g.shape[0]] = 1.0

    w_in_t = jnp.swapaxes(p["w_in"], 1, 2)
    w_mix = pack_shifted(w_in_t, l, total, shift, head_rows=head_w, kpe_row=src["kpe"][0])
    w_gate = pack_shifted(w_in_t, l, N_BRANCH * D_MODEL, -MIX_COLS)

    wq3 = p["mla_wq_up"][l].reshape(MLA_Q_LORA, MLA_HEADS, MLA_QK)
    nope, x1, x2 = wq3[:, :, :MLA_NOPE], wq3[:, :, MLA_NOPE:MLA_NOPE + half], wq3[:, :, MLA_NOPE + half:]
    z = jnp.zeros((MLA_Q_LORA, MLA_HEADS, LANES - MLA_ROPE), F32)
    wq_full = jnp.concatenate([nope, x1, x2, z], axis=-1).reshape(MLA_Q_LORA, MLA_HEADS * MLA_QK_PAD)
    wq_sw = jnp.concatenate([x2, x1, z], axis=-1).reshape(MLA_Q_LORA, MLA_HEADS * LANES)
    wkv3 = p["mla_wkv_up"][l].reshape(MLA_KV_LORA, MLA_HEADS, MLA_NOPE + MLA_V)
    zg = jnp.zeros((LANES - MLA_ROPE,), F32)
    qk_g = p["mla_qk_g"][l].astype(F32)

    return {
        "ln_mix_g": p["ln_mix_g"][l],
        "w_mix": w_mix, "w_gate": w_gate, "mix_gain": gain, "mix_flag": flag,
        "mla_qa_g": p["mla_qa_g"][l].reshape(1, -1).astype(F32),
        "mla_kva_g": p["mla_kva_g"][l].reshape(1, -1).astype(F32),
        "mla_wq": wq_full.astype(BF16), "mla_wq_sw": wq_sw.astype(BF16),
        "mla_wk": wkv3[:, :, :MLA_NOPE].reshape(MLA_KV_LORA, -1).astype(BF16),
        "mla_wv": wkv3[:, :, MLA_NOPE:].reshape(MLA_KV_LORA, -1).astype(BF16),
        "mla_gq": (jnp.concatenate([qk_g[0], zg]) * (MLA_QK ** -0.5 * LOG2E)).reshape(1, -1),
        "mla_gk": jnp.concatenate([qk_g[1], zg]).reshape(1, -1),
        "win_sink": p["win_sink"][l], "diff_lambda": p["diff_lambda"][l], "diff_subln_g": p["diff_subln_g"][l],
        "w_branch": cast_layer(p["w_branch"].reshape(DEPTH, N_BRANCH * BRANCH_W, D_MODEL), l).reshape(
            N_BRANCH, BRANCH_W, D_MODEL),
        "w_out": cast_layer(p["w_out"], l),
        "ln_mem_g": p["ln_mem_g"][l], "mem_ln_g": p["mem_ln_g"][l],
        "mem_wq": cast_layer(p["mem_wq"], l), "mem_wkv": cast_layer(p["mem_wkv"], l),
        "mem_q_gain": tile(p["mem_qk_g"][l, 0], MEM_HEADS) * (HEAD_DIM ** -0.5 * LOG2E),
        "mem_kv_gain": jnp.concatenate([tile(p["mem_qk_g"][l, 1], MEM_HEADS), jnp.ones((MEM_W,), F32)]),
        "mem_kv_flag": np.concatenate([np.ones((MEM_W,), np.float32), np.zeros((MEM_W,), np.float32)]),
        "mem_wo": cast_layer(p["mem_wo"], l),
        "ln_ffn_g": p["ln_ffn_g"][l],
        "ffn_w_in": cast_layer(p["ffn_w_in"], l), "ffn_w_out": cast_layer(p["ffn_w_out"], l),
    }


def rotary_tables(S):
    half = MLA_ROPE // 2
    inv_freq = ROPE_THETA ** (-jnp.arange(half, dtype=F32) / half)
    ang = jnp.arange(S, dtype=F32)[:, None] * inv_freq[None, :]
    cos, sin = jnp.cos(ang), jnp.sin(ang)
    z = jnp.zeros((S, LANES - MLA_ROPE), F32)
    return jnp.concatenate([cos, cos, z], axis=1), jnp.concatenate([-sin, sin, z], axis=1)


def _trunk(x, mem, layers, tables):
    B, S, D = x.shape
    M = B * S
    _, cols, _, _, _ = _mix_layout()
    ct, st = rotary_tables(S)
    xf = x.reshape(M, D)
    memf = mem.reshape(-1, D)
    dil_tq = _pick(S, (256,))
    for l, lw in enumerate(layers):
        h = rmsnorm(xf, lw["ln_mix_g"])
        mix = mm_segnorm(h, lw["w_mix"], lw["mix_gain"], lw["mix_flag"], w_output_major=True,
                         tm_prefs=(2048, 1024, 512, 256), tn_prefs=(MIX_TILE,))
        gates = mm_sigmoid(h, lw["w_gate"])

        q, k, v = mla_prep(mix, cols, S, ct, st, lw)
        o_mla = dense_attention(q, k, v, B, S, MLA_HEADS, MLA_QK_PAD, MLA_V)
        o_dil = band_attention(mix, cols["dq"], mix, cols["dk"], mix, cols["dv"], tables["dil_bias"],
                               jnp.full((DIL_HEADS,), NEG_INF, F32), B, S, DIL_HEADS, 1, DIL_HEADS // 2,
                               dil_tq, dil_tq, DIL_RADIUS)
        o_win = band_attention(mix, cols["wq"], mix, cols["wk"], mix, cols["wv"], tables["win_bias"],
                               lw["win_sink"], B, S, WIN_Q_HEADS, WIN_Q_HEADS // WIN_KV_HEADS, WIN_Q_HEADS,
                               2 * WIN_RADIUS, WIN_RADIUS, WIN_RADIUS)
        o_diff = diff_attention(mix, cols, tables["slopes_diff"], lw["diff_lambda"], lw["diff_subln_g"], B, S, l)

        merged = branch_merge((o_mla, o_dil, o_win, o_diff), lw["w_branch"], gates)
        xf = mm_residual(merged, lw["w_out"], xf)

        kvm = mm_segnorm(rmsnorm(memf, lw["mem_ln_g"]), lw["mem_wkv"], lw["mem_kv_gain"], lw["mem_kv_flag"])
        xf, hf = mem_attention(xf, kvm, lw, B, S)

        hid = mm_swiglu(hf, lw["ffn_w_in"], FFN_HIDDEN)
        xf = mm_residual(hid, lw["ffn_w_out"], xf, tm_prefs=(512, 256, 128), tn_prefs=(256, 128))
    return xf.reshape(B, S, D)


def kernel(x_prompt, x_sample, mem_prompt, mem_sample, ln_mix_g, w_in, mla_qa_g, mla_kva_g, mla_wq_up,
           mla_wkv_up, mla_qk_g, dil_qk_g, win_qk_g, win_sink, diff_qk_g, diff_lambda, diff_subln_g,
           w_branch, w_out, ln_mem_g, mem_ln_g, mem_wq, mem_wkv, mem_qk_g, mem_wo, ln_ffn_g, ffn_w_in,
           ffn_w_out):
    p = dict(ln_mix_g=ln_mix_g, w_in=w_in, mla_qa_g=mla_qa_g, mla_kva_g=mla_kva_g, mla_wq_up=mla_wq_up,
             mla_wkv_up=mla_wkv_up, mla_qk_g=mla_qk_g, dil_qk_g=dil_qk_g, win_qk_g=win_qk_g,
             win_sink=win_sink, diff_qk_g=diff_qk_g, diff_lambda=diff_lambda, diff_subln_g=diff_subln_g,
             w_branch=w_branch, w_out=w_out, ln_mem_g=ln_mem_g, mem_ln_g=mem_ln_g, mem_wq=mem_wq,
             mem_wkv=mem_wkv, mem_qk_g=mem_qk_g, mem_wo=mem_wo, ln_ffn_g=ln_ffn_g, ffn_w_in=ffn_w_in,
             ffn_w_out=ffn_w_out)
    layers = [pack_layer(p, l) for l in range(DEPTH)]
    tables = {
        "dil_bias": band_bias(alibi_slopes(DIL_HEADS), dil_multiplicity, 256, 256, DIL_RADIUS),
        "win_bias": band_bias(alibi_slopes(WIN_Q_HEADS), win_multiplicity, 2 * WIN_RADIUS, WIN_RADIUS,
                              WIN_RADIUS),
        "slopes_diff": alibi_slopes(DIFF_HEADS),
    }
    y_prompt = _trunk(x_prompt, mem_prompt, layers, tables)
    y_sample = _trunk(x_sample, mem_sample, layers, tables)
    return (y_prompt, y_sample)
```

```python
import functools
import math

import jax
import jax.numpy as jnp
import numpy as np
from jax import lax
from jax.experimental import pallas as pl
from jax.experimental.pallas import tpu as pltpu

F32 = jnp.float32
BF16 = jnp.bfloat16

D_MODEL = 4096
DEPTH = 2
EPS = 1e-6
NEG_INF = -1e30
N_BRANCH = 4
BRANCH_W = D_MODEL // N_BRANCH
HEAD_DIM = 128

MLA_NOPE = 128
MLA_ROPE = 64
MLA_V = 128
MLA_HEADS = BRANCH_W // MLA_V
MLA_Q_LORA = D_MODEL // 4
MLA_KV_LORA = D_MODEL // 8
MLA_QK = MLA_NOPE + MLA_ROPE
MLA_QK_PAD = 256
ROPE_THETA = 10000.0

DIL_HEADS = BRANCH_W // HEAD_DIM
DIL_PATTERNS = ((128, 1), (512, 4), (2048, 16))
DIL_RADIUS = max(w // 2 for w, _ in DIL_PATTERNS)

WIN_Q_HEADS = BRANCH_W // HEAD_DIM
WIN_KV_HEADS = WIN_Q_HEADS // 4
WIN_RADIUS = 128

DIFF_HD = 128
DIFF_HEADS = BRANCH_W // (2 * DIFF_HD)

MEM_HEADS = 4
MEM_W = MEM_HEADS * HEAD_DIM

FFN_HIDDEN = -(-8 * D_MODEL // (3 * 256)) * 256

MIX_SPLITS = (MLA_Q_LORA, MLA_KV_LORA, MLA_ROPE,
              BRANCH_W, BRANCH_W, BRANCH_W,
              BRANCH_W, WIN_KV_HEADS * HEAD_DIM, WIN_KV_HEADS * HEAD_DIM,
              BRANCH_W, BRANCH_W, BRANCH_W)
MIX_COLS = sum(MIX_SPLITS)
MIX_OFFSETS = tuple(int(o) for o in np.cumsum((0,) + MIX_SPLITS))

LOG2E = math.log2(math.e)
LANES = 128
MIX_TILE = 512
VMEM_LIMIT_BYTES = 56 * 1024 * 1024


def _pick(n, prefs):
    for p in prefs:
        if n % p == 0:
            return p
    raise ValueError(f"no tile in {prefs} divides {n}")


def _params(*sem):
    return pltpu.CompilerParams(dimension_semantics=sem, vmem_limit_bytes=VMEM_LIMIT_BYTES)


def _rmsnorm_kernel(x_ref, g_ref, o_ref):
    x = x_ref[...].astype(F32)
    ms = jnp.mean(x * x, axis=-1, keepdims=True)
    o_ref[...] = (x * lax.rsqrt(ms + EPS) * g_ref[...]).astype(o_ref.dtype)


def rmsnorm(x, g):
    M, D = x.shape
    tm = _pick(M, (512, 256, 128, 64, 8))
    return pl.pallas_call(
        _rmsnorm_kernel,
        grid=(M // tm,),
        in_specs=[pl.BlockSpec((tm, D), lambda i: (i, 0)),
                  pl.BlockSpec((1, D), lambda i: (0, 0))],
        out_specs=pl.BlockSpec((tm, D), lambda i: (i, 0)),
        out_shape=jax.ShapeDtypeStruct((M, D), BF16),
        compiler_params=_params("parallel"),
        name="rmsnorm",
    )(x, g.reshape(1, D).astype(F32))


def _dot(a, b):
    return jnp.dot(a, b, preferred_element_type=F32)


MXU_COLS = 256


def _sigmoid(x):
    return 0.5 * jnp.tanh(0.5 * x) + 0.5


def _col_slabs(n):
    w = MXU_COLS if n % MXU_COLS == 0 else n
    return [slice(c, c + w) for c in range(0, n, w)]


def _dot_nt(a, bt):
    return lax.dot_general(a, bt, (((1,), (1,)), ((), ())), preferred_element_type=F32)


def _mm_sigmoid_kernel(a_ref, bt_ref, o_ref):
    for cs in _col_slabs(o_ref.shape[1]):
        o_ref[:, cs] = _sigmoid(_dot_nt(a_ref[...], bt_ref[cs, :])).astype(o_ref.dtype)


def _mm_segnorm_kernel(tile_norm_ref, a_ref, b_ref, g_ref, f_ref, o_ref, *, w_output_major):
    def slab(cs):
        return _dot_nt(a_ref[...], b_ref[cs, :]) if w_output_major else _dot(a_ref[...], b_ref[:, cs])

    has_norm = tile_norm_ref[pl.program_id(1)] > 0

    @pl.when(has_norm)
    def _():
        for cs in _col_slabs(o_ref.shape[1]):
            acc = slab(cs)
            for c in range(0, acc.shape[1], LANES):
                seg = slice(cs.start + c, cs.start + c + LANES)
                y = acc[:, c:c + LANES]
                r = lax.rsqrt(jnp.mean(y * y, axis=-1, keepdims=True) + EPS)
                mult = jnp.where(f_ref[:, seg] > 0.0, r, 1.0) * g_ref[:, seg]
                o_ref[:, seg] = (y * mult).astype(o_ref.dtype)

    @pl.when(jnp.logical_not(has_norm))
    def _():
        for cs in _col_slabs(o_ref.shape[1]):
            o_ref[:, cs] = (slab(cs) * g_ref[:, cs]).astype(o_ref.dtype)


def _mm_residual_kernel(a_ref, b_ref, r_ref, o_ref):
    for cs in _col_slabs(o_ref.shape[1]):
        o_ref[:, cs] = r_ref[:, cs] + _dot(a_ref[...], b_ref[:, cs])


def _mm_swiglu_kernel(a_ref, bg_ref, bu_ref, o_ref):
    half = a_ref.shape[0] // 2
    for rows in (slice(0, half), slice(half, 2 * half)):
        a = a_ref[rows, :]
        g = _dot(a, bg_ref[...])
        u = _dot(a, bu_ref[...])
        o_ref[rows, :] = (g * _sigmoid(g) * u).astype(o_ref.dtype)


def mm_sigmoid(a, wt, tm_prefs=(2048, 1024, 512, 256, 128), tn_prefs=(512, 256, 128)):
    M, K = a.shape
    N = wt.shape[0]
    tm, tn = _pick(M, tm_prefs), _pick(N, tn_prefs)
    return pl.pallas_call(
        _mm_sigmoid_kernel,
        grid=(M // tm, N // tn),
        in_specs=[pl.BlockSpec((tm, K), lambda i, j: (i, 0)),
                  pl.BlockSpec((tn, K), lambda i, j: (j, 0))],
        out_specs=pl.BlockSpec((tm, tn), lambda i, j: (i, j)),
        out_shape=jax.ShapeDtypeStruct((M, N), BF16),
        compiler_params=_params("parallel", "arbitrary"),
        name="mm_sigmoid",
    )(a, wt)


def mm_segnorm(a, w, gain, flag, w_output_major=False, tm_prefs=(1024, 512, 256, 128), tn_prefs=(512, 256, 128)):
    M, K = a.shape
    N = gain.shape[0]
    tm, tn = _pick(M, tm_prefs), _pick(N, tn_prefs)
    flag = np.asarray(flag, np.float32)
    tile_norm = jnp.asarray(flag.reshape(N // tn, tn).max(axis=1) > 0, jnp.int32)
    w_spec = (pl.BlockSpec((tn, K), lambda i, j: (j, 0)) if w_output_major
              else pl.BlockSpec((K, tn), lambda i, j: (0, j)))
    return pl.pallas_call(
        functools.partial(_mm_segnorm_kernel, w_output_major=w_output_major),
        grid=(M // tm, N // tn),
        in_specs=[pl.BlockSpec(memory_space=pltpu.SMEM),
                  pl.BlockSpec((tm, K), lambda i, j: (i, 0)),
                  w_spec,
                  pl.BlockSpec((1, tn), lambda i, j: (0, j)),
                  pl.BlockSpec((1, tn), lambda i, j: (0, j))],
        out_specs=pl.BlockSpec((tm, tn), lambda i, j: (i, j)),
        out_shape=jax.ShapeDtypeStruct((M, N), BF16),
        compiler_params=_params("parallel", "arbitrary"),
        name="mm_segnorm",
    )(tile_norm, a, w, gain.reshape(1, N).astype(F32), jnp.asarray(flag).reshape(1, N))


def mm_residual(a, w, res, tm_prefs=(1024, 512, 256, 128), tn_prefs=(512, 256, 128)):
    M, K = a.shape
    N = w.shape[1]
    tm, tn = _pick(M, tm_prefs), _pick(N, tn_prefs)
    return pl.pallas_call(
        _mm_residual_kernel,
        grid=(M // tm, N // tn),
        in_specs=[pl.BlockSpec((tm, K), lambda i, j: (i, 0)),
                  pl.BlockSpec((K, tn), lambda i, j: (0, j)),
                  pl.BlockSpec((tm, tn), lambda i, j: (i, j))],
        out_specs=pl.BlockSpec((tm, tn), lambda i, j: (i, j)),
        out_shape=jax.ShapeDtypeStruct((M, N), F32),
        compiler_params=_params("parallel", "arbitrary"),
        name="mm_residual",
    )(a, w, res)


def mm_swiglu(a, w, hidden):
    M, K = a.shape
    tm = _pick(M, (2048, 1024, 512, 256, 128))
    tn = _pick(hidden, (256, 128))
    nj = hidden // tn
    return pl.pallas_call(
        _mm_swiglu_kernel,
        grid=(M // tm, nj),
        in_specs=[pl.BlockSpec((tm, K), lambda i, j: (i, 0)),
                  pl.BlockSpec((K, tn), lambda i, j: (0, j)),
                  pl.BlockSpec((K, tn), lambda i, j: (0, j + nj))],
        out_specs=pl.BlockSpec((tm, tn), lambda i, j: (i, j)),
        out_shape=jax.ShapeDtypeStruct((M, hidden), BF16),
        compiler_params=_params("parallel", "arbitrary"),
        name="mm_swiglu",
    )(a, w, w)


def _branch_kernel(oa_ref, ob_ref, oc_ref, od_ref, w_ref, ga_ref, gb_ref, gc_ref, gd_ref, o_ref):
    for cs in _col_slabs(o_ref.shape[1]):
        acc = ga_ref[:, cs].astype(F32) * _dot(oa_ref[...], w_ref[0, :, cs])
        acc += gb_ref[:, cs].astype(F32) * _dot(ob_ref[...], w_ref[1, :, cs])
        acc += gc_ref[:, cs].astype(F32) * _dot(oc_ref[...], w_ref[2, :, cs])
        acc += gd_ref[:, cs].astype(F32) * _dot(od_ref[...], w_ref[3, :, cs])
        o_ref[:, cs] = acc.astype(o_ref.dtype)


def branch_merge(outs, w_branch, gates):
    M, W = outs[0].shape
    D = w_branch.shape[2]
    tm = _pick(M, (1024, 512, 256, 128))
    tn = _pick(D, (512, 256, 128))
    nj = D // tn
    o_spec = pl.BlockSpec((tm, W), lambda i, j: (i, 0))
    g_specs = [pl.BlockSpec((tm, tn), functools.partial(lambda i, j, b: (i, b * nj + j), b=b))
               for b in range(N_BRANCH)]
    return pl.pallas_call(
        _branch_kernel,
        grid=(M // tm, nj),
        in_specs=[o_spec] * N_BRANCH + [pl.BlockSpec((N_BRANCH, W, tn), lambda i, j: (0, 0, j))] + g_specs,
        out_specs=pl.BlockSpec((tm, tn), lambda i, j: (i, j)),
        out_shape=jax.ShapeDtypeStruct((M, D), BF16),
        compiler_params=_params("parallel", "arbitrary"),
        name="branch_merge",
    )(*outs, w_branch, gates, gates, gates, gates)


def _mla_prep_kernel(qa_ref, kva_ref, kpe_ref, ct_ref, st_ref, qag_ref, kvag_ref,
                     wq_ref, wqs_ref, wk_ref, wv_ref, gq_ref, gk_ref,
                     q_ref, k_ref, v_ref):
    def norm(x, g):
        return (x * lax.rsqrt(jnp.mean(x * x, axis=-1, keepdims=True) + EPS) * g).astype(BF16)

    ct = ct_ref[...]
    st = st_ref[...]
    qn = norm(qa_ref[...].astype(F32), qag_ref[...])
    kvn = norm(kva_ref[...].astype(F32), kvag_ref[...])
    qfull = _dot(qn, wq_ref[...])
    qsw = _dot(qn, wqs_ref[...])
    knope = _dot(kvn, wk_ref[...])
    vals = _dot(kvn, wv_ref[...])
    ones = jnp.ones((vals.shape[0], MLA_V), v_ref.dtype)
    for h in range(MLA_HEADS):
        v_ref[:, 2 * h * MLA_V:(2 * h + 1) * MLA_V] = vals[:, h * MLA_V:(h + 1) * MLA_V].astype(v_ref.dtype)
        v_ref[:, (2 * h + 1) * MLA_V:(2 * h + 2) * MLA_V] = ones

    kpe = kpe_ref[...].astype(F32)
    kpe_rot = kpe * ct + pltpu.roll(kpe, 2 * (MLA_ROPE // 2), 1) * st
    kpe_ss = jnp.sum(kpe_rot * kpe_rot, axis=-1, keepdims=True)
    gq_n, gq_r = gq_ref[:, :LANES], gq_ref[:, LANES:]
    gk_n, gk_r = gk_ref[:, :LANES], gk_ref[:, LANES:]
    for h in range(MLA_HEADS):
        lo = h * MLA_QK_PAD
        q_n = qfull[:, lo:lo + LANES]
        q_r = qfull[:, lo + LANES:lo + 2 * LANES] * ct + qsw[:, h * LANES:(h + 1) * LANES] * st
        ss = jnp.sum(q_n * q_n, axis=-1, keepdims=True) + jnp.sum(q_r * q_r, axis=-1, keepdims=True)
        r = lax.rsqrt(ss * (1.0 / MLA_QK) + EPS)
        q_ref[:, lo:lo + LANES] = (q_n * r * gq_n).astype(q_ref.dtype)
        q_ref[:, lo + LANES:lo + 2 * LANES] = (q_r * r * gq_r).astype(q_ref.dtype)
        k_n = knope[:, h * LANES:(h + 1) * LANES]
        ss = jnp.sum(k_n * k_n, axis=-1, keepdims=True) + kpe_ss
        r = lax.rsqrt(ss * (1.0 / MLA_QK) + EPS)
        k_ref[:, lo:lo + LANES] = (k_n * r * gk_n).astype(k_ref.dtype)
        k_ref[:, lo + LANES:lo + 2 * LANES] = (kpe_rot * r * gk_r).astype(k_ref.dtype)


def mla_prep(mix, cols, S, ct, st, lw):
    M = mix.shape[0]
    tm = _pick(S, (256, 128))
    ns = S // tm
    HQ = MLA_HEADS * MLA_QK_PAD
    HV = MLA_HEADS * 2 * MLA_V

    def col(width, off):
        assert off % width == 0
        return pl.BlockSpec((tm, width), lambda i: (i, off // width))

    def whole(a):
        return pl.BlockSpec(a.shape, lambda i: (0,) * a.ndim)

    consts = (lw["mla_qa_g"], lw["mla_kva_g"], lw["mla_wq"], lw["mla_wq_sw"], lw["mla_wk"], lw["mla_wv"],
              lw["mla_gq"], lw["mla_gk"])
    return pl.pallas_call(
        _mla_prep_kernel,
        grid=(M // tm,),
        in_specs=[col(MLA_Q_LORA, cols["qa"]), col(MLA_KV_LORA, cols["kva"]), col(LANES, cols["kpe"]),
                  pl.BlockSpec((tm, LANES), lambda i: (i % ns, 0)),
                  pl.BlockSpec((tm, LANES), lambda i: (i % ns, 0))] + [whole(c) for c in consts],
        out_specs=[pl.BlockSpec((tm, HQ), lambda i: (i, 0)),
                   pl.BlockSpec((tm, HQ), lambda i: (i, 0)),
                   pl.BlockSpec((tm, HV), lambda i: (i, 0))],
        out_shape=[jax.ShapeDtypeStruct((M, HQ), BF16),
                   jax.ShapeDtypeStruct((M, HQ), BF16),
                   jax.ShapeDtypeStruct((M, HV), BF16)],
        compiler_params=_params("parallel"),
        name="mla_prep",
    )(mix, mix, mix, ct, st, *consts)


def _qk(q, k):
    return lax.dot_general(q, k, (((1,), (1,)), ((), ())), preferred_element_type=F32)


def _dense_attn_kernel(q_ref, k_ref, v_ref, o_ref, *, n_sub, chunk):
    dv = o_ref.shape[1]
    ts = q_ref.shape[0] // n_sub
    seq = k_ref.shape[0]
    for t in range(n_sub):
        rows = slice(t * ts, (t + 1) * ts)
        q = q_ref[rows, :]
        pieces = [_qk(q, k_ref[c:c + chunk, :]) for c in range(0, seq, chunk)]
        mt = _lane_fold(pieces[0], jnp.maximum)
        for piece in pieces[1:]:
            mt = jnp.maximum(mt, _lane_fold(piece, jnp.maximum))
        m = jnp.max(mt, axis=-1, keepdims=True)
        o = jnp.zeros((ts, 2 * dv), F32)
        for c, piece in zip(range(0, seq, chunk), pieces):
            o = o + _dot(jnp.exp2(piece - m).astype(BF16), v_ref[c:c + chunk, :])
        o_ref[rows, :] = (o[:, :dv] / o[:, dv:]).astype(o_ref.dtype)


def dense_attention(q, k, v, B, S, H, dk, dv, n_sub=4):
    tq = _pick(S, (1024, 512, 256))
    chunk = _pick(S, (512, 256))
    q3, k3, v3 = (t.reshape(B, S, t.shape[1]) for t in (q, k, v))
    out = pl.pallas_call(
        functools.partial(_dense_attn_kernel, n_sub=n_sub, chunk=chunk),
        grid=(B, H, S // tq),
        in_specs=[pl.BlockSpec((None, tq, dk), lambda b, h, i: (b, i, h)),
                  pl.BlockSpec((None, S, dk), lambda b, h, i: (b, 0, h)),
                  pl.BlockSpec((None, S, 2 * dv), lambda b, h, i: (b, 0, h))],
        out_specs=pl.BlockSpec((None, tq, dv), lambda b, h, i: (b, i, h)),
        out_shape=jax.ShapeDtypeStruct((B, S, H * dv), BF16),
        compiler_params=_params("parallel", "parallel", "arbitrary"),
        name="dense_attention",
    )(q3, k3, v3)
    return out.reshape(B * S, H * dv)


def _lane_fold(x, op):
    out = x[:, :LANES]
    for j in range(1, x.shape[1] // LANES):
        out = op(out, x[:, j * LANES:(j + 1) * LANES])
    return out


def _band_kernel(sink_ref, q_ref, k_ref, v_ref, bias_ref, o_ref, *, tq, tk, radius, seq, n_chunks, hpb, rep):
    hb = pl.program_id(1)
    q0 = pl.program_id(2) * tq
    starts, tiles = [], []
    for c in range(n_chunks):
        start = q0 - radius + c * tk
        inside = jnp.logical_and(start >= 0, start + tk <= seq)
        starts.append(pl.multiple_of(jnp.clip(start, 0, seq - tk), tk))
        tiles.append(jnp.where(inside, c, n_chunks))
    for hh in range(hpb):
        qs = slice(hh * HEAD_DIM, (hh + 1) * HEAD_DIM)
        ks = slice((hh // rep) * HEAD_DIM, (hh // rep + 1) * HEAD_DIM)
        q = q_ref[:, qs]
        scores = [_qk(q, k_ref[pl.ds(sc, tk), ks]) + bias_ref[hh, t] for sc, t in zip(starts, tiles)]
        mt = _lane_fold(scores[0], jnp.maximum)
        for s in scores[1:]:
            mt = jnp.maximum(mt, _lane_fold(s, jnp.maximum))
        sink = sink_ref[hb * hpb + hh] * LOG2E
        m = jnp.maximum(jnp.max(mt, axis=-1, keepdims=True), sink)
        lt = jnp.zeros((tq, LANES), F32)
        acc = jnp.zeros((tq, HEAD_DIM), F32)
        for s, sc in zip(scores, starts):
            p = jnp.exp2(s - m)
            lt = lt + _lane_fold(p, jnp.add)
            acc = acc + _dot(p.astype(BF16), v_ref[pl.ds(sc, tk), ks])
        l = jnp.sum(lt, axis=-1, keepdims=True) + jnp.exp2(sink - m)
        o_ref[:, qs] = (acc / l).astype(o_ref.dtype)


def band_bias(slopes, mult_fn, tq, tk, radius):
    n_chunks = (tq + 2 * radius) // tk
    a = jnp.arange(tq)[None, :, None]
    c = jnp.arange(tk)[None, None, :]
    delta = (jnp.arange(n_chunks)[:, None, None] * tk - radius) + c - a
    mult = mult_fn(delta)
    dist = jnp.abs(delta).astype(F32)
    logm = jnp.log2(jnp.maximum(mult, 1).astype(F32))
    bias = logm[None] - (slopes * LOG2E)[:, None, None, None] * dist[None]
    bias = jnp.where(mult[None] > 0, bias, NEG_INF)
    dead = jnp.full((slopes.shape[0], 1, tq, tk), NEG_INF, F32)
    return jnp.concatenate([bias, dead], axis=1)


def dil_multiplicity(delta):
    m = jnp.zeros(delta.shape, jnp.int32)
    for window, dil in DIL_PATTERNS:
        m = m + ((delta % dil == 0) & (jnp.abs(delta) <= window // 2)).astype(jnp.int32)
    return m


def win_multiplicity(delta):
    return (jnp.abs(delta) <= WIN_RADIUS).astype(jnp.int32)


def band_attention(qarr, q_off, karr, k_off, varr, v_off, bias, sinks, B, S, H, rep, hpb, tq, tk, radius):
    n_chunks = bias.shape[1] - 1
    assert radius % tk == 0 and tq % tk == 0 and S % tq == 0 and S >= tk and hpb % rep == 0 and H % hpb == 0
    wq, wk = hpb * HEAD_DIM, (hpb // rep) * HEAD_DIM
    assert q_off % wq == 0 and k_off % wk == 0 and v_off % wk == 0
    qb, kb, vb = q_off // wq, k_off // wk, v_off // wk
    q3, k3, v3 = (t.reshape(B, S, t.shape[1]) for t in (qarr, karr, varr))
    kern = functools.partial(_band_kernel, tq=tq, tk=tk, radius=radius, seq=S, n_chunks=n_chunks, hpb=hpb, rep=rep)
    out = pl.pallas_call(
        kern,
        grid=(B, H // hpb, S // tq),
        in_specs=[pl.BlockSpec(memory_space=pltpu.SMEM),
                  pl.BlockSpec((None, tq, wq), lambda b, h, i: (b, i, qb + h)),
                  pl.BlockSpec((None, S, wk), lambda b, h, i: (b, 0, kb + h)),
                  pl.BlockSpec((None, S, wk), lambda b, h, i: (b, 0, vb + h)),
                  pl.BlockSpec((hpb, n_chunks + 1, tq, tk), lambda b, h, i: (h, 0, 0, 0))],
        out_specs=pl.BlockSpec((None, tq, wq), lambda b, h, i: (b, i, h)),
        out_shape=jax.ShapeDtypeStruct((B, S, H * HEAD_DIM), BF16),
        compiler_params=_params("parallel", "parallel", "arbitrary"),
        name="band_attention",
    )(sinks.astype(F32), q3, k3, v3, bias)
    return out.reshape(B * S, H * HEAD_DIM)


def _diff_kernel(slope_ref, lam_ref, q_ref, k_ref, v_ref, pos_ref, g_ref, o_ref, *, tq, lam_init, chunk, n_sub):
    h = pl.program_id(1)
    q0 = pl.program_id(2) * tq
    lp = lam_ref[...]
    lam = (jnp.exp(jnp.sum(lp[0:1] * lp[1:2], axis=-1, keepdims=True))
           - jnp.exp(jnp.sum(lp[2:3] * lp[3:4], axis=-1, keepdims=True)) + lam_init)
    slope = slope_ref[h]
    seq = k_ref.shape[0]
    kpos = pos_ref[...] * slope
    ts = tq // n_sub
    for t in range(n_sub):
        rows = slice(t * ts, (t + 1) * ts)
        qrow = (q0 + t * ts + lax.broadcasted_iota(jnp.int32, (ts, LANES), 0)).astype(F32) * slope
        qpos = jnp.concatenate([qrow] * (chunk // LANES), axis=1)
        q1, q2 = q_ref[rows, :DIFF_HD], q_ref[rows, DIFF_HD:]
        t1, t2, mt1, mt2 = [], [], None, None
        for c in range(0, seq, chunk):
            b = jnp.abs(kpos[:, c:c + chunk] - qpos)
            a1 = _qk(q1, k_ref[c:c + chunk, :DIFF_HD]) - b
            a2 = _qk(q2, k_ref[c:c + chunk, DIFF_HD:]) - b
            t1.append(a1)
            t2.append(a2)
            f1, f2 = _lane_fold(a1, jnp.maximum), _lane_fold(a2, jnp.maximum)
            mt1 = f1 if mt1 is None else jnp.maximum(mt1, f1)
            mt2 = f2 if mt2 is None else jnp.maximum(mt2, f2)
        m1 = jnp.max(mt1, axis=-1, keepdims=True)
        m2 = jnp.max(mt2, axis=-1, keepdims=True)
        lt1 = jnp.zeros((ts, LANES), F32)
        lt2 = jnp.zeros((ts, LANES), F32)
        o1 = jnp.zeros((ts, 2 * DIFF_HD), F32)
        o2 = jnp.zeros((ts, 2 * DIFF_HD), F32)
        for i, c in enumerate(range(0, seq, chunk)):
            p1, p2 = jnp.exp2(t1[i] - m1), jnp.exp2(t2[i] - m2)
            lt1 = lt1 + _lane_fold(p1, jnp.add)
            lt2 = lt2 + _lane_fold(p2, jnp.add)
            o1 = o1 + _dot(p1.astype(BF16), v_ref[c:c + chunk, :])
            o2 = o2 + _dot(p2.astype(BF16), v_ref[c:c + chunk, :])
        l1 = jnp.sum(lt1, axis=-1, keepdims=True)
        l2 = jnp.sum(lt2, axis=-1, keepdims=True)
        o = o1 * (1.0 / l1) - o2 * (lam / l2)
        r = lax.rsqrt(jnp.mean(o * o, axis=-1, keepdims=True) + EPS)
        o_ref[rows, :] = (o * r * g_ref[...] * (1.0 - lam_init)).astype(o_ref.dtype)


def diff_attention(mix, cols, slopes, lam_p, subln_g, B, S, layer):
    tq = _pick(S, (512, 256, 128))
    lam_init = 0.8 - 0.6 * math.exp(-0.3 * layer)
    W = 2 * DIFF_HD
    qb, kb, vb = (cols[n] // W for n in ("fq", "fk", "fv"))
    m3 = mix.reshape(B, S, mix.shape[1])
    pos = jnp.arange(S, dtype=F32).reshape(1, S)
    kern = functools.partial(_diff_kernel, tq=tq, lam_init=lam_init, chunk=_pick(S, (512, 256)),
                             n_sub=4 if S > 2048 else 2)
    out = pl.pallas_call(
        kern,
        grid=(B, DIFF_HEADS, S // tq),
        in_specs=[pl.BlockSpec(memory_space=pltpu.SMEM),
                  pl.BlockSpec((4, DIFF_HD), lambda b, h, i: (0, 0)),
                  pl.BlockSpec((None, tq, W), lambda b, h, i: (b, i, qb + h)),
                  pl.BlockSpec((None, S, W), lambda b, h, i: (b, 0, kb + h)),
                  pl.BlockSpec((None, S, W), lambda b, h, i: (b, 0, vb + h)),
                  pl.BlockSpec((1, S), lambda b, h, i: (0, 0)),
                  pl.BlockSpec((1, W), lambda b, h, i: (0, 0))],
        out_specs=pl.BlockSpec((None, tq, W), lambda b, h, i: (b, i, h)),
        out_shape=jax.ShapeDtypeStruct((B, S, DIFF_HEADS * W), BF16),
        compiler_params=_params("parallel", "parallel", "arbitrary"),
        name="diff_attention",
    )((slopes * LOG2E).astype(F32), lam_p.astype(F32), m3, m3, m3, pos, subln_g.reshape(1, W).astype(F32))
    return out.reshape(B * S, DIFF_HEADS * W)


def _mem_attn_kernel(x_ref, gm_ref, wq_ref, gq_ref, kv_ref, wo_ref, gf_ref, o_ref, hf_ref):
    def rms(t):
        return lax.rsqrt(jnp.mean(t * t, axis=-1, keepdims=True) + EPS)

    x = x_ref[...]
    hm = (x * rms(x) * gm_ref[...]).astype(BF16)
    qacc = _dot(hm, wq_ref[...])
    heads = []
    for h in range(MEM_HEADS):
        seg = slice(h * HEAD_DIM, (h + 1) * HEAD_DIM)
        y = qacc[:, seg]
        q = (y * rms(y) * gq_ref[:, seg]).astype(BF16)
        s = _qk(q, kv_ref[:, seg])
        p = jnp.exp2(s - jnp.max(s, axis=-1, keepdims=True))
        l = jnp.sum(p, axis=-1, keepdims=True)
        vseg = slice(MEM_W + h * HEAD_DIM, MEM_W + (h + 1) * HEAD_DIM)
        heads.append((_dot(p.astype(BF16), kv_ref[:, vseg]) / l).astype(BF16))
    x2 = x + _dot(jnp.concatenate(heads, axis=-1), wo_ref[...])
    o_ref[...] = x2
    hf_ref[...] = (x2 * rms(x2) * gf_ref[...]).astype(hf_ref.dtype)


def mem_attention(x, kv, lw, B, S):
    D = x.shape[1]
    Mt = kv.shape[0] // B
    tq = _pick(S, (256, 128))
    kv3 = kv.reshape(B, Mt, 2 * MEM_W)
    x3 = x.reshape(B, S, D)

    def row(n):
        return pl.BlockSpec((1, n), lambda b, i: (0, 0))

    out, hf = pl.pallas_call(
        _mem_attn_kernel,
        grid=(B, S // tq),
        in_specs=[pl.BlockSpec((None, tq, D), lambda b, i: (b, i, 0)),
                  row(D),
                  pl.BlockSpec((D, MEM_W), lambda b, i: (0, 0)),
                  row(MEM_W),
                  pl.BlockSpec((None, Mt, 2 * MEM_W), lambda b, i: (b, 0, 0)),
                  pl.BlockSpec((MEM_W, D), lambda b, i: (0, 0)),
                  row(D)],
        out_specs=[pl.BlockSpec((None, tq, D), lambda b, i: (b, i, 0)),
                   pl.BlockSpec((None, tq, D), lambda b, i: (b, i, 0))],
        out_shape=[jax.ShapeDtypeStruct((B, S, D), F32), jax.ShapeDtypeStruct((B, S, D), BF16)],
        compiler_params=_params("parallel", "arbitrary"),
        name="mem_attention",
    )(x3, lw["ln_mem_g"].reshape(1, D).astype(F32), lw["mem_wq"], lw["mem_q_gain"].reshape(1, MEM_W),
      kv3, lw["mem_wo"], lw["ln_ffn_g"].reshape(1, D).astype(F32))
    return out.reshape(B * S, D), hf.reshape(B * S, D)


def alibi_slopes(n):
    return 2.0 ** (-8.0 * jnp.arange(1, n + 1, dtype=F32) / n)


MIX_NAMES = ("qa", "kva", "kpe", "dq", "dk", "dv", "wq", "wk", "wv", "fq", "fk", "fv")


def _mix_layout():
    src = {n: (MIX_OFFSETS[i], MIX_OFFSETS[i + 1]) for i, n in enumerate(MIX_NAMES)}
    head = -(-(src["kpe"][0] + LANES) // MIX_TILE) * MIX_TILE
    shift = head - src["dq"][0]
    cols = {n: src[n][0] + (0 if n in ("qa", "kva", "kpe") else shift) for n in MIX_NAMES}
    total = -(-(MIX_COLS + shift) // MIX_TILE) * MIX_TILE
    return src, cols, head, shift, total


def _pack_kernel(src3_ref, o_ref, *, n_head, kpe_off, tn):
    src_ref = src3_ref.at[0]
    j = pl.program_id(0)
    if not n_head:
        o_ref[...] = src_ref[...].astype(o_ref.dtype)
        return

    @pl.when(j != n_head - 1)
    def _():
        o_ref[...] = src_ref[...].astype(o_ref.dtype)

    @pl.when(j == n_head - 1)
    def _():
        quarter = MLA_ROPE // 2
        x1 = src_ref[kpe_off:kpe_off + quarter, :].astype(o_ref.dtype)
        x2 = src_ref[kpe_off + quarter:kpe_off + 2 * quarter, :].astype(o_ref.dtype)
        if kpe_off:
            o_ref[:kpe_off, :] = src_ref[:kpe_off, :].astype(o_ref.dtype)
        for t, piece in enumerate((x1, x2, x2, x1)):
            o_ref[kpe_off + t * quarter:kpe_off + (t + 1) * quarter, :] = piece
        if kpe_off + LANES < tn:
            o_ref[kpe_off + LANES:, :] = jnp.zeros((tn - kpe_off - LANES, o_ref.shape[1]), o_ref.dtype)


def _cast_kernel(x_ref, o_ref):
    o_ref[...] = x_ref[...].astype(o_ref.dtype)


def cast_layer(w, l):
    _, R, C = w.shape
    tc = _pick(C, (2048, 1024, 512, 256, 128))
    tr = _pick(R, tuple(t for t in (2048, 1024, 512, 256, 128, 64, 8) if t * tc <= 2 ** 21))
    return pl.pallas_call(
        _cast_kernel,
        grid=(R // tr, C // tc),
        in_specs=[pl.BlockSpec((None, tr, tc), lambda i, j: (l, i, j))],
        out_specs=pl.BlockSpec((tr, tc), lambda i, j: (i, j)),
        out_shape=jax.ShapeDtypeStruct((R, C), BF16),
        compiler_params=_params("parallel", "arbitrary"),
        name="cast_layer",
    )(w)


def pack_shifted(src_t, l, n_out, shift, head_rows=0, kpe_row=0):
    _, C, K = src_t.shape
    tn = MIX_TILE
    n_head = head_rows // tn
    assert shift % 8 == 0 and head_rows % tn == 0 and (n_head == 0 or kpe_row // tn == n_head - 1)
    kern = functools.partial(_pack_kernel, n_head=n_head, kpe_off=kpe_row % tn, tn=tn)
    return pl.pallas_call(
        kern,
        grid=(n_out // tn,),
        in_specs=[pl.BlockSpec((pl.Element(1), pl.Element(tn), pl.Element(K)),
                               lambda j: (l, pl.multiple_of(jnp.where(j < n_head, j * tn, j * tn - shift), 8), 0))],
        out_specs=pl.BlockSpec((tn, K), lambda j: (j, 0)),
        out_shape=jax.ShapeDtypeStruct((n_out, K), BF16),
        compiler_params=_params("arbitrary"),
        name="pack_shifted",
    )(src_t)


def pack_layer(p, l):
    src, cols, head_w, shift, total = _mix_layout()
    half = MLA_ROPE // 2
    qscale = HEAD_DIM ** -0.5 * LOG2E

    def tile(g, n):
        return jnp.tile(g.astype(F32), n)

    norm_gain = {
        "dq": tile(p["dil_qk_g"][l, 0], DIL_HEADS) * qscale, "dk": tile(p["dil_qk_g"][l, 1], DIL_HEADS),
        "fq": tile(p["diff_qk_g"][l, 0], 2 * DIFF_HEADS) * (DIFF_HD ** -0.5 * LOG2E),
        "fk": tile(p["diff_qk_g"][l, 1], 2 * DIFF_HEADS),
        "wq": tile(p["win_qk_g"][l, 0], WIN_Q_HEADS) * qscale, "wk": tile(p["win_qk_g"][l, 1], WIN_KV_HEADS),
    }
    gain = jnp.ones((total,), F32)
    flag = np.zeros((total,), np.float32)
    for n, g in norm_gain.items():
        gain = gain.at[cols[n]:cols[n] + g.shape[0]].set(g)
        flag[cols[n]:cols[n] + g.shape[0]] = 1.0

    w_in_t = jnp.swapaxes(p["w_in"], 1, 2)
    w_mix = pack_shifted(w_in_t, l, total, shift, head_rows=head_w, kpe_row=src["kpe"][0])
    w_gate = pack_shifted(w_in_t, l, N_BRANCH * D_MODEL, -MIX_COLS)

    wq3 = p["mla_wq_up"][l].reshape(MLA_Q_LORA, MLA_HEADS, MLA_QK)
    nope, x1, x2 = wq3[:, :, :MLA_NOPE], wq3[:, :, MLA_NOPE:MLA_NOPE + half], wq3[:, :, MLA_NOPE + half:]
    z = jnp.zeros((MLA_Q_LORA, MLA_HEADS, LANES - MLA_ROPE), F32)
    wq_full = jnp.concatenate([nope, x1, x2, z], axis=-1).reshape(MLA_Q_LORA, MLA_HEADS * MLA_QK_PAD)
    wq_sw = jnp.concatenate([x2, x1, z], axis=-1).reshape(MLA_Q_LORA, MLA_HEADS * LANES)
    wkv3 = p["mla_wkv_up"][l].reshape(MLA_KV_LORA, MLA_HEADS, MLA_NOPE + MLA_V)
    zg = jnp.zeros((LANES - MLA_ROPE,), F32)
    qk_g = p["mla_qk_g"][l].astype(F32)

    return {
        "ln_mix_g": p["ln_mix_g"][l],
        "w_mix": w_mix, "w_gate": w_gate, "mix_gain": gain, "mix_flag": flag,
        "mla_qa_g": p["mla_qa_g"][l].reshape(1, -1).astype(F32),
        "mla_kva_g": p["mla_kva_g"][l].reshape(1, -1).astype(F32),
        "mla_wq": wq_full.astype(BF16), "mla_wq_sw": wq_sw.astype(BF16),
        "mla_wk": wkv3[:, :, :MLA_NOPE].reshape(MLA_KV_LORA, -1).astype(BF16),
        "mla_wv": wkv3[:, :, MLA_NOPE:].reshape(MLA_KV_LORA, -1).astype(BF16),
        "mla_gq": (jnp.concatenate([qk_g[0], zg]) * (MLA_QK ** -0.5 * LOG2E)).reshape(1, -1),
        "mla_gk": jnp.concatenate([qk_g[1], zg]).reshape(1, -1),
        "win_sink": p["win_sink"][l], "diff_lambda": p["diff_lambda"][l], "diff_subln_g": p["diff_subln_g"][l],
        "w_branch": cast_layer(p["w_branch"].reshape(DEPTH, N_BRANCH * BRANCH_W, D_MODEL), l).reshape(
            N_BRANCH, BRANCH_W, D_MODEL),
        "w_out": cast_layer(p["w_out"], l),
        "ln_mem_g": p["ln_mem_g"][l], "mem_ln_g": p["mem_ln_g"][l],
        "mem_wq": cast_layer(p["mem_wq"], l), "mem_wkv": cast_layer(p["mem_wkv"], l),
        "mem_q_gain": tile(p["mem_qk_g"][l, 0], MEM_HEADS) * (HEAD_DIM ** -0.5 * LOG2E),
        "mem_kv_gain": jnp.concatenate([tile(p["mem_qk_g"][l, 1], MEM_HEADS), jnp.ones((MEM_W,), F32)]),
        "mem_kv_flag": np.concatenate([np.ones((MEM_W,), np.float32), np.zeros((MEM_W,), np.float32)]),
        "mem_wo": cast_layer(p["mem_wo"], l),
        "ln_ffn_g": p["ln_ffn_g"][l],
        "ffn_w_in": cast_layer(p["ffn_w_in"], l), "ffn_w_out": cast_layer(p["ffn_w_out"], l),
    }


def rotary_tables(S):
    half = MLA_ROPE // 2
    inv_freq = ROPE_THETA ** (-jnp.arange(half, dtype=F32) / half)
    ang = jnp.arange(S, dtype=F32)[:, None] * inv_freq[None, :]
    cos, sin = jnp.cos(ang), jnp.sin(ang)
    z = jnp.zeros((S, LANES - MLA_ROPE), F32)
    return jnp.concatenate([cos, cos, z], axis=1), jnp.concatenate([-sin, sin, z], axis=1)


def _trunk(x, mem, layers, tables):
    B, S, D = x.shape
    M = B * S
    _, cols, _, _, _ = _mix_layout()
    ct, st = rotary_tables(S)
    xf = x.reshape(M, D)
    memf = mem.reshape(-1, D)
    dil_tq = _pick(S, (256,))
    for l, lw in enumerate(layers):
        h = rmsnorm(xf, lw["ln_mix_g"])
        mix = mm_segnorm(h, lw["w_mix"], lw["mix_gain"], lw["mix_flag"], w_output_major=True,
                         tm_prefs=(2048, 1024, 512, 256), tn_prefs=(MIX_TILE,))
        gates = mm_sigmoid(h, lw["w_gate"])

        q, k, v = mla_prep(mix, cols, S, ct, st, lw)
        o_mla = dense_attention(q, k, v, B, S, MLA_HEADS, MLA_QK_PAD, MLA_V)
        o_dil = band_attention(mix, cols["dq"], mix, cols["dk"], mix, cols["dv"], tables["dil_bias"],
                               jnp.full((DIL_HEADS,), NEG_INF, F32), B, S, DIL_HEADS, 1, DIL_HEADS // 2,
                               dil_tq, dil_tq, DIL_RADIUS)
        o_win = band_attention(mix, cols["wq"], mix, cols["wk"], mix, cols["wv"], tables["win_bias"],
                               lw["win_sink"], B, S, WIN_Q_HEADS, WIN_Q_HEADS // WIN_KV_HEADS, WIN_Q_HEADS,
                               2 * WIN_RADIUS, WIN_RADIUS, WIN_RADIUS)
        o_diff = diff_attention(mix, cols, tables["slopes_diff"], lw["diff_lambda"], lw["diff_subln_g"], B, S, l)

        merged = branch_merge((o_mla, o_dil, o_win, o_diff), lw["w_branch"], gates)
        xf = mm_residual(merged, lw["w_out"], xf)

        kvm = mm_segnorm(rmsnorm(memf, lw["mem_ln_g"]), lw["mem_wkv"], lw["mem_kv_gain"], lw["mem_kv_flag"])
        xf, hf = mem_attention(xf, kvm, lw, B, S)

        hid = mm_swiglu(hf, lw["ffn_w_in"], FFN_HIDDEN)
        xf = mm_residual(hid, lw["ffn_w_out"], xf, tm_prefs=(512, 256, 128), tn_prefs=(512, 256, 128))
    return xf.reshape(B, S, D)


def kernel(x_prompt, x_sample, mem_prompt, mem_sample, ln_mix_g, w_in, mla_qa_g, mla_kva_g, mla_wq_up,
           mla_wkv_up, mla_qk_g, dil_qk_g, win_qk_g, win_sink, diff_qk_g, diff_lambda, diff_subln_g,
           w_branch, w_out, ln_mem_g, mem_ln_g, mem_wq, mem_wkv, mem_qk_g, mem_wo, ln_ffn_g, ffn_w_in,
           ffn_w_out):
    p = dict(ln_mix_g=ln_mix_g, w_in=w_in, mla_qa_g=mla_qa_g, mla_kva_g=mla_kva_g, mla_wq_up=mla_wq_up,
             mla_wkv_up=mla_wkv_up, mla_qk_g=mla_qk_g, dil_qk_g=dil_qk_g, win_qk_g=win_qk_g,
             win_sink=win_sink, diff_qk_g=diff_qk_g, diff_lambda=diff_lambda, diff_subln_g=diff_subln_g,
             w_branch=w_branch, w_out=w_out, ln_mem_g=ln_mem_g, mem_ln_g=mem_ln_g, mem_wq=mem_wq,
             mem_wkv=mem_wkv, mem_qk_g=mem_qk_g, mem_wo=mem_wo, ln_ffn_g=ln_ffn_g, ffn_w_in=ffn_w_in,
             ffn_w_out=ffn_w_out)
    layers = [pack_layer(p, l) for l in range(DEPTH)]
    tables = {
        "dil_bias": band_bias(alibi_slopes(DIL_HEADS), dil_multiplicity, 256, 256, DIL_RADIUS),
        "win_bias": band_bias(alibi_slopes(WIN_Q_HEADS), win_multiplicity, 2 * WIN_RADIUS, WIN_RADIUS,
                              WIN_RADIUS),
        "slopes_diff": alibi_slopes(DIFF_HEADS),
    }
    y_prompt = _trunk(x_prompt, mem_prompt, layers, tables)
    y_sample = _trunk(x_sample, mem_sample, layers, tables)
    return (y_prompt, y_sample)
```

```python
import functools
import math

import jax
import jax.numpy as jnp
import numpy as np
from jax import lax
from jax.experimental import pallas as pl
from jax.experimental.pallas import tpu as pltpu

F32 = jnp.float32
BF16 = jnp.bfloat16

D_MODEL = 4096
DEPTH = 2
EPS = 1e-6
NEG_INF = -1e30
N_BRANCH = 4
BRANCH_W = D_MODEL // N_BRANCH
HEAD_DIM = 128

MLA_NOPE = 128
MLA_ROPE = 64
MLA_V = 128
MLA_HEADS = BRANCH_W // MLA_V
MLA_Q_LORA = D_MODEL // 4
MLA_KV_LORA = D_MODEL // 8
MLA_QK = MLA_NOPE + MLA_ROPE
MLA_QK_PAD = 256
ROPE_THETA = 10000.0

DIL_HEADS = BRANCH_W // HEAD_DIM
DIL_PATTERNS = ((128, 1), (512, 4), (2048, 16))
DIL_RADIUS = max(w // 2 for w, _ in DIL_PATTERNS)

WIN_Q_HEADS = BRANCH_W // HEAD_DIM
WIN_KV_HEADS = WIN_Q_HEADS // 4
WIN_RADIUS = 128

DIFF_HD = 128
DIFF_HEADS = BRANCH_W // (2 * DIFF_HD)

MEM_HEADS = 4
MEM_W = MEM_HEADS * HEAD_DIM

FFN_HIDDEN = -(-8 * D_MODEL // (3 * 256)) * 256

MIX_SPLITS = (MLA_Q_LORA, MLA_KV_LORA, MLA_ROPE,
              BRANCH_W, BRANCH_W, BRANCH_W,
              BRANCH_W, WIN_KV_HEADS * HEAD_DIM, WIN_KV_HEADS * HEAD_DIM,
              BRANCH_W, BRANCH_W, BRANCH_W)
MIX_COLS = sum(MIX_SPLITS)
MIX_OFFSETS = tuple(int(o) for o in np.cumsum((0,) + MIX_SPLITS))

LOG2E = math.log2(math.e)
LANES = 128
MIX_TILE = 512
VMEM_LIMIT_BYTES = 56 * 1024 * 1024


def _pick(n, prefs):
    for p in prefs:
        if n % p == 0:
            return p
    raise ValueError(f"no tile in {prefs} divides {n}")


def _params(*sem):
    return pltpu.CompilerParams(dimension_semantics=sem, vmem_limit_bytes=VMEM_LIMIT_BYTES)


def _rmsnorm_kernel(x_ref, g_ref, o_ref):
    x = x_ref[...].astype(F32)
    ms = jnp.mean(x * x, axis=-1, keepdims=True)
    o_ref[...] = (x * lax.rsqrt(ms + EPS) * g_ref[...]).astype(o_ref.dtype)


def rmsnorm(x, g):
    M, D = x.shape
    tm = _pick(M, (512, 256, 128, 64, 8))
    return pl.pallas_call(
        _rmsnorm_kernel,
        grid=(M // tm,),
        in_specs=[pl.BlockSpec((tm, D), lambda i: (i, 0)),
                  pl.BlockSpec((1, D), lambda i: (0, 0))],
        out_specs=pl.BlockSpec((tm, D), lambda i: (i, 0)),
        out_shape=jax.ShapeDtypeStruct((M, D), BF16),
        compiler_params=_params("parallel"),
        name="rmsnorm",
    )(x, g.reshape(1, D).astype(F32))


def _dot(a, b):
    return jnp.dot(a, b, preferred_element_type=F32)


MXU_COLS = 256


def _sigmoid(x):
    return 0.5 * jnp.tanh(0.5 * x) + 0.5


def _col_slabs(n):
    w = MXU_COLS if n % MXU_COLS == 0 else n
    return [slice(c, c + w) for c in range(0, n, w)]


def _dot_nt(a, bt):
    return lax.dot_general(a, bt, (((1,), (1,)), ((), ())), preferred_element_type=F32)


def _mm_sigmoid_kernel(a_ref, bt_ref, o_ref):
    for cs in _col_slabs(o_ref.shape[1]):
        o_ref[:, cs] = _sigmoid(_dot_nt(a_ref[...], bt_ref[cs, :])).astype(o_ref.dtype)


def _mm_segnorm_kernel(tile_norm_ref, a_ref, b_ref, g_ref, f_ref, o_ref, *, w_output_major):
    def slab(cs):
        return _dot_nt(a_ref[...], b_ref[cs, :]) if w_output_major else _dot(a_ref[...], b_ref[:, cs])

    has_norm = tile_norm_ref[pl.program_id(1)] > 0

    @pl.when(has_norm)
    def _():
        for cs in _col_slabs(o_ref.shape[1]):
            acc = slab(cs)
            for c in range(0, acc.shape[1], LANES):
                seg = slice(cs.start + c, cs.start + c + LANES)
                y = acc[:, c:c + LANES]
                r = lax.rsqrt(jnp.mean(y * y, axis=-1, keepdims=True) + EPS)
                mult = jnp.where(f_ref[:, seg] > 0.0, r, 1.0) * g_ref[:, seg]
                o_ref[:, seg] = (y * mult).astype(o_ref.dtype)

    @pl.when(jnp.logical_not(has_norm))
    def _():
        for cs in _col_slabs(o_ref.shape[1]):
            o_ref[:, cs] = (slab(cs) * g_ref[:, cs]).astype(o_ref.dtype)


def _mm_residual_kernel(a_ref, b_ref, r_ref, o_ref):
    for cs in _col_slabs(o_ref.shape[1]):
        o_ref[:, cs] = r_ref[:, cs] + _dot(a_ref[...], b_ref[:, cs])


def _mm_swiglu_kernel(a_ref, bg_ref, bu_ref, o_ref):
    half = a_ref.shape[0] // 2
    for rows in (slice(0, half), slice(half, 2 * half)):
        a = a_ref[rows, :]
        g = _dot(a, bg_ref[...])
        u = _dot(a, bu_ref[...])
        o_ref[rows, :] = (g * _sigmoid(g) * u).astype(o_ref.dtype)


def mm_sigmoid(a, wt, tm_prefs=(2048, 1024, 512, 256, 128), tn_prefs=(512, 256, 128)):
    M, K = a.shape
    N = wt.shape[0]
    tm, tn = _pick(M, tm_prefs), _pick(N, tn_prefs)
    return pl.pallas_call(
        _mm_sigmoid_kernel,
        grid=(M // tm, N // tn),
        in_specs=[pl.BlockSpec((tm, K), lambda i, j: (i, 0)),
                  pl.BlockSpec((tn, K), lambda i, j: (j, 0))],
        out_specs=pl.BlockSpec((tm, tn), lambda i, j: (i, j)),
        out_shape=jax.ShapeDtypeStruct((M, N), BF16),
        compiler_params=_params("parallel", "arbitrary"),
        name="mm_sigmoid",
    )(a, wt)


def mm_segnorm(a, w, gain, flag, w_output_major=False, tm_prefs=(1024, 512, 256, 128), tn_prefs=(512, 256, 128)):
    M, K = a.shape
    N = gain.shape[0]
    tm, tn = _pick(M, tm_prefs), _pick(N, tn_prefs)
    flag = np.asarray(flag, np.float32)
    tile_norm = jnp.asarray(flag.reshape(N // tn, tn).max(axis=1) > 0, jnp.int32)
    w_spec = (pl.BlockSpec((tn, K), lambda i, j: (j, 0)) if w_output_major
              else pl.BlockSpec((K, tn), lambda i, j: (0, j)))
    return pl.pallas_call(
        functools.partial(_mm_segnorm_kernel, w_output_major=w_output_major),
        grid=(M // tm, N // tn),
        in_specs=[pl.BlockSpec(memory_space=pltpu.SMEM),
                  pl.BlockSpec((tm, K), lambda i, j: (i, 0)),
                  w_spec,
                  pl.BlockSpec((1, tn), lambda i, j: (0, j)),
                  pl.BlockSpec((1, tn), lambda i, j: (0, j))],
        out_specs=pl.BlockSpec((tm, tn), lambda i, j: (i, j)),
        out_shape=jax.ShapeDtypeStruct((M, N), BF16),
        compiler_params=_params("parallel", "arbitrary"),
        name="mm_segnorm",
    )(tile_norm, a, w, gain.reshape(1, N).astype(F32), jnp.asarray(flag).reshape(1, N))


def mm_residual(a, w, res, tm_prefs=(1024, 512, 256, 128), tn_prefs=(512, 256, 128)):
    M, K = a.shape
    N = w.shape[1]
    tm, tn = _pick(M, tm_prefs), _pick(N, tn_prefs)
    return pl.pallas_call(
        _mm_residual_kernel,
        grid=(M // tm, N // tn),
        in_specs=[pl.BlockSpec((tm, K), lambda i, j: (i, 0)),
                  pl.BlockSpec((K, tn), lambda i, j: (0, j)),
                  pl.BlockSpec((tm, tn), lambda i, j: (i, j))],
        out_specs=pl.BlockSpec((tm, tn), lambda i, j: (i, j)),
        out_shape=jax.ShapeDtypeStruct((M, N), F32),
        compiler_params=_params("parallel", "arbitrary"),
        name="mm_residual",
    )(a, w, res)


def mm_swiglu(a, w, hidden):
    M, K = a.shape
    tm = _pick(M, (2048, 1024, 512, 256, 128))
    tn = _pick(hidden, (256, 128))
    nj = hidden // tn
    return pl.pallas_call(
        _mm_swiglu_kernel,
        grid=(M // tm, nj),
        in_specs=[pl.BlockSpec((tm, K), lambda i, j: (i, 0)),
                  pl.BlockSpec((K, tn), lambda i, j: (0, j)),
                  pl.BlockSpec((K, tn), lambda i, j: (0, j + nj))],
        out_specs=pl.BlockSpec((tm, tn), lambda i, j: (i, j)),
        out_shape=jax.ShapeDtypeStruct((M, hidden), BF16),
        compiler_params=_params("parallel", "arbitrary"),
        name="mm_swiglu",
    )(a, w, w)


def _branch_kernel(oa_ref, ob_ref, oc_ref, od_ref, w_ref, ga_ref, gb_ref, gc_ref, gd_ref, o_ref):
    for cs in _col_slabs(o_ref.shape[1]):
        acc = ga_ref[:, cs].astype(F32) * _dot(oa_ref[...], w_ref[0, :, cs])
        acc += gb_ref[:, cs].astype(F32) * _dot(ob_ref[...], w_ref[1, :, cs])
        acc += gc_ref[:, cs].astype(F32) * _dot(oc_ref[...], w_ref[2, :, cs])
        acc += gd_ref[:, cs].astype(F32) * _dot(od_ref[...], w_ref[3, :, cs])
        o_ref[:, cs] = acc.astype(o_ref.dtype)


def branch_merge(outs, w_branch, gates):
    M, W = outs[0].shape
    D = w_branch.shape[2]
    tm = _pick(M, (1024, 512, 256, 128))
    tn = _pick(D, (1024, 512, 256, 128))
    nj = D // tn
    o_spec = pl.BlockSpec((tm, W), lambda i, j: (i, 0))
    g_specs = [pl.BlockSpec((tm, tn), functools.partial(lambda i, j, b: (i, b * nj + j), b=b))
               for b in range(N_BRANCH)]
    return pl.pallas_call(
        _branch_kernel,
        grid=(M // tm, nj),
        in_specs=[o_spec] * N_BRANCH + [pl.BlockSpec((N_BRANCH, W, tn), lambda i, j: (0, 0, j))] + g_specs,
        out_specs=pl.BlockSpec((tm, tn), lambda i, j: (i, j)),
        out_shape=jax.ShapeDtypeStruct((M, D), BF16),
        compiler_params=_params("parallel", "arbitrary"),
        name="branch_merge",
    )(*outs, w_branch, gates, gates, gates, gates)


def _mla_prep_kernel(qa_ref, kva_ref, kpe_ref, ct_ref, st_ref, qag_ref, kvag_ref,
                     wq_ref, wqs_ref, wk_ref, wv_ref, gq_ref, gk_ref,
                     q_ref, k_ref, v_ref):
    def norm(x, g):
        return (x * lax.rsqrt(jnp.mean(x * x, axis=-1, keepdims=True) + EPS) * g).astype(BF16)

    ct = ct_ref[...]
    st = st_ref[...]
    qn = norm(qa_ref[...].astype(F32), qag_ref[...])
    kvn = norm(kva_ref[...].astype(F32), kvag_ref[...])
    qfull = _dot(qn, wq_ref[...])
    qsw = _dot(qn, wqs_ref[...])
    knope = _dot(kvn, wk_ref[...])
    vals = _dot(kvn, wv_ref[...])
    ones = jnp.ones((vals.shape[0], MLA_V), v_ref.dtype)
    for h in range(MLA_HEADS):
        v_ref[:, 2 * h * MLA_V:(2 * h + 1) * MLA_V] = vals[:, h * MLA_V:(h + 1) * MLA_V].astype(v_ref.dtype)
        v_ref[:, (2 * h + 1) * MLA_V:(2 * h + 2) * MLA_V] = ones

    kpe = kpe_ref[...].astype(F32)
    kpe_rot = kpe * ct + pltpu.roll(kpe, 2 * (MLA_ROPE // 2), 1) * st
    kpe_ss = jnp.sum(kpe_rot * kpe_rot, axis=-1, keepdims=True)
    gq_n, gq_r = gq_ref[:, :LANES], gq_ref[:, LANES:]
    gk_n, gk_r = gk_ref[:, :LANES], gk_ref[:, LANES:]
    for h in range(MLA_HEADS):
        lo = h * MLA_QK_PAD
        q_n = qfull[:, lo:lo + LANES]
        q_r = qfull[:, lo + LANES:lo + 2 * LANES] * ct + qsw[:, h * LANES:(h + 1) * LANES] * st
        ss = jnp.sum(q_n * q_n, axis=-1, keepdims=True) + jnp.sum(q_r * q_r, axis=-1, keepdims=True)
        r = lax.rsqrt(ss * (1.0 / MLA_QK) + EPS)
        q_ref[:, lo:lo + LANES] = (q_n * r * gq_n).astype(q_ref.dtype)
        q_ref[:, lo + LANES:lo + 2 * LANES] = (q_r * r * gq_r).astype(q_ref.dtype)
        k_n = knope[:, h * LANES:(h + 1) * LANES]
        ss = jnp.sum(k_n * k_n, axis=-1, keepdims=True) + kpe_ss
        r = lax.rsqrt(ss * (1.0 / MLA_QK) + EPS)
        k_ref[:, lo:lo + LANES] = (k_n * r * gk_n).astype(k_ref.dtype)
        k_ref[:, lo + LANES:lo + 2 * LANES] = (kpe_rot * r * gk_r).astype(k_ref.dtype)


def mla_prep(mix, cols, S, ct, st, lw):
    M = mix.shape[0]
    tm = _pick(S, (256, 128))
    ns = S // tm
    HQ = MLA_HEADS * MLA_QK_PAD
    HV = MLA_HEADS * 2 * MLA_V

    def col(width, off):
        assert off % width == 0
        return pl.BlockSpec((tm, width), lambda i: (i, off // width))

    def whole(a):
        return pl.BlockSpec(a.shape, lambda i: (0,) * a.ndim)

    consts = (lw["mla_qa_g"], lw["mla_kva_g"], lw["mla_wq"], lw["mla_wq_sw"], lw["mla_wk"], lw["mla_wv"],
              lw["mla_gq"], lw["mla_gk"])
    return pl.pallas_call(
        _mla_prep_kernel,
        grid=(M // tm,),
        in_specs=[col(MLA_Q_LORA, cols["qa"]), col(MLA_KV_LORA, cols["kva"]), col(LANES, cols["kpe"]),
                  pl.BlockSpec((tm, LANES), lambda i: (i % ns, 0)),
                  pl.BlockSpec((tm, LANES), lambda i: (i % ns, 0))] + [whole(c) for c in consts],
        out_specs=[pl.BlockSpec((tm, HQ), lambda i: (i, 0)),
                   pl.BlockSpec((tm, HQ), lambda i: (i, 0)),
                   pl.BlockSpec((tm, HV), lambda i: (i, 0))],
        out_shape=[jax.ShapeDtypeStruct((M, HQ), BF16),
                   jax.ShapeDtypeStruct((M, HQ), BF16),
                   jax.ShapeDtypeStruct((M, HV), BF16)],
        compiler_params=_params("parallel"),
        name="mla_prep",
    )(mix, mix, mix, ct, st, *consts)


def _qk(q, k):
    return lax.dot_general(q, k, (((1,), (1,)), ((), ())), preferred_element_type=F32)


def _dense_attn_kernel(q_ref, k_ref, v_ref, o_ref, *, n_sub, chunk):
    dv = o_ref.shape[1]
    ts = q_ref.shape[0] // n_sub
    seq = k_ref.shape[0]
    for t in range(n_sub):
        rows = slice(t * ts, (t + 1) * ts)
        q = q_ref[rows, :]
        pieces = [_qk(q, k_ref[c:c + chunk, :]) for c in range(0, seq, chunk)]
        mt = _lane_fold(pieces[0], jnp.maximum)
        for piece in pieces[1:]:
            mt = jnp.maximum(mt, _lane_fold(piece, jnp.maximum))
        m = jnp.max(mt, axis=-1, keepdims=True)
        o = jnp.zeros((ts, 2 * dv), F32)
        for c, piece in zip(range(0, seq, chunk), pieces):
            o = o + _dot(jnp.exp2(piece - m).astype(BF16), v_ref[c:c + chunk, :])
        o_ref[rows, :] = (o[:, :dv] / o[:, dv:]).astype(o_ref.dtype)


def dense_attention(q, k, v, B, S, H, dk, dv, n_sub=4):
    tq = _pick(S, (1024, 512, 256))
    chunk = _pick(S, (512, 256))
    q3, k3, v3 = (t.reshape(B, S, t.shape[1]) for t in (q, k, v))
    out = pl.pallas_call(
        functools.partial(_dense_attn_kernel, n_sub=n_sub, chunk=chunk),
        grid=(B, H, S // tq),
        in_specs=[pl.BlockSpec((None, tq, dk), lambda b, h, i: (b, i, h)),
                  pl.BlockSpec((None, S, dk), lambda b, h, i: (b, 0, h)),
                  pl.BlockSpec((None, S, 2 * dv), lambda b, h, i: (b, 0, h))],
        out_specs=pl.BlockSpec((None, tq, dv), lambda b, h, i: (b, i, h)),
        out_shape=jax.ShapeDtypeStruct((B, S, H * dv), BF16),
        compiler_params=_params("parallel", "parallel", "arbitrary"),
        name="dense_attention",
    )(q3, k3, v3)
    return out.reshape(B * S, H * dv)


def _lane_fold(x, op):
    out = x[:, :LANES]
    for j in range(1, x.shape[1] // LANES):
        out = op(out, x[:, j * LANES:(j + 1) * LANES])
    return out


def _band_kernel(sink_ref, q_ref, k_ref, v_ref, bias_ref, o_ref, *, tq, tk, radius, seq, n_chunks, hpb, rep):
    hb = pl.program_id(1)
    q0 = pl.program_id(2) * tq
    starts, tiles = [], []
    for c in range(n_chunks):
        start = q0 - radius + c * tk
        inside = jnp.logical_and(start >= 0, start + tk <= seq)
        starts.append(pl.multiple_of(jnp.clip(start, 0, seq - tk), tk))
        tiles.append(jnp.where(inside, c, n_chunks))
    for hh in range(hpb):
        qs = slice(hh * HEAD_DIM, (hh + 1) * HEAD_DIM)
        ks = slice((hh // rep) * HEAD_DIM, (hh // rep + 1) * HEAD_DIM)
        q = q_ref[:, qs]
        scores = [_qk(q, k_ref[pl.ds(sc, tk), ks]) + bias_ref[hh, t] for sc, t in zip(starts, tiles)]
        mt = _lane_fold(scores[0], jnp.maximum)
        for s in scores[1:]:
            mt = jnp.maximum(mt, _lane_fold(s, jnp.maximum))
        sink = sink_ref[hb * hpb + hh] * LOG2E
        m = jnp.maximum(jnp.max(mt, axis=-1, keepdims=True), sink)
        lt = jnp.zeros((tq, LANES), F32)
        acc = jnp.zeros((tq, HEAD_DIM), F32)
        for s, sc in zip(scores, starts):
            p = jnp.exp2(s - m)
            lt = lt + _lane_fold(p, jnp.add)
            acc = acc + _dot(p.astype(BF16), v_ref[pl.ds(sc, tk), ks])
        l = jnp.sum(lt, axis=-1, keepdims=True) + jnp.exp2(sink - m)
        o_ref[:, qs] = (acc / l).astype(o_ref.dtype)


def band_bias(slopes, mult_fn, tq, tk, radius):
    n_chunks = (tq + 2 * radius) // tk
    a = jnp.arange(tq)[None, :, None]
    c = jnp.arange(tk)[None, None, :]
    delta = (jnp.arange(n_chunks)[:, None, None] * tk - radius) + c - a
    mult = mult_fn(delta)
    dist = jnp.abs(delta).astype(F32)
    logm = jnp.log2(jnp.maximum(mult, 1).astype(F32))
    bias = logm[None] - (slopes * LOG2E)[:, None, None, None] * dist[None]
    bias = jnp.where(mult[None] > 0, bias, NEG_INF)
    dead = jnp.full((slopes.shape[0], 1, tq, tk), NEG_INF, F32)
    return jnp.concatenate([bias, dead], axis=1)


def dil_multiplicity(delta):
    m = jnp.zeros(delta.shape, jnp.int32)
    for window, dil in DIL_PATTERNS:
        m = m + ((delta % dil == 0) & (jnp.abs(delta) <= window // 2)).astype(jnp.int32)
    return m


def win_multiplicity(delta):
    return (jnp.abs(delta) <= WIN_RADIUS).astype(jnp.int32)


def band_attention(qarr, q_off, karr, k_off, varr, v_off, bias, sinks, B, S, H, rep, hpb, tq, tk, radius):
    n_chunks = bias.shape[1] - 1
    assert radius % tk == 0 and tq % tk == 0 and S % tq == 0 and S >= tk and hpb % rep == 0 and H % hpb == 0
    wq, wk = hpb * HEAD_DIM, (hpb // rep) * HEAD_DIM
    assert q_off % wq == 0 and k_off % wk == 0 and v_off % wk == 0
    qb, kb, vb = q_off // wq, k_off // wk, v_off // wk
    q3, k3, v3 = (t.reshape(B, S, t.shape[1]) for t in (qarr, karr, varr))
    kern = functools.partial(_band_kernel, tq=tq, tk=tk, radius=radius, seq=S, n_chunks=n_chunks, hpb=hpb, rep=rep)
    out = pl.pallas_call(
        kern,
        grid=(B, H // hpb, S // tq),
        in_specs=[pl.BlockSpec(memory_space=pltpu.SMEM),
                  pl.BlockSpec((None, tq, wq), lambda b, h, i: (b, i, qb + h)),
                  pl.BlockSpec((None, S, wk), lambda b, h, i: (b, 0, kb + h)),
                  pl.BlockSpec((None, S, wk), lambda b, h, i: (b, 0, vb + h)),
                  pl.BlockSpec((hpb, n_chunks + 1, tq, tk), lambda b, h, i: (h, 0, 0, 0))],
        out_specs=pl.BlockSpec((None, tq, wq), lambda b, h, i: (b, i, h)),
        out_shape=jax.ShapeDtypeStruct((B, S, H * HEAD_DIM), BF16),
        compiler_params=_params("parallel", "parallel", "arbitrary"),
        name="band_attention",
    )(sinks.astype(F32), q3, k3, v3, bias)
    return out.reshape(B * S, H * HEAD_DIM)


def _diff_kernel(slope_ref, lam_ref, q_ref, k_ref, v_ref, pos_ref, g_ref, o_ref, *, tq, lam_init, chunk, n_sub):
    h = pl.program_id(1)
    q0 = pl.program_id(2) * tq
    lp = lam_ref[...]
    lam = (jnp.exp(jnp.sum(lp[0:1] * lp[1:2], axis=-1, keepdims=True))
           - jnp.exp(jnp.sum(lp[2:3] * lp[3:4], axis=-1, keepdims=True)) + lam_init)
    slope = slope_ref[h]
    seq = k_ref.shape[0]
    kpos = pos_ref[...] * slope
    ts = tq // n_sub
    for t in range(n_sub):
        rows = slice(t * ts, (t + 1) * ts)
        qrow = (q0 + t * ts + lax.broadcasted_iota(jnp.int32, (ts, LANES), 0)).astype(F32) * slope
        qpos = jnp.concatenate([qrow] * (chunk // LANES), axis=1)
        q1, q2 = q_ref[rows, :DIFF_HD], q_ref[rows, DIFF_HD:]
        t1, t2, mt1, mt2 = [], [], None, None
        for c in range(0, seq, chunk):
            b = jnp.abs(kpos[:, c:c + chunk] - qpos)
            a1 = _qk(q1, k_ref[c:c + chunk, :DIFF_HD]) - b
            a2 = _qk(q2, k_ref[c:c + chunk, DIFF_HD:]) - b
            t1.append(a1)
            t2.append(a2)
            f1, f2 = _lane_fold(a1, jnp.maximum), _lane_fold(a2, jnp.maximum)
            mt1 = f1 if mt1 is None else jnp.maximum(mt1, f1)
            mt2 = f2 if mt2 is None else jnp.maximum(mt2, f2)
        m1 = jnp.max(mt1, axis=-1, keepdims=True)
        m2 = jnp.max(mt2, axis=-1, keepdims=True)
        lt1 = jnp.zeros((ts, LANES), F32)
        lt2 = jnp.zeros((ts, LANES), F32)
        o1 = jnp.zeros((ts, 2 * DIFF_HD), F32)
        o2 = jnp.zeros((ts, 2 * DIFF_HD), F32)
        for i, c in enumerate(range(0, seq, chunk)):
            p1, p2 = jnp.exp2(t1[i] - m1), jnp.exp2(t2[i] - m2)
            lt1 = lt1 + _lane_fold(p1, jnp.add)
            lt2 = lt2 + _lane_fold(p2, jnp.add)
            o1 = o1 + _dot(p1.astype(BF16), v_ref[c:c + chunk, :])
            o2 = o2 + _dot(p2.astype(BF16), v_ref[c:c + chunk, :])
        l1 = jnp.sum(lt1, axis=-1, keepdims=True)
        l2 = jnp.sum(lt2, axis=-1, keepdims=True)
        o = o1 * (1.0 / l1) - o2 * (lam / l2)
        r = lax.rsqrt(jnp.mean(o * o, axis=-1, keepdims=True) + EPS)
        o_ref[rows, :] = (o * r * g_ref[...] * (1.0 - lam_init)).astype(o_ref.dtype)


def diff_attention(mix, cols, slopes, lam_p, subln_g, B, S, layer):
    tq = _pick(S, (512, 256, 128))
    lam_init = 0.8 - 0.6 * math.exp(-0.3 * layer)
    W = 2 * DIFF_HD
    qb, kb, vb = (cols[n] // W for n in ("fq", "fk", "fv"))
    m3 = mix.reshape(B, S, mix.shape[1])
    pos = jnp.arange(S, dtype=F32).reshape(1, S)
    kern = functools.partial(_diff_kernel, tq=tq, lam_init=lam_init, chunk=_pick(S, (512, 256)),
                             n_sub=4 if S > 2048 else 2)
    out = pl.pallas_call(
        kern,
        grid=(B, DIFF_HEADS, S // tq),
        in_specs=[pl.BlockSpec(memory_space=pltpu.SMEM),
                  pl.BlockSpec((4, DIFF_HD), lambda b, h, i: (0, 0)),
                  pl.BlockSpec((None, tq, W), lambda b, h, i: (b, i, qb + h)),
                  pl.BlockSpec((None, S, W), lambda b, h, i: (b, 0, kb + h)),
                  pl.BlockSpec((None, S, W), lambda b, h, i: (b, 0, vb + h)),
                  pl.BlockSpec((1, S), lambda b, h, i: (0, 0)),
                  pl.BlockSpec((1, W), lambda b, h, i: (0, 0))],
        out_specs=pl.BlockSpec((None, tq, W), lambda b, h, i: (b, i, h)),
        out_shape=jax.ShapeDtypeStruct((B, S, DIFF_HEADS * W), BF16),
        compiler_params=_params("parallel", "parallel", "arbitrary"),
        name="diff_attention",
    )((slopes * LOG2E).astype(F32), lam_p.astype(F32), m3, m3, m3, pos, subln_g.reshape(1, W).astype(F32))
    return out.reshape(B * S, DIFF_HEADS * W)


def _mem_attn_kernel(x_ref, gm_ref, wq_ref, gq_ref, kv_ref, wo_ref, gf_ref, o_ref, hf_ref):
    def rms(t):
        return lax.rsqrt(jnp.mean(t * t, axis=-1, keepdims=True) + EPS)

    x = x_ref[...]
    hm = (x * rms(x) * gm_ref[...]).astype(BF16)
    qacc = _dot(hm, wq_ref[...])
    heads = []
    for h in range(MEM_HEADS):
        seg = slice(h * HEAD_DIM, (h + 1) * HEAD_DIM)
        y = qacc[:, seg]
        q = (y * rms(y) * gq_ref[:, seg]).astype(BF16)
        s = _qk(q, kv_ref[:, seg])
        p = jnp.exp2(s - jnp.max(s, axis=-1, keepdims=True))
        l = jnp.sum(p, axis=-1, keepdims=True)
        vseg = slice(MEM_W + h * HEAD_DIM, MEM_W + (h + 1) * HEAD_DIM)
        heads.append((_dot(p.astype(BF16), kv_ref[:, vseg]) / l).astype(BF16))
    x2 = x + _dot(jnp.concatenate(heads, axis=-1), wo_ref[...])
    o_ref[...] = x2
    hf_ref[...] = (x2 * rms(x2) * gf_ref[...]).astype(hf_ref.dtype)


def mem_attention(x, kv, lw, B, S):
    D = x.shape[1]
    Mt = kv.shape[0] // B
    tq = _pick(S, (256, 128))
    kv3 = kv.reshape(B, Mt, 2 * MEM_W)
    x3 = x.reshape(B, S, D)

    def row(n):
        return pl.BlockSpec((1, n), lambda b, i: (0, 0))

    out, hf = pl.pallas_call(
        _mem_attn_kernel,
        grid=(B, S // tq),
        in_specs=[pl.BlockSpec((None, tq, D), lambda b, i: (b, i, 0)),
                  row(D),
                  pl.BlockSpec((D, MEM_W), lambda b, i: (0, 0)),
                  row(MEM_W),
                  pl.BlockSpec((None, Mt, 2 * MEM_W), lambda b, i: (b, 0, 0)),
                  pl.BlockSpec((MEM_W, D), lambda b, i: (0, 0)),
                  row(D)],
        out_specs=[pl.BlockSpec((None, tq, D), lambda b, i: (b, i, 0)),
                   pl.BlockSpec((None, tq, D), lambda b, i: (b, i, 0))],
        out_shape=[jax.ShapeDtypeStruct((B, S, D), F32), jax.ShapeDtypeStruct((B, S, D), BF16)],
        compiler_params=_params("parallel", "arbitrary"),
        name="mem_attention",
    )(x3, lw["ln_mem_g"].reshape(1, D).astype(F32), lw["mem_wq"], lw["mem_q_gain"].reshape(1, MEM_W),
      kv3, lw["mem_wo"], lw["ln_ffn_g"].reshape(1, D).astype(F32))
    return out.reshape(B * S, D), hf.reshape(B * S, D)


def alibi_slopes(n):
    return 2.0 ** (-8.0 * jnp.arange(1, n + 1, dtype=F32) / n)


MIX_NAMES = ("qa", "kva", "kpe", "dq", "dk", "dv", "wq", "wk", "wv", "fq", "fk", "fv")


def _mix_layout():
    src = {n: (MIX_OFFSETS[i], MIX_OFFSETS[i + 1]) for i, n in enumerate(MIX_NAMES)}
    head = -(-(src["kpe"][0] + LANES) // MIX_TILE) * MIX_TILE
    shift = head - src["dq"][0]
    cols = {n: src[n][0] + (0 if n in ("qa", "kva", "kpe") else shift) for n in MIX_NAMES}
    total = -(-(MIX_COLS + shift) // MIX_TILE) * MIX_TILE
    return src, cols, head, shift, total


def _pack_kernel(src3_ref, o_ref, *, n_head, kpe_off, tn):
    src_ref = src3_ref.at[0]
    j = pl.program_id(0)
    if not n_head:
        o_ref[...] = src_ref[...].astype(o_ref.dtype)
        return

    @pl.when(j != n_head - 1)
    def _():
        o_ref[...] = src_ref[...].astype(o_ref.dtype)

    @pl.when(j == n_head - 1)
    def _():
        quarter = MLA_ROPE // 2
        x1 = src_ref[kpe_off:kpe_off + quarter, :].astype(o_ref.dtype)
        x2 = src_ref[kpe_off + quarter:kpe_off + 2 * quarter, :].astype(o_ref.dtype)
        if kpe_off:
            o_ref[:kpe_off, :] = src_ref[:kpe_off, :].astype(o_ref.dtype)
        for t, piece in enumerate((x1, x2, x2, x1)):
            o_ref[kpe_off + t * quarter:kpe_off + (t + 1) * quarter, :] = piece
        if kpe_off + LANES < tn:
            o_ref[kpe_off + LANES:, :] = jnp.zeros((tn - kpe_off - LANES, o_ref.shape[1]), o_ref.dtype)


def _cast_kernel(x_ref, o_ref):
    o_ref[...] = x_ref[...].astype(o_ref.dtype)


def cast_layer(w, l):
    _, R, C = w.shape
    tc = _pick(C, (2048, 1024, 512, 256, 128))
    tr = _pick(R, tuple(t for t in (2048, 1024, 512, 256, 128, 64, 8) if t * tc <= 2 ** 21))
    return pl.pallas_call(
        _cast_kernel,
        grid=(R // tr, C // tc),
        in_specs=[pl.BlockSpec((None, tr, tc), lambda i, j: (l, i, j))],
        out_specs=pl.BlockSpec((tr, tc), lambda i, j: (i, j)),
        out_shape=jax.ShapeDtypeStruct((R, C), BF16),
        compiler_params=_params("parallel", "arbitrary"),
        name="cast_layer",
    )(w)


def pack_shifted(src_t, l, n_out, shift, head_rows=0, kpe_row=0):
    _, C, K = src_t.shape
    tn = MIX_TILE
    n_head = head_rows // tn
    assert shift % 8 == 0 and head_rows % tn == 0 and (n_head == 0 or kpe_row // tn == n_head - 1)
    kern = functools.partial(_pack_kernel, n_head=n_head, kpe_off=kpe_row % tn, tn=tn)
    return pl.pallas_call(
        kern,
        grid=(n_out // tn,),
        in_specs=[pl.BlockSpec((pl.Element(1), pl.Element(tn), pl.Element(K)),
                               lambda j: (l, pl.multiple_of(jnp.where(j < n_head, j * tn, j * tn - shift), 8), 0))],
        out_specs=pl.BlockSpec((tn, K), lambda j: (j, 0)),
        out_shape=jax.ShapeDtypeStruct((n_out, K), BF16),
        compiler_params=_params("arbitrary"),
        name="pack_shifted",
    )(src_t)


def pack_layer(p, l):
    src, cols, head_w, shift, total = _mix_layout()
    half = MLA_ROPE // 2
    qscale = HEAD_DIM ** -0.5 * LOG2E

    def tile(g, n):
        return jnp.tile(g.astype(F32), n)

    norm_gain = {
        "dq": tile(p["dil_qk_g"][l, 0], DIL_HEADS) * qscale, "dk": tile(p["dil_qk_g"][l, 1], DIL_HEADS),
        "fq": tile(p["diff_qk_g"][l, 0], 2 * DIFF_HEADS) * (DIFF_HD ** -0.5 * LOG2E),
        "fk": tile(p["diff_qk_g"][l, 1], 2 * DIFF_HEADS),
        "wq": tile(p["win_qk_g"][l, 0], WIN_Q_HEADS) * qscale, "wk": tile(p["win_qk_g"][l, 1], WIN_KV_HEADS),
    }
    gain = jnp.ones((total,), F32)
    flag = np.zeros((total,), np.float32)
    for n, g in norm_gain.items():
        gain = gain.at[cols[n]:cols[n] + g.shape[0]].set(g)
        flag[cols[n]:cols[n] + g.shape[0]] = 1.0

    w_in_t = jnp.swapaxes(p["w_in"], 1, 2)
    w_mix = pack_shifted(w_in_t, l, total, shift, head_rows=head_w, kpe_row=src["kpe"][0])
    w_gate = pack_shifted(w_in_t, l, N_BRANCH * D_MODEL, -MIX_COLS)

    wq3 = p["mla_wq_up"][l].reshape(MLA_Q_LORA, MLA_HEADS, MLA_QK)
    nope, x1, x2 = wq3[:, :, :MLA_NOPE], wq3[:, :, MLA_NOPE:MLA_NOPE + half], wq3[:, :, MLA_NOPE + half:]
    z = jnp.zeros((MLA_Q_LORA, MLA_HEADS, LANES - MLA_ROPE), F32)
    wq_full = jnp.concatenate([nope, x1, x2, z], axis=-1).reshape(MLA_Q_LORA, MLA_HEADS * MLA_QK_PAD)
    wq_sw = jnp.concatenate([x2, x1, z], axis=-1).reshape(MLA_Q_LORA, MLA_HEADS * LANES)
    wkv3 = p["mla_wkv_up"][l].reshape(MLA_KV_LORA, MLA_HEADS, MLA_NOPE + MLA_V)
    zg = jnp.zeros((LANES - MLA_ROPE,), F32)
    qk_g = p["mla_qk_g"][l].astype(F32)

    return {
        "ln_mix_g": p["ln_mix_g"][l],
        "w_mix": w_mix, "w_gate": w_gate, "mix_gain": gain, "mix_flag": flag,
        "mla_qa_g": p["mla_qa_g"][l].reshape(1, -1).astype(F32),
        "mla_kva_g": p["mla_kva_g"][l].reshape(1, -1).astype(F32),
        "mla_wq": wq_full.astype(BF16), "mla_wq_sw": wq_sw.astype(BF16),
        "mla_wk": wkv3[:, :, :MLA_NOPE].reshape(MLA_KV_LORA, -1).astype(BF16),
        "mla_wv": wkv3[:, :, MLA_NOPE:].reshape(MLA_KV_LORA, -1).astype(BF16),
        "mla_gq": (jnp.concatenate([qk_g[0], zg]) * (MLA_QK ** -0.5 * LOG2E)).reshape(1, -1),
        "mla_gk": jnp.concatenate([qk_g[1], zg]).reshape(1, -1),
        "win_sink": p["win_sink"][l], "diff_lambda": p["diff_lambda"][l], "diff_subln_g": p["diff_subln_g"][l],
        "w_branch": cast_layer(p["w_branch"].reshape(DEPTH, N_BRANCH * BRANCH_W, D_MODEL), l).reshape(
            N_BRANCH, BRANCH_W, D_MODEL),
        "w_out": cast_layer(p["w_out"], l),
        "ln_mem_g": p["ln_mem_g"][l], "mem_ln_g": p["mem_ln_g"][l],
        "mem_wq": cast_layer(p["mem_wq"], l), "mem_wkv": cast_layer(p["mem_wkv"], l),
        "mem_q_gain": tile(p["mem_qk_g"][l, 0], MEM_HEADS) * (HEAD_DIM ** -0.5 * LOG2E),
        "mem_kv_gain": jnp.concatenate([tile(p["mem_qk_g"][l, 1], MEM_HEADS), jnp.ones((MEM_W,), F32)]),
        "mem_kv_flag": np.concatenate([np.ones((MEM_W,), np.float32), np.zeros((MEM_W,), np.float32)]),
        "mem_wo": cast_layer(p["mem_wo"], l),
        "ln_ffn_g": p["ln_ffn_g"][l],
        "ffn_w_in": cast_layer(p["ffn_w_in"], l), "ffn_w_out": cast_layer(p["ffn_w_out"], l),
    }


def rotary_tables(S):
    half = MLA_ROPE // 2
    inv_freq = ROPE_THETA ** (-jnp.arange(half, dtype=F32) / half)
    ang = jnp.arange(S, dtype=F32)[:, None] * inv_freq[None, :]
    cos, sin = jnp.cos(ang), jnp.sin(ang)
    z = jnp.zeros((S, LANES - MLA_ROPE), F32)
    return jnp.concatenate([cos, cos, z], axis=1), jnp.concatenate([-sin, sin, z], axis=1)


def _trunk(x, mem, layers, tables):
    B, S, D = x.shape
    M = B * S
    _, cols, _, _, _ = _mix_layout()
    ct, st = rotary_tables(S)
    xf = x.reshape(M, D)
    memf = mem.reshape(-1, D)
    dil_tq = _pick(S, (256,))
    for l, lw in enumerate(layers):
        h = rmsnorm(xf, lw["ln_mix_g"])
        mix = mm_segnorm(h, lw["w_mix"], lw["mix_gain"], lw["mix_flag"], w_output_major=True,
                         tm_prefs=(2048, 1024, 512, 256), tn_prefs=(MIX_TILE,))
        gates = mm_sigmoid(h, lw["w_gate"])

        q, k, v = mla_prep(mix, cols, S, ct, st, lw)
        o_mla = dense_attention(q, k, v, B, S, MLA_HEADS, MLA_QK_PAD, MLA_V)
        o_dil = band_attention(mix, cols["dq"], mix, cols["dk"], mix, cols["dv"], tables["dil_bias"],
                               jnp.full((DIL_HEADS,), NEG_INF, F32), B, S, DIL_HEADS, 1, DIL_HEADS // 2,
                               dil_tq, dil_tq, DIL_RADIUS)
        o_win = band_attention(mix, cols["wq"], mix, cols["wk"], mix, cols["wv"], tables["win_bias"],
                               lw["win_sink"], B, S, WIN_Q_HEADS, WIN_Q_HEADS // WIN_KV_HEADS, WIN_Q_HEADS,
                               2 * WIN_RADIUS, WIN_RADIUS, WIN_RADIUS)
        o_diff = diff_attention(mix, cols, tables["slopes_diff"], lw["diff_lambda"], lw["diff_subln_g"], B, S, l)

        merged = branch_merge((o_mla, o_dil, o_win, o_diff), lw["w_branch"], gates)
        xf = mm_residual(merged, lw["w_out"], xf, tn_prefs=(1024, 512, 256, 128))

        kvm = mm_segnorm(rmsnorm(memf, lw["mem_ln_g"]), lw["mem_wkv"], lw["mem_kv_gain"], lw["mem_kv_flag"])
        xf, hf = mem_attention(xf, kvm, lw, B, S)

        hid = mm_swiglu(hf, lw["ffn_w_in"], FFN_HIDDEN)
        xf = mm_residual(hid, lw["ffn_w_out"], xf, tm_prefs=(512, 256, 128), tn_prefs=(512, 256, 128))
    return xf.reshape(B, S, D)


def kernel(x_prompt, x_sample, mem_prompt, mem_sample, ln_mix_g, w_in, mla_qa_g, mla_kva_g, mla_wq_up,
           mla_wkv_up, mla_qk_g, dil_qk_g, win_qk_g, win_sink, diff_qk_g, diff_lambda, diff_subln_g,
           w_branch, w_out, ln_mem_g, mem_ln_g, mem_wq, mem_wkv, mem_qk_g, mem_wo, ln_ffn_g, ffn_w_in,
           ffn_w_out):
    p = dict(ln_mix_g=ln_mix_g, w_in=w_in, mla_qa_g=mla_qa_g, mla_kva_g=mla_kva_g, mla_wq_up=mla_wq_up,
             mla_wkv_up=mla_wkv_up, mla_qk_g=mla_qk_g, dil_qk_g=dil_qk_g, win_qk_g=win_qk_g,
             win_sink=win_sink, diff_qk_g=diff_qk_g, diff_lambda=diff_lambda, diff_subln_g=diff_subln_g,
             w_branch=w_branch, w_out=w_out, ln_mem_g=ln_mem_g, mem_ln_g=mem_ln_g, mem_wq=mem_wq,
             mem_wkv=mem_wkv, mem_qk_g=mem_qk_g, mem_wo=mem_wo, ln_ffn_g=ln_ffn_g, ffn_w_in=ffn_w_in,
             ffn_w_out=ffn_w_out)
    layers = [pack_layer(p, l) for l in range(DEPTH)]
    tables = {
        "dil_bias": band_bias(alibi_slopes(DIL_HEADS), dil_multiplicity, 256, 256, DIL_RADIUS),
        "win_bias": band_bias(alibi_slopes(WIN_Q_HEADS), win_multiplicity, 2 * WIN_RADIUS, WIN_RADIUS,
                              WIN_RADIUS),
        "slopes_diff": alibi_slopes(DIFF_HEADS),
    }
    y_prompt = _trunk(x_prompt, mem_prompt, layers, tables)
    y_sample = _trunk(x_sample, mem_sample, layers, tables)
    return (y_prompt, y_sample)
```

```python
import functools
import math

import jax
import jax.numpy as jnp
import numpy as np
from jax import lax
from jax.experimental import pallas as pl
from jax.experimental.pallas import tpu as pltpu

F32 = jnp.float32
BF16 = jnp.bfloat16

D_MODEL = 4096
DEPTH = 2
EPS = 1e-6
NEG_INF = -1e30
N_BRANCH = 4
BRANCH_W = D_MODEL // N_BRANCH
HEAD_DIM = 128

MLA_NOPE = 128
MLA_ROPE = 64
MLA_V = 128
MLA_HEADS = BRANCH_W // MLA_V
MLA_Q_LORA = D_MODEL // 4
MLA_KV_LORA = D_MODEL // 8
MLA_QK = MLA_NOPE + MLA_ROPE
MLA_QK_PAD = 256
ROPE_THETA = 10000.0

DIL_HEADS = BRANCH_W // HEAD_DIM
DIL_PATTERNS = ((128, 1), (512, 4), (2048, 16))
DIL_RADIUS = max(w // 2 for w, _ in DIL_PATTERNS)

WIN_Q_HEADS = BRANCH_W // HEAD_DIM
WIN_KV_HEADS = WIN_Q_HEADS // 4
WIN_RADIUS = 128

DIFF_HD = 128
DIFF_HEADS = BRANCH_W // (2 * DIFF_HD)

MEM_HEADS = 4
MEM_W = MEM_HEADS * HEAD_DIM

FFN_HIDDEN = -(-8 * D_MODEL // (3 * 256)) * 256

MIX_SPLITS = (MLA_Q_LORA, MLA_KV_LORA, MLA_ROPE,
              BRANCH_W, BRANCH_W, BRANCH_W,
              BRANCH_W, WIN_KV_HEADS * HEAD_DIM, WIN_KV_HEADS * HEAD_DIM,
              BRANCH_W, BRANCH_W, BRANCH_W)
MIX_COLS = sum(MIX_SPLITS)
MIX_OFFSETS = tuple(int(o) for o in np.cumsum((0,) + MIX_SPLITS))

LOG2E = math.log2(math.e)
LANES = 128
MIX_TILE = 512
VMEM_LIMIT_BYTES = 56 * 1024 * 1024


def _pick(n, prefs):
    for p in prefs:
        if n % p == 0:
            return p
    raise ValueError(f"no tile in {prefs} divides {n}")


def _params(*sem):
    return pltpu.CompilerParams(dimension_semantics=sem, vmem_limit_bytes=VMEM_LIMIT_BYTES)


def _rmsnorm_kernel(x_ref, g_ref, o_ref):
    x = x_ref[...].astype(F32)
    ms = jnp.mean(x * x, axis=-1, keepdims=True)
    o_ref[...] = (x * lax.rsqrt(ms + EPS) * g_ref[...]).astype(o_ref.dtype)


def rmsnorm(x, g):
    M, D = x.shape
    tm = _pick(M, (512, 256, 128, 64, 8))
    return pl.pallas_call(
        _rmsnorm_kernel,
        grid=(M // tm,),
        in_specs=[pl.BlockSpec((tm, D), lambda i: (i, 0)),
                  pl.BlockSpec((1, D), lambda i: (0, 0))],
        out_specs=pl.BlockSpec((tm, D), lambda i: (i, 0)),
        out_shape=jax.ShapeDtypeStruct((M, D), BF16),
        compiler_params=_params("parallel"),
        name="rmsnorm",
    )(x, g.reshape(1, D).astype(F32))


def _dot(a, b):
    return jnp.dot(a, b, preferred_element_type=F32)


MXU_COLS = 256


def _sigmoid(x):
    return 0.5 * jnp.tanh(0.5 * x) + 0.5


def _col_slabs(n):
    w = MXU_COLS if n % MXU_COLS == 0 else n
    return [slice(c, c + w) for c in range(0, n, w)]


def _dot_nt(a, bt):
    return lax.dot_general(a, bt, (((1,), (1,)), ((), ())), preferred_element_type=F32)


def _mm_sigmoid_kernel(a_ref, bt_ref, o_ref):
    for cs in _col_slabs(o_ref.shape[1]):
        o_ref[:, cs] = _sigmoid(_dot_nt(a_ref[...], bt_ref[cs, :])).astype(o_ref.dtype)


def _mm_segnorm_kernel(tile_norm_ref, a_ref, b_ref, g_ref, f_ref, o_ref, *, w_output_major):
    def slab(cs):
        return _dot_nt(a_ref[...], b_ref[cs, :]) if w_output_major else _dot(a_ref[...], b_ref[:, cs])

    has_norm = tile_norm_ref[pl.program_id(1)] > 0

    @pl.when(has_norm)
    def _():
        for cs in _col_slabs(o_ref.shape[1]):
            acc = slab(cs)
            for c in range(0, acc.shape[1], LANES):
                seg = slice(cs.start + c, cs.start + c + LANES)
                y = acc[:, c:c + LANES]
                r = lax.rsqrt(jnp.mean(y * y, axis=-1, keepdims=True) + EPS)
                mult = jnp.where(f_ref[:, seg] > 0.0, r, 1.0) * g_ref[:, seg]
                o_ref[:, seg] = (y * mult).astype(o_ref.dtype)

    @pl.when(jnp.logical_not(has_norm))
    def _():
        for cs in _col_slabs(o_ref.shape[1]):
            o_ref[:, cs] = (slab(cs) * g_ref[:, cs]).astype(o_ref.dtype)


def _mm_residual_kernel(a_ref, b_ref, r_ref, o_ref):
    for cs in _col_slabs(o_ref.shape[1]):
        o_ref[:, cs] = r_ref[:, cs] + _dot(a_ref[...], b_ref[:, cs])


def _mm_swiglu_kernel(a_ref, bg_ref, bu_ref, o_ref):
    half = a_ref.shape[0] // 2
    for rows in (slice(0, half), slice(half, 2 * half)):
        a = a_ref[rows, :]
        g = _dot(a, bg_ref[...])
        u = _dot(a, bu_ref[...])
        o_ref[rows, :] = (g * _sigmoid(g) * u).astype(o_ref.dtype)


def mm_sigmoid(a, wt, tm_prefs=(2048, 1024, 512, 256, 128), tn_prefs=(512, 256, 128)):
    M, K = a.shape
    N = wt.shape[0]
    tm, tn = _pick(M, tm_prefs), _pick(N, tn_prefs)
    return pl.pallas_call(
        _mm_sigmoid_kernel,
        grid=(M // tm, N // tn),
        in_specs=[pl.BlockSpec((tm, K), lambda i, j: (i, 0)),
                  pl.BlockSpec((tn, K), lambda i, j: (j, 0))],
        out_specs=pl.BlockSpec((tm, tn), lambda i, j: (i, j)),
        out_shape=jax.ShapeDtypeStruct((M, N), BF16),
        compiler_params=_params("parallel", "arbitrary"),
        name="mm_sigmoid",
    )(a, wt)


def mm_segnorm(a, w, gain, flag, w_output_major=False, tm_prefs=(1024, 512, 256, 128), tn_prefs=(512, 256, 128)):
    M, K = a.shape
    N = gain.shape[0]
    tm, tn = _pick(M, tm_prefs), _pick(N, tn_prefs)
    flag = np.asarray(flag, np.float32)
    tile_norm = jnp.asarray(flag.reshape(N // tn, tn).max(axis=1) > 0, jnp.int32)
    w_spec = (pl.BlockSpec((tn, K), lambda i, j: (j, 0)) if w_output_major
              else pl.BlockSpec((K, tn), lambda i, j: (0, j)))
    return pl.pallas_call(
        functools.partial(_mm_segnorm_kernel, w_output_major=w_output_major),
        grid=(M // tm, N // tn),
        in_specs=[pl.BlockSpec(memory_space=pltpu.SMEM),
                  pl.BlockSpec((tm, K), lambda i, j: (i, 0)),
                  w_spec,
                  pl.BlockSpec((1, tn), lambda i, j: (0, j)),
                  pl.BlockSpec((1, tn), lambda i, j: (0, j))],
        out_specs=pl.BlockSpec((tm, tn), lambda i, j: (i, j)),
        out_shape=jax.ShapeDtypeStruct((M, N), BF16),
        compiler_params=_params("parallel", "arbitrary"),
        name="mm_segnorm",
    )(tile_norm, a, w, gain.reshape(1, N).astype(F32), jnp.asarray(flag).reshape(1, N))


def mm_residual(a, w, res, tm_prefs=(1024, 512, 256, 128), tn_prefs=(512, 256, 128)):
    M, K = a.shape
    N = w.shape[1]
    tm, tn = _pick(M, tm_prefs), _pick(N, tn_prefs)
    return pl.pallas_call(
        _mm_residual_kernel,
        grid=(M // tm, N // tn),
        in_specs=[pl.BlockSpec((tm, K), lambda i, j: (i, 0)),
                  pl.BlockSpec((K, tn), lambda i, j: (0, j)),
                  pl.BlockSpec((tm, tn), lambda i, j: (i, j))],
        out_specs=pl.BlockSpec((tm, tn), lambda i, j: (i, j)),
        out_shape=jax.ShapeDtypeStruct((M, N), F32),
        compiler_params=_params("parallel", "arbitrary"),
        name="mm_residual",
    )(a, w, res)


def mm_swiglu(a, w, hidden):
    M, K = a.shape
    tm = _pick(M, (2048, 1024, 512, 256, 128))
    tn = _pick(hidden, (256, 128))
    nj = hidden // tn
    return pl.pallas_call(
        _mm_swiglu_kernel,
        grid=(M // tm, nj),
        in_specs=[pl.BlockSpec((tm, K), lambda i, j: (i, 0)),
                  pl.BlockSpec((K, tn), lambda i, j: (0, j)),
                  pl.BlockSpec((K, tn), lambda i, j: (0, j + nj))],
        out_specs=pl.BlockSpec((tm, tn), lambda i, j: (i, j)),
        out_shape=jax.ShapeDtypeStruct((M, hidden), BF16),
        compiler_params=_params("parallel", "arbitrary"),
        name="mm_swiglu",
    )(a, w, w)


def _branch_kernel(oa_ref, ob_ref, oc_ref, od_ref, w_ref, ga_ref, gb_ref, gc_ref, gd_ref, o_ref):
    for cs in _col_slabs(o_ref.shape[1]):
        acc = ga_ref[:, cs].astype(F32) * _dot(oa_ref[...], w_ref[0, :, cs])
        acc += gb_ref[:, cs].astype(F32) * _dot(ob_ref[...], w_ref[1, :, cs])
        acc += gc_ref[:, cs].astype(F32) * _dot(oc_ref[...], w_ref[2, :, cs])
        acc += gd_ref[:, cs].astype(F32) * _dot(od_ref[...], w_ref[3, :, cs])
        o_ref[:, cs] = acc.astype(o_ref.dtype)


def branch_merge(outs, w_branch, gates):
    M, W = outs[0].shape
    D = w_branch.shape[2]
    tm = _pick(M, (1024, 512, 256, 128))
    tn = _pick(D, (1024, 512, 256, 128))
    nj = D // tn
    o_spec = pl.BlockSpec((tm, W), lambda i, j: (i, 0))
    g_specs = [pl.BlockSpec((tm, tn), functools.partial(lambda i, j, b: (i, b * nj + j), b=b))
               for b in range(N_BRANCH)]
    return pl.pallas_call(
        _branch_kernel,
        grid=(M // tm, nj),
        in_specs=[o_spec] * N_BRANCH + [pl.BlockSpec((N_BRANCH, W, tn), lambda i, j: (0, 0, j))] + g_specs,
        out_specs=pl.BlockSpec((tm, tn), lambda i, j: (i, j)),
        out_shape=jax.ShapeDtypeStruct((M, D), BF16),
        compiler_params=_params("parallel", "arbitrary"),
        name="branch_merge",
    )(*outs, w_branch, gates, gates, gates, gates)


def _mla_prep_kernel(qa_ref, kva_ref, kpe_ref, ct_ref, st_ref, qag_ref, kvag_ref,
                     wq_ref, wqs_ref, wk_ref, wv_ref, gq_ref, gk_ref,
                     q_ref, k_ref, v_ref):
    def norm(x, g):
        return (x * lax.rsqrt(jnp.mean(x * x, axis=-1, keepdims=True) + EPS) * g).astype(BF16)

    ct = ct_ref[...]
    st = st_ref[...]
    qn = norm(qa_ref[...].astype(F32), qag_ref[...])
    kvn = norm(kva_ref[...].astype(F32), kvag_ref[...])
    qfull = _dot(qn, wq_ref[...])
    qsw = _dot(qn, wqs_ref[...])
    knope = _dot(kvn, wk_ref[...])
    vals = _dot(kvn, wv_ref[...])
    ones = jnp.ones((vals.shape[0], MLA_V), v_ref.dtype)
    for h in range(MLA_HEADS):
        v_ref[:, 2 * h * MLA_V:(2 * h + 1) * MLA_V] = vals[:, h * MLA_V:(h + 1) * MLA_V].astype(v_ref.dtype)
        v_ref[:, (2 * h + 1) * MLA_V:(2 * h + 2) * MLA_V] = ones

    kpe = kpe_ref[...].astype(F32)
    kpe_rot = kpe * ct + pltpu.roll(kpe, 2 * (MLA_ROPE // 2), 1) * st
    kpe_ss = jnp.sum(kpe_rot * kpe_rot, axis=-1, keepdims=True)
    gq_n, gq_r = gq_ref[:, :LANES], gq_ref[:, LANES:]
    gk_n, gk_r = gk_ref[:, :LANES], gk_ref[:, LANES:]
    for h in range(MLA_HEADS):
        lo = h * MLA_QK_PAD
        q_n = qfull[:, lo:lo + LANES]
        q_r = qfull[:, lo + LANES:lo + 2 * LANES] * ct + qsw[:, h * LANES:(h + 1) * LANES] * st
        ss = jnp.sum(q_n * q_n, axis=-1, keepdims=True) + jnp.sum(q_r * q_r, axis=-1, keepdims=True)
        r = lax.rsqrt(ss * (1.0 / MLA_QK) + EPS)
        q_ref[:, lo:lo + LANES] = (q_n * r * gq_n).astype(q_ref.dtype)
        q_ref[:, lo + LANES:lo + 2 * LANES] = (q_r * r * gq_r).astype(q_ref.dtype)
        k_n = knope[:, h * LANES:(h + 1) * LANES]
        ss = jnp.sum(k_n * k_n, axis=-1, keepdims=True) + kpe_ss
        r = lax.rsqrt(ss * (1.0 / MLA_QK) + EPS)
        k_ref[:, lo:lo + LANES] = (k_n * r * gk_n).astype(k_ref.dtype)
        k_ref[:, lo + LANES:lo + 2 * LANES] = (kpe_rot * r * gk_r).astype(k_ref.dtype)


def mla_prep(mix, cols, S, ct, st, lw):
    M = mix.shape[0]
    tm = _pick(S, (256, 128))
    ns = S // tm
    HQ = MLA_HEADS * MLA_QK_PAD
    HV = MLA_HEADS * 2 * MLA_V

    def col(width, off):
        assert off % width == 0
        return pl.BlockSpec((tm, width), lambda i: (i, off // width))

    def whole(a):
        return pl.BlockSpec(a.shape, lambda i: (0,) * a.ndim)

    consts = (lw["mla_qa_g"], lw["mla_kva_g"], lw["mla_wq"], lw["mla_wq_sw"], lw["mla_wk"], lw["mla_wv"],
              lw["mla_gq"], lw["mla_gk"])
    return pl.pallas_call(
        _mla_prep_kernel,
        grid=(M // tm,),
        in_specs=[col(MLA_Q_LORA, cols["qa"]), col(MLA_KV_LORA, cols["kva"]), col(LANES, cols["kpe"]),
                  pl.BlockSpec((tm, LANES), lambda i: (i % ns, 0)),
                  pl.BlockSpec((tm, LANES), lambda i: (i % ns, 0))] + [whole(c) for c in consts],
        out_specs=[pl.BlockSpec((tm, HQ), lambda i: (i, 0)),
                   pl.BlockSpec((tm, HQ), lambda i: (i, 0)),
                   pl.BlockSpec((tm, HV), lambda i: (i, 0))],
        out_shape=[jax.ShapeDtypeStruct((M, HQ), BF16),
                   jax.ShapeDtypeStruct((M, HQ), BF16),
                   jax.ShapeDtypeStruct((M, HV), BF16)],
        compiler_params=_params("parallel"),
        name="mla_prep",
    )(mix, mix, mix, ct, st, *consts)


def _qk(q, k):
    return lax.dot_general(q, k, (((1,), (1,)), ((), ())), preferred_element_type=F32)


def _dense_attn_kernel(q_ref, k_ref, v_ref, o_ref, *, n_sub, chunk):
    dv = o_ref.shape[1]
    ts = q_ref.shape[0] // n_sub
    seq = k_ref.shape[0]
    for t in range(n_sub):
        rows = slice(t * ts, (t + 1) * ts)
        q = q_ref[rows, :]
        pieces = [_qk(q, k_ref[c:c + chunk, :]) for c in range(0, seq, chunk)]
        mt = _lane_fold(pieces[0], jnp.maximum)
        for piece in pieces[1:]:
            mt = jnp.maximum(mt, _lane_fold(piece, jnp.maximum))
        m = jnp.max(mt, axis=-1, keepdims=True)
        o = jnp.zeros((ts, 2 * dv), F32)
        for c, piece in zip(range(0, seq, chunk), pieces):
            o = o + _dot(jnp.exp2(piece - m).astype(BF16), v_ref[c:c + chunk, :])
        o_ref[rows, :] = (o[:, :dv] / o[:, dv:]).astype(o_ref.dtype)


def dense_attention(q, k, v, B, S, H, dk, dv, n_sub=4):
    tq = _pick(S, (1024, 512, 256))
    chunk = _pick(S, (512, 256))
    q3, k3, v3 = (t.reshape(B, S, t.shape[1]) for t in (q, k, v))
    out = pl.pallas_call(
        functools.partial(_dense_attn_kernel, n_sub=n_sub, chunk=chunk),
        grid=(B, H, S // tq),
        in_specs=[pl.BlockSpec((None, tq, dk), lambda b, h, i: (b, i, h)),
                  pl.BlockSpec((None, S, dk), lambda b, h, i: (b, 0, h)),
                  pl.BlockSpec((None, S, 2 * dv), lambda b, h, i: (b, 0, h))],
        out_specs=pl.BlockSpec((None, tq, dv), lambda b, h, i: (b, i, h)),
        out_shape=jax.ShapeDtypeStruct((B, S, H * dv), BF16),
        compiler_params=_params("parallel", "parallel", "arbitrary"),
        name="dense_attention",
    )(q3, k3, v3)
    return out.reshape(B * S, H * dv)


def _lane_fold(x, op):
    out = x[:, :LANES]
    for j in range(1, x.shape[1] // LANES):
        out = op(out, x[:, j * LANES:(j + 1) * LANES])
    return out


def _band_kernel(sink_ref, q_ref, k_ref, v_ref, bias_ref, o_ref, *, tq, tk, radius, seq, n_chunks, hpb, rep,
                 mxu_rowsum):
    hb = pl.program_id(1)
    q0 = pl.program_id(2) * tq
    starts, tiles = [], []
    for c in range(n_chunks):
        start = q0 - radius + c * tk
        inside = jnp.logical_and(start >= 0, start + tk <= seq)
        starts.append(pl.multiple_of(jnp.clip(start, 0, seq - tk), tk))
        tiles.append(jnp.where(inside, c, n_chunks))
    for hh in range(hpb):
        qs = slice(hh * HEAD_DIM, (hh + 1) * HEAD_DIM)
        ks = slice((hh // rep) * HEAD_DIM, (hh // rep + 1) * HEAD_DIM)
        q = q_ref[:, qs]
        scores = [_qk(q, k_ref[pl.ds(sc, tk), ks]) + bias_ref[hh, t] for sc, t in zip(starts, tiles)]
        mt = _lane_fold(scores[0], jnp.maximum)
        for s in scores[1:]:
            mt = jnp.maximum(mt, _lane_fold(s, jnp.maximum))
        sink = sink_ref[hb * hpb + hh] * LOG2E
        m = jnp.maximum(jnp.max(mt, axis=-1, keepdims=True), sink)
        if mxu_rowsum:
            ones = jnp.ones((tk, HEAD_DIM), BF16)
            acc = jnp.zeros((tq, 2 * HEAD_DIM), F32)
            for s, sc in zip(scores, starts):
                vc = jnp.concatenate([v_ref[pl.ds(sc, tk), ks], ones], axis=1)
                acc = acc + _dot(jnp.exp2(s - m).astype(BF16), vc)
            l = acc[:, HEAD_DIM:] + jnp.exp2(sink - m)
            o_ref[:, qs] = (acc[:, :HEAD_DIM] / l).astype(o_ref.dtype)
        else:
            lt = jnp.zeros((tq, LANES), F32)
            acc = jnp.zeros((tq, HEAD_DIM), F32)
            for s, sc in zip(scores, starts):
                p = jnp.exp2(s - m)
                lt = lt + _lane_fold(p, jnp.add)
                acc = acc + _dot(p.astype(BF16), v_ref[pl.ds(sc, tk), ks])
            l = jnp.sum(lt, axis=-1, keepdims=True) + jnp.exp2(sink - m)
            o_ref[:, qs] = (acc / l).astype(o_ref.dtype)


def band_bias(slopes, mult_fn, tq, tk, radius):
    n_chunks = (tq + 2 * radius) // tk
    a = jnp.arange(tq)[None, :, None]
    c = jnp.arange(tk)[None, None, :]
    delta = (jnp.arange(n_chunks)[:, None, None] * tk - radius) + c - a
    mult = mult_fn(delta)
    dist = jnp.abs(delta).astype(F32)
    logm = jnp.log2(jnp.maximum(mult, 1).astype(F32))
    bias = logm[None] - (slopes * LOG2E)[:, None, None, None] * dist[None]
    bias = jnp.where(mult[None] > 0, bias, NEG_INF)
    dead = jnp.full((slopes.shape[0], 1, tq, tk), NEG_INF, F32)
    return jnp.concatenate([bias, dead], axis=1)


def dil_multiplicity(delta):
    m = jnp.zeros(delta.shape, jnp.int32)
    for window, dil in DIL_PATTERNS:
        m = m + ((delta % dil == 0) & (jnp.abs(delta) <= window // 2)).astype(jnp.int32)
    return m


def win_multiplicity(delta):
    return (jnp.abs(delta) <= WIN_RADIUS).astype(jnp.int32)


def band_attention(qarr, q_off, karr, k_off, varr, v_off, bias, sinks, B, S, H, rep, hpb, tq, tk, radius):
    n_chunks = bias.shape[1] - 1
    assert radius % tk == 0 and tq % tk == 0 and S % tq == 0 and S >= tk and hpb % rep == 0 and H % hpb == 0
    wq, wk = hpb * HEAD_DIM, (hpb // rep) * HEAD_DIM
    assert q_off % wq == 0 and k_off % wk == 0 and v_off % wk == 0
    qb, kb, vb = q_off // wq, k_off // wk, v_off // wk
    q3, k3, v3 = (t.reshape(B, S, t.shape[1]) for t in (qarr, karr, varr))
    kern = functools.partial(_band_kernel, tq=tq, tk=tk, radius=radius, seq=S, n_chunks=n_chunks, hpb=hpb, rep=rep,
                             mxu_rowsum=tk >= MXU_COLS)
    out = pl.pallas_call(
        kern,
        grid=(B, H // hpb, S // tq),
        in_specs=[pl.BlockSpec(memory_space=pltpu.SMEM),
                  pl.BlockSpec((None, tq, wq), lambda b, h, i: (b, i, qb + h)),
                  pl.BlockSpec((None, S, wk), lambda b, h, i: (b, 0, kb + h)),
                  pl.BlockSpec((None, S, wk), lambda b, h, i: (b, 0, vb + h)),
                  pl.BlockSpec((hpb, n_chunks + 1, tq, tk), lambda b, h, i: (h, 0, 0, 0))],
        out_specs=pl.BlockSpec((None, tq, wq), lambda b, h, i: (b, i, h)),
        out_shape=jax.ShapeDtypeStruct((B, S, H * HEAD_DIM), BF16),
        compiler_params=_params("parallel", "parallel", "arbitrary"),
        name="band_attention",
    )(sinks.astype(F32), q3, k3, v3, bias)
    return out.reshape(B * S, H * HEAD_DIM)


def _diff_kernel(slope_ref, lam_ref, q_ref, k_ref, v_ref, pos_ref, g_ref, o_ref, *, tq, lam_init, chunk, n_sub):
    h = pl.program_id(1)
    q0 = pl.program_id(2) * tq
    lp = lam_ref[...]
    lam = (jnp.exp(jnp.sum(lp[0:1] * lp[1:2], axis=-1, keepdims=True))
           - jnp.exp(jnp.sum(lp[2:3] * lp[3:4], axis=-1, keepdims=True)) + lam_init)
    slope = slope_ref[h]
    seq = k_ref.shape[0]
    kpos = pos_ref[...] * slope
    ts = tq // n_sub
    for t in range(n_sub):
        rows = slice(t * ts, (t + 1) * ts)
        qrow = (q0 + t * ts + lax.broadcasted_iota(jnp.int32, (ts, LANES), 0)).astype(F32) * slope
        qpos = jnp.concatenate([qrow] * (chunk // LANES), axis=1)
        q1, q2 = q_ref[rows, :DIFF_HD], q_ref[rows, DIFF_HD:]
        t1, t2, mt1, mt2 = [], [], None, None
        for c in range(0, seq, chunk):
            b = jnp.abs(kpos[:, c:c + chunk] - qpos)
            a1 = _qk(q1, k_ref[c:c + chunk, :DIFF_HD]) - b
            a2 = _qk(q2, k_ref[c:c + chunk, DIFF_HD:]) - b
            t1.append(a1)
            t2.append(a2)
            f1, f2 = _lane_fold(a1, jnp.maximum), _lane_fold(a2, jnp.maximum)
            mt1 = f1 if mt1 is None else jnp.maximum(mt1, f1)
            mt2 = f2 if mt2 is None else jnp.maximum(mt2, f2)
        m1 = jnp.max(mt1, axis=-1, keepdims=True)
        m2 = jnp.max(mt2, axis=-1, keepdims=True)
        lt1 = jnp.zeros((ts, LANES), F32)
        lt2 = jnp.zeros((ts, LANES), F32)
        o1 = jnp.zeros((ts, 2 * DIFF_HD), F32)
        o2 = jnp.zeros((ts, 2 * DIFF_HD), F32)
        for i, c in enumerate(range(0, seq, chunk)):
            p1, p2 = jnp.exp2(t1[i] - m1), jnp.exp2(t2[i] - m2)
            lt1 = lt1 + _lane_fold(p1, jnp.add)
            lt2 = lt2 + _lane_fold(p2, jnp.add)
            o1 = o1 + _dot(p1.astype(BF16), v_ref[c:c + chunk, :])
            o2 = o2 + _dot(p2.astype(BF16), v_ref[c:c + chunk, :])
        l1 = jnp.sum(lt1, axis=-1, keepdims=True)
        l2 = jnp.sum(lt2, axis=-1, keepdims=True)
        o = o1 * (1.0 / l1) - o2 * (lam / l2)
        r = lax.rsqrt(jnp.mean(o * o, axis=-1, keepdims=True) + EPS)
        o_ref[rows, :] = (o * r * g_ref[...] * (1.0 - lam_init)).astype(o_ref.dtype)


def diff_attention(mix, cols, slopes, lam_p, subln_g, B, S, layer):
    tq = _pick(S, (512, 256, 128))
    lam_init = 0.8 - 0.6 * math.exp(-0.3 * layer)
    W = 2 * DIFF_HD
    qb, kb, vb = (cols[n] // W for n in ("fq", "fk", "fv"))
    m3 = mix.reshape(B, S, mix.shape[1])
    pos = jnp.arange(S, dtype=F32).reshape(1, S)
    kern = functools.partial(_diff_kernel, tq=tq, lam_init=lam_init, chunk=_pick(S, (512, 256)),
                             n_sub=4 if S > 2048 else 2)
    out = pl.pallas_call(
        kern,
        grid=(B, DIFF_HEADS, S // tq),
        in_specs=[pl.BlockSpec(memory_space=pltpu.SMEM),
                  pl.BlockSpec((4, DIFF_HD), lambda b, h, i: (0, 0)),
                  pl.BlockSpec((None, tq, W), lambda b, h, i: (b, i, qb + h)),
                  pl.BlockSpec((None, S, W), lambda b, h, i: (b, 0, kb + h)),
                  pl.BlockSpec((None, S, W), lambda b, h, i: (b, 0, vb + h)),
                  pl.BlockSpec((1, S), lambda b, h, i: (0, 0)),
                  pl.BlockSpec((1, W), lambda b, h, i: (0, 0))],
        out_specs=pl.BlockSpec((None, tq, W), lambda b, h, i: (b, i, h)),
        out_shape=jax.ShapeDtypeStruct((B, S, DIFF_HEADS * W), BF16),
        compiler_params=_params("parallel", "parallel", "arbitrary"),
        name="diff_attention",
    )((slopes * LOG2E).astype(F32), lam_p.astype(F32), m3, m3, m3, pos, subln_g.reshape(1, W).astype(F32))
    return out.reshape(B * S, DIFF_HEADS * W)


def _mem_attn_kernel(x_ref, gm_ref, wq_ref, gq_ref, kv_ref, wo_ref, gf_ref, o_ref, hf_ref):
    def rms(t):
        return lax.rsqrt(jnp.mean(t * t, axis=-1, keepdims=True) + EPS)

    x = x_ref[...]
    hm = (x * rms(x) * gm_ref[...]).astype(BF16)
    qacc = _dot(hm, wq_ref[...])
    heads = []
    for h in range(MEM_HEADS):
        seg = slice(h * HEAD_DIM, (h + 1) * HEAD_DIM)
        y = qacc[:, seg]
        q = (y * rms(y) * gq_ref[:, seg]).astype(BF16)
        s = _qk(q, kv_ref[:, seg])
        p = jnp.exp2(s - jnp.max(s, axis=-1, keepdims=True))
        l = jnp.sum(p, axis=-1, keepdims=True)
        vseg = slice(MEM_W + h * HEAD_DIM, MEM_W + (h + 1) * HEAD_DIM)
        heads.append((_dot(p.astype(BF16), kv_ref[:, vseg]) / l).astype(BF16))
    x2 = x + _dot(jnp.concatenate(heads, axis=-1), wo_ref[...])
    o_ref[...] = x2
    hf_ref[...] = (x2 * rms(x2) * gf_ref[...]).astype(hf_ref.dtype)


def mem_attention(x, kv, lw, B, S):
    D = x.shape[1]
    Mt = kv.shape[0] // B
    tq = _pick(S, (256, 128))
    kv3 = kv.reshape(B, Mt, 2 * MEM_W)
    x3 = x.reshape(B, S, D)

    def row(n):
        return pl.BlockSpec((1, n), lambda b, i: (0, 0))

    out, hf = pl.pallas_call(
        _mem_attn_kernel,
        grid=(B, S // tq),
        in_specs=[pl.BlockSpec((None, tq, D), lambda b, i: (b, i, 0)),
                  row(D),
                  pl.BlockSpec((D, MEM_W), lambda b, i: (0, 0)),
                  row(MEM_W),
                  pl.BlockSpec((None, Mt, 2 * MEM_W), lambda b, i: (b, 0, 0)),
                  pl.BlockSpec((MEM_W, D), lambda b, i: (0, 0)),
                  row(D)],
        out_specs=[pl.BlockSpec((None, tq, D), lambda b, i: (b, i, 0)),
                   pl.BlockSpec((None, tq, D), lambda b, i: (b, i, 0))],
        out_shape=[jax.ShapeDtypeStruct((B, S, D), F32), jax.ShapeDtypeStruct((B, S, D), BF16)],
        compiler_params=_params("parallel", "arbitrary"),
        name="mem_attention",
    )(x3, lw["ln_mem_g"].reshape(1, D).astype(F32), lw["mem_wq"], lw["mem_q_gain"].reshape(1, MEM_W),
      kv3, lw["mem_wo"], lw["ln_ffn_g"].reshape(1, D).astype(F32))
    return out.reshape(B * S, D), hf.reshape(B * S, D)


def alibi_slopes(n):
    return 2.0 ** (-8.0 * jnp.arange(1, n + 1, dtype=F32) / n)


MIX_NAMES = ("qa", "kva", "kpe", "dq", "dk", "dv", "wq", "wk", "wv", "fq", "fk", "fv")


def _mix_layout():
    src = {n: (MIX_OFFSETS[i], MIX_OFFSETS[i + 1]) for i, n in enumerate(MIX_NAMES)}
    head = -(-(src["kpe"][0] + LANES) // MIX_TILE) * MIX_TILE
    shift = head - src["dq"][0]
    cols = {n: src[n][0] + (0 if n in ("qa", "kva", "kpe") else shift) for n in MIX_NAMES}
    total = -(-(MIX_COLS + shift) // MIX_TILE) * MIX_TILE
    return src, cols, head, shift, total


def _pack_kernel(src3_ref, o_ref, *, n_head, kpe_off, tn):
    src_ref = src3_ref.at[0]
    j = pl.program_id(0)
    if not n_head:
        o_ref[...] = src_ref[...].astype(o_ref.dtype)
        return

    @pl.when(j != n_head - 1)
    def _():
        o_ref[...] = src_ref[...].astype(o_ref.dtype)

    @pl.when(j == n_head - 1)
    def _():
        quarter = MLA_ROPE // 2
        x1 = src_ref[kpe_off:kpe_off + quarter, :].astype(o_ref.dtype)
        x2 = src_ref[kpe_off + quarter:kpe_off + 2 * quarter, :].astype(o_ref.dtype)
        if kpe_off:
            o_ref[:kpe_off, :] = src_ref[:kpe_off, :].astype(o_ref.dtype)
        for t, piece in enumerate((x1, x2, x2, x1)):
            o_ref[kpe_off + t * quarter:kpe_off + (t + 1) * quarter, :] = piece
        if kpe_off + LANES < tn:
            o_ref[kpe_off + LANES:, :] = jnp.zeros((tn - kpe_off - LANES, o_ref.shape[1]), o_ref.dtype)


def _cast_kernel(x_ref, o_ref):
    o_ref[...] = x_ref[...].astype(o_ref.dtype)


def cast_layer(w, l):
    _, R, C = w.shape
    tc = _pick(C, (2048, 1024, 512, 256, 128))
    tr = _pick(R, tuple(t for t in (2048, 1024, 512, 256, 128, 64, 8) if t * tc <= 2 ** 21))
    return pl.pallas_call(
        _cast_kernel,
        grid=(R // tr, C // tc),
        in_specs=[pl.BlockSpec((None, tr, tc), lambda i, j: (l, i, j))],
        out_specs=pl.BlockSpec((tr, tc), lambda i, j: (i, j)),
        out_shape=jax.ShapeDtypeStruct((R, C), BF16),
        compiler_params=_params("parallel", "arbitrary"),
        name="cast_layer",
    )(w)


def pack_shifted(src_t, l, n_out, shift, head_rows=0, kpe_row=0):
    _, C, K = src_t.shape
    tn = MIX_TILE
    n_head = head_rows // tn
    assert shift % 8 == 0 and head_rows % tn == 0 and (n_head == 0 or kpe_row // tn == n_head - 1)
    kern = functools.partial(_pack_kernel, n_head=n_head, kpe_off=kpe_row % tn, tn=tn)
    return pl.pallas_call(
        kern,
        grid=(n_out // tn,),
        in_specs=[pl.BlockSpec((pl.Element(1), pl.Element(tn), pl.Element(K)),
                               lambda j: (l, pl.multiple_of(jnp.where(j < n_head, j * tn, j * tn - shift), 8), 0))],
        out_specs=pl.BlockSpec((tn, K), lambda j: (j, 0)),
        out_shape=jax.ShapeDtypeStruct((n_out, K), BF16),
        compiler_params=_params("arbitrary"),
        name="pack_shifted",
    )(src_t)


def pack_layer(p, l):
    src, cols, head_w, shift, total = _mix_layout()
    half = MLA_ROPE // 2
    qscale = HEAD_DIM ** -0.5 * LOG2E

    def tile(g, n):
        return jnp.tile(g.astype(F32), n)

    norm_gain = {
        "dq": tile(p["dil_qk_g"][l, 0], DIL_HEADS) * qscale, "dk": tile(p["dil_qk_g"][l, 1], DIL_HEADS),
        "fq": tile(p["diff_qk_g"][l, 0], 2 * DIFF_HEADS) * (DIFF_HD ** -0.5 * LOG2E),
        "fk": tile(p["diff_qk_g"][l, 1], 2 * DIFF_HEADS),
        "wq": tile(p["win_qk_g"][l, 0], WIN_Q_HEADS) * qscale, "wk": tile(p["win_qk_g"][l, 1], WIN_KV_HEADS),
    }
    gain = jnp.ones((total,), F32)
    flag = np.zeros((total,), np.float32)
    for n, g in norm_gain.items():
        gain = gain.at[cols[n]:cols[n] + g.shape[0]].set(g)
        flag[cols[n]:cols[n] + g.shape[0]] = 1.0

    w_in_t = jnp.swapaxes(p["w_in"], 1, 2)
    w_mix = pack_shifted(w_in_t, l, total, shift, head_rows=head_w, kpe_row=src["kpe"][0])
    w_gate = pack_shifted(w_in_t, l, N_BRANCH * D_MODEL, -MIX_COLS)

    wq3 = p["mla_wq_up"][l].reshape(MLA_Q_LORA, MLA_HEADS, MLA_QK)
    nope, x1, x2 = wq3[:, :, :MLA_NOPE], wq3[:, :, MLA_NOPE:MLA_NOPE + half], wq3[:, :, MLA_NOPE + half:]
    z = jnp.zeros((MLA_Q_LORA, MLA_HEADS, LANES - MLA_ROPE), F32)
    wq_full = jnp.concatenate([nope, x1, x2, z], axis=-1).reshape(MLA_Q_LORA, MLA_HEADS * MLA_QK_PAD)
    wq_sw = jnp.concatenate([x2, x1, z], axis=-1).reshape(MLA_Q_LORA, MLA_HEADS * LANES)
    wkv3 = p["mla_wkv_up"][l].reshape(MLA_KV_LORA, MLA_HEADS, MLA_NOPE + MLA_V)
    zg = jnp.zeros((LANES - MLA_ROPE,), F32)
    qk_g = p["mla_qk_g"][l].astype(F32)

    return {
        "ln_mix_g": p["ln_mix_g"][l],
        "w_mix": w_mix, "w_gate": w_gate, "mix_gain": gain, "mix_flag": flag,
        "mla_qa_g": p["mla_qa_g"][l].reshape(1, -1).astype(F32),
        "mla_kva_g": p["mla_kva_g"][l].reshape(1, -1).astype(F32),
        "mla_wq": wq_full.astype(BF16), "mla_wq_sw": wq_sw.astype(BF16),
        "mla_wk": wkv3[:, :, :MLA_NOPE].reshape(MLA_KV_LORA, -1).astype(BF16),
        "mla_wv": wkv3[:, :, MLA_NOPE:].reshape(MLA_KV_LORA, -1).astype(BF16),
        "mla_gq": (jnp.concatenate([qk_g[0], zg]) * (MLA_QK ** -0.5 * LOG2E)).reshape(1, -1),
        "mla_gk": jnp.concatenate([qk_g[1], zg]).reshape(1, -1),
        "win_sink": p["win_sink"][l], "diff_lambda": p["diff_lambda"][l], "diff_subln_g": p["diff_subln_g"][l],
        "w_branch": cast_layer(p["w_branch"].reshape(DEPTH, N_BRANCH * BRANCH_W, D_MODEL), l).reshape(
            N_BRANCH, BRANCH_W, D_MODEL),
        "w_out": cast_layer(p["w_out"], l),
        "ln_mem_g": p["ln_mem_g"][l], "mem_ln_g": p["mem_ln_g"][l],
        "mem_wq": cast_layer(p["mem_wq"], l), "mem_wkv": cast_layer(p["mem_wkv"], l),
        "mem_q_gain": tile(p["mem_qk_g"][l, 0], MEM_HEADS) * (HEAD_DIM ** -0.5 * LOG2E),
        "mem_kv_gain": jnp.concatenate([tile(p["mem_qk_g"][l, 1], MEM_HEADS), jnp.ones((MEM_W,), F32)]),
        "mem_kv_flag": np.concatenate([np.ones((MEM_W,), np.float32), np.zeros((MEM_W,), np.float32)]),
        "mem_wo": cast_layer(p["mem_wo"], l),
        "ln_ffn_g": p["ln_ffn_g"][l],
        "ffn_w_in": cast_layer(p["ffn_w_in"], l), "ffn_w_out": cast_layer(p["ffn_w_out"], l),
    }


def rotary_tables(S):
    half = MLA_ROPE // 2
    inv_freq = ROPE_THETA ** (-jnp.arange(half, dtype=F32) / half)
    ang = jnp.arange(S, dtype=F32)[:, None] * inv_freq[None, :]
    cos, sin = jnp.cos(ang), jnp.sin(ang)
    z = jnp.zeros((S, LANES - MLA_ROPE), F32)
    return jnp.concatenate([cos, cos, z], axis=1), jnp.concatenate([-sin, sin, z], axis=1)


def _trunk(x, mem, layers, tables):
    B, S, D = x.shape
    M = B * S
    _, cols, _, _, _ = _mix_layout()
    ct, st = rotary_tables(S)
    xf = x.reshape(M, D)
    memf = mem.reshape(-1, D)
    dil_tq = _pick(S, (256,))
    for l, lw in enumerate(layers):
        h = rmsnorm(xf, lw["ln_mix_g"])
        mix = mm_segnorm(h, lw["w_mix"], lw["mix_gain"], lw["mix_flag"], w_output_major=True,
                         tm_prefs=(2048, 1024, 512, 256), tn_prefs=(MIX_TILE,))
        gates = mm_sigmoid(h, lw["w_gate"])

        q, k, v = mla_prep(mix, cols, S, ct, st, lw)
        o_mla = dense_attention(q, k, v, B, S, MLA_HEADS, MLA_QK_PAD, MLA_V)
        o_dil = band_attention(mix, cols["dq"], mix, cols["dk"], mix, cols["dv"], tables["dil_bias"],
                               jnp.full((DIL_HEADS,), NEG_INF, F32), B, S, DIL_HEADS, 1, DIL_HEADS // 2,
                               dil_tq, dil_tq, DIL_RADIUS)
        o_win = band_attention(mix, cols["wq"], mix, cols["wk"], mix, cols["wv"], tables["win_bias"],
                               lw["win_sink"], B, S, WIN_Q_HEADS, WIN_Q_HEADS // WIN_KV_HEADS, WIN_Q_HEADS,
                               2 * WIN_RADIUS, WIN_RADIUS, WIN_RADIUS)
        o_diff = diff_attention(mix, cols, tables["slopes_diff"], lw["diff_lambda"], lw["diff_subln_g"], B, S, l)

        merged = branch_merge((o_mla, o_dil, o_win, o_diff), lw["w_branch"], gates)
        xf = mm_residual(merged, lw["w_out"], xf, tn_prefs=(1024, 512, 256, 128))

        kvm = mm_segnorm(rmsnorm(memf, lw["mem_ln_g"]), lw["mem_wkv"], lw["mem_kv_gain"], lw["mem_kv_flag"])
        xf, hf = mem_attention(xf, kvm, lw, B, S)

        hid = mm_swiglu(hf, lw["ffn_w_in"], FFN_HIDDEN)
        xf = mm_residual(hid, lw["ffn_w_out"], xf, tm_prefs=(512, 256, 128), tn_prefs=(512, 256, 128))
    return xf.reshape(B, S, D)


def kernel(x_prompt, x_sample, mem_prompt, mem_sample, ln_mix_g, w_in, mla_qa_g, mla_kva_g, mla_wq_up,
           mla_wkv_up, mla_qk_g, dil_qk_g, win_qk_g, win_sink, diff_qk_g, diff_lambda, diff_subln_g,
           w_branch, w_out, ln_mem_g, mem_ln_g, mem_wq, mem_wkv, mem_qk_g, mem_wo, ln_ffn_g, ffn_w_in,
           ffn_w_out):
    p = dict(ln_mix_g=ln_mix_g, w_in=w_in, mla_qa_g=mla_qa_g, mla_kva_g=mla_kva_g, mla_wq_up=mla_wq_up,
             mla_wkv_up=mla_wkv_up, mla_qk_g=mla_qk_g, dil_qk_g=dil_qk_g, win_qk_g=win_qk_g,
             win_sink=win_sink, diff_qk_g=diff_qk_g, diff_lambda=diff_lambda, diff_subln_g=diff_subln_g,
             w_branch=w_branch, w_out=w_out, ln_mem_g=ln_mem_g, mem_ln_g=mem_ln_g, mem_wq=mem_wq,
             mem_wkv=mem_wkv, mem_qk_g=mem_qk_g, mem_wo=mem_wo, ln_ffn_g=ln_ffn_g, ffn_w_in=ffn_w_in,
             ffn_w_out=ffn_w_out)
    layers = [pack_layer(p, l) for l in range(DEPTH)]
    tables = {
        "dil_bias": band_bias(alibi_slopes(DIL_HEADS), dil_multiplicity, 256, 256, DIL_RADIUS),
        "win_bias": band_bias(alibi_slopes(WIN_Q_HEADS), win_multiplicity, 2 * WIN_RADIUS, WIN_RADIUS,
                              WIN_RADIUS),
        "slopes_diff": alibi_slopes(DIFF_HEADS),
    }
    y_prompt = _trunk(x_prompt, mem_prompt, layers, tables)
    y_sample = _trunk(x_sample, mem_sample, layers, tables)
    return (y_prompt, y_sample)
```
